```python
import math
import jax
import jax.numpy as jnp
from jax import lax
import numpy as np

D_MODEL = 1024
BATCH = 8
SEQ = 8192
DEPTH = 4
DEC_BATCH = 32
DEC_SEQ = 64
PAST_LEN = 2048

CHUNK = 64
Q_BLOCK = 128
N_MIXERS = 4
NORM_EPS = 1e-6
NEG_INF = -1e30

N_HEADS = 16
HEAD_DIM = 64
HEAD_SCALE = HEAD_DIM ** -0.5
N_BUCKETS = 32
MAX_DISTANCE = 128

A_Q_LORA = 512
A_KV_LORA = 256
A_NOPE = 64
A_ROPE = 32
A_V = 64
A_SCALE = (A_NOPE + A_ROPE) ** -0.5
ROPE_THETA = 10000.0

C_IDX_HEADS = 8
C_IDX_DIM = 64
C_TOPK_MAX = 256

D_KV_HEADS = 2
D_REP = N_HEADS // D_KV_HEADS
WINDOW = 128
N_WIN_CHUNKS = WINDOW // CHUNK

N_EXPERTS = 16
N_GROUPS = 4
EXPERTS_PER_GROUP = N_EXPERTS // N_GROUPS
TOP_K = 2
D_EXPERT = 256

A_IN_SIZES = (A_Q_LORA, A_KV_LORA, A_ROPE)
B_IN_SIZES = (N_HEADS * HEAD_DIM, N_HEADS * HEAD_DIM, N_HEADS * HEAD_DIM, N_HEADS)
C_IN_SIZES = (N_HEADS * HEAD_DIM, HEAD_DIM, HEAD_DIM, C_IDX_HEADS * C_IDX_DIM, C_IDX_DIM, C_IDX_HEADS)
D_IN_SIZES = (N_HEADS * HEAD_DIM, D_KV_HEADS * HEAD_DIM, D_KV_HEADS * HEAD_DIM)

kernel_name = 'chunk_streaming_hybrid_encoder_step'

F32 = jnp.float32


def split_cols(a, sizes):
    return jnp.split(a, [int(s) for s in np.cumsum(sizes)[:-1]], axis=-1)


def rmsnorm(x, g):
    x32 = x.astype(F32)
    y = x32 * lax.rsqrt(jnp.mean(x32 * x32, axis=-1, keepdims=True) + NORM_EPS)
    return (y * g.astype(F32)).astype(x.dtype)


def adaln(c, w, b):
    m = jax.nn.silu(c) @ w + b
    return jnp.split(m[:, None, :], 6, axis=-1)


def rel_bucket(rel):
    nb = N_BUCKETS // 2
    max_exact = nb // 2
    n = jnp.abs(rel)
    far = max_exact + (jnp.log(jnp.maximum(n, 1).astype(F32) / max_exact)
                       / math.log(MAX_DISTANCE / max_exact) * (nb - max_exact)).astype(jnp.int32)
    far = jnp.minimum(far, nb - 1)
    return jnp.where(rel > 0, nb, 0) + jnp.where(n < max_exact, n, far)


def head_bias(rel, table):
    return jnp.moveaxis(table[rel_bucket(rel)].astype(F32), -1, 0)


def rope(x, pos):
    half = x.shape[-1] // 2
    inv = ROPE_THETA ** (-jnp.arange(half, dtype=F32) / half)
    ang = pos.astype(F32)[:, None] * inv[None, :]
    cos = jnp.cos(ang)[:, None, :]
    sin = jnp.sin(ang)[:, None, :]
    x32 = x.astype(F32)
    x1, x2 = x32[..., :half], x32[..., half:]
    return jnp.concatenate([x1 * cos - x2 * sin, x2 * cos + x1 * sin], axis=-1).astype(x.dtype)


def attn_block(q, k, v, q_pos, k_pos, scale, per_frame, q_dec=None, k_dec=None):
    logits = jnp.einsum('bqhd,bkhd->bhqk', q, k, preferred_element_type=F32) * scale
    if per_frame:
        mask = k_pos[None, :] <= q_pos[:, None]
    else:
        mask = (k_pos[None, :] // CHUNK) <= (q_pos[:, None] // CHUNK)
    if q_dec is not None:
        logits = logits + (jnp.swapaxes(q_dec, 1, 2)[..., :, None] - jnp.swapaxes(k_dec, 1, 2)[..., None, :])
    p = jax.nn.softmax(jnp.where(mask, logits, NEG_INF), axis=-1)
    return jnp.einsum('bhqk,bkhd->bqhd', p.astype(v.dtype), v)


def blocked_dense_attention(q, k, v, scale, per_frame, dec=None):
    B, S = q.shape[0], q.shape[1]
    k_pos = jnp.arange(S)

    def one_block(i):
        start = i * Q_BLOCK
        qb = lax.dynamic_slice_in_dim(q, start, Q_BLOCK, axis=1)
        qd = None if dec is None else lax.dynamic_slice_in_dim(dec, start, Q_BLOCK, axis=1)
        return attn_block(qb, k, v, start + jnp.arange(Q_BLOCK), k_pos, scale, per_frame, qd, dec)

    o = lax.map(one_block, jnp.arange(S // Q_BLOCK))
    return jnp.moveaxis(o, 0, 1).reshape(B, S, -1)


def mla_project(h, pos, w_in, g_q, w_uq, g_kv):
    B, L, _ = h.shape
    cq, ckv, kr = split_cols(h @ w_in, A_IN_SIZES)
    q = (rmsnorm(cq, g_q) @ w_uq).reshape(B, L, N_HEADS, A_NOPE + A_ROPE)
    q = jnp.concatenate([q[..., :A_NOPE], rope(q[..., A_NOPE:], pos)], axis=-1)
    latent = rmsnorm(ckv, g_kv)
    k_rope = rope(kr[:, :, None, :], pos)[:, :, 0, :]
    return q, latent, k_rope


def mla_expand(latent, k_rope, w_ukv):
    B, L, _ = latent.shape
    kv = (latent @ w_ukv).reshape(B, L, N_HEADS, A_NOPE + A_V)
    k = jnp.concatenate([kv[..., :A_NOPE],
                         jnp.broadcast_to(k_rope[:, :, None, :], (B, L, N_HEADS, A_ROPE)).astype(kv.dtype)], axis=-1)
    return k, kv[..., A_NOPE:]


def mla_prompt(h, w_in, g_q, w_uq, g_kv, w_ukv, w_o):
    B, S, _ = h.shape
    q, lat, kr = mla_project(h, jnp.arange(S), w_in, g_q, w_uq, g_kv)
    k, v = mla_expand(lat, kr, w_ukv)
    o = blocked_dense_attention(q, k, v, A_SCALE, False)
    return o @ w_o, lat, kr


def mla_sample(h, cache_lat, cache_kr, w_in, g_q, w_uq, g_kv, w_ukv, w_o):
    B, Ls, _ = h.shape
    P = cache_lat.shape[1]
    q_pos = P + jnp.arange(Ls)
    q, lat, kr = mla_project(h, q_pos, w_in, g_q, w_uq, g_kv)
    k, v = mla_expand(jnp.concatenate([cache_lat, lat], axis=1), jnp.concatenate([cache_kr, kr], axis=1), w_ukv)
    o = attn_block(q, k, v, q_pos, jnp.arange(P + Ls), A_SCALE, False)
    return o.reshape(B, Ls, -1) @ w_o, lat, kr


def fox_project(h, w_in, b_f):
    B, L, _ = h.shape
    q, k, v, f = split_cols(h @ w_in, B_IN_SIZES)
    shp = (B, L, N_HEADS, HEAD_DIM)
    logf = jax.nn.log_sigmoid(f.astype(F32) + b_f.astype(F32))
    return q.reshape(shp), k.reshape(shp), v.reshape(shp), logf


def fox_prompt(h, w_in, b_f, w_o):
    q, k, v, logf = fox_project(h, w_in, b_f)
    cum = jnp.cumsum(logf, axis=1)
    o = blocked_dense_attention(q, k, v, HEAD_SCALE, True, cum)
    return o @ w_o, k, v, logf


def fox_sample(h, cache_k, cache_v, cache_logf, w_in, b_f, w_o):
    B, Ls, _ = h.shape
    P = cache_k.shape[1]
    q, k, v, logf = fox_project(h, w_in, b_f)
    cum = jnp.cumsum(jnp.concatenate([cache_logf.astype(F32), logf], axis=1), axis=1)
    o = attn_block(q, jnp.concatenate([cache_k, k], axis=1), jnp.concatenate([cache_v, v], axis=1),
                   P + jnp.arange(Ls), jnp.arange(P + Ls), HEAD_SCALE, True, cum[:, P:], cum)
    return o.reshape(B, Ls, -1) @ w_o, k, v, logf


def dsa_project(h, w_in):
    B, L, _ = h.shape
    q, k, v, qi, ki, wi = split_cols(h @ w_in, C_IN_SIZES)
    return (q.reshape(B, L, N_HEADS, HEAD_DIM), k, v, qi.reshape(B, L, C_IDX_HEADS, C_IDX_DIM), ki,
            wi.astype(F32) * (C_IDX_HEADS ** -0.5))


def dsa_attend(q, qi, wi, q_pos, k, v, ki, table, n_sel):
    dots = jnp.einsum('bqhd,bkd->bqhk', qi, ki, preferred_element_type=F32) * (C_IDX_DIM ** -0.5)
    score = jnp.einsum('bqhk,bqh->bqk', jax.nn.relu(dots), wi)
    k_pos = jnp.arange(k.shape[1])
    adm = (k_pos[None, :] // CHUNK) <= (q_pos[:, None] // CHUNK)
    _, sel = lax.top_k(jnp.where(adm[None], score, NEG_INF), n_sel)
    valid = (sel // CHUNK) <= (q_pos[None, :, None] // CHUNK)
    take = jax.vmap(lambda rows, idx: rows[idx])
    k_sel = take(k, sel)
    v_sel = take(v, sel)
    logits = jnp.einsum('bqhd,bqnd->bhqn', q, k_sel, preferred_element_type=F32) * HEAD_SCALE
    logits = logits + jnp.swapaxes(head_bias(sel - q_pos[None, :, None], table), 0, 1)
    p = jax.nn.softmax(jnp.where(valid[:, None], logits, NEG_INF), axis=-1)
    return jnp.einsum('bhqn,bqnd->bqhd', p.astype(v_sel.dtype), v_sel)


def dsa_prompt(h, w_in, w_o, table):
    B, S, _ = h.shape
    q, k, v, qi, ki, wi = dsa_project(h, w_in)
    n_sel = min(C_TOPK_MAX, S // 4)

    def one_block(i):
        start = i * Q_BLOCK
        sl = lambda a: lax.dynamic_slice_in_dim(a, start, Q_BLOCK, axis=1)
        return dsa_attend(sl(q), sl(qi), sl(wi), start + jnp.arange(Q_BLOCK), k, v, ki, table, n_sel)

    o = lax.map(one_block, jnp.arange(S // Q_BLOCK))
    o = jnp.moveaxis(o, 0, 1).reshape(B, S, -1)
    return o @ w_o, k, v, ki


def dsa_sample(h, cache_k, cache_v, cache_ki, w_in, w_o, table):
    B, Ls, _ = h.shape
    P = cache_k.shape[1]
    q, k, v, qi, ki, wi = dsa_project(h, w_in)
    n_sel = min(C_TOPK_MAX, (P + Ls) // 4)
    o = dsa_attend(q, qi, wi, P + jnp.arange(Ls), jnp.concatenate([cache_k, k], axis=1),
                   jnp.concatenate([cache_v, v], axis=1), jnp.concatenate([cache_ki, ki], axis=1), table, n_sel)
    return o.reshape(B, Ls, -1) @ w_o, k, v, ki


def swa_project(h, w_in):
    B, L, _ = h.shape
    q, k, v = split_cols(h @ w_in, D_IN_SIZES)
    return (q.reshape(B, L, D_KV_HEADS, D_REP, HEAD_DIM), k.reshape(B, L, D_KV_HEADS, HEAD_DIM),
            v.reshape(B, L, D_KV_HEADS, HEAD_DIM))


def sink_softmax(logits, sink):
    m = jnp.maximum(jnp.max(logits, axis=-1, keepdims=True), sink)
    e = jnp.exp(logits - m)
    return e / (jnp.sum(e, axis=-1, keepdims=True) + jnp.exp(sink - m))


def swa_prompt(h, w_in, sinks, w_o, table):
    B, S, _ = h.shape
    q, k, v = swa_project(h, w_in)
    nC = S // CHUNK
    back = N_WIN_CHUNKS * CHUNK
    band = back + CHUNK

    def banded(a):
        a = jnp.pad(a, ((0, 0), (back, 0), (0, 0), (0, 0))).reshape(B, nC + N_WIN_CHUNKS, CHUNK, *a.shape[2:])
        return jnp.concatenate([a[:, j:j + nC] for j in range(N_WIN_CHUNKS + 1)], axis=2)

    kb, vb = banded(k), banded(v)
    qc = q.reshape(B, nC, CHUNK, D_KV_HEADS, D_REP, HEAD_DIM)
    logits = jnp.einsum('bcqgrd,bckgd->bcgrqk', qc, kb, preferred_element_type=F32) * HEAD_SCALE
    rel = (jnp.arange(band) - back)[None, :] - jnp.arange(CHUNK)[:, None]
    logits = logits + head_bias(rel, table).reshape(D_KV_HEADS, D_REP, CHUNK, band)
    kpos = (jnp.arange(nC)[:, None] - N_WIN_CHUNKS) * CHUNK + jnp.arange(band)[None, :]
    logits = jnp.where((kpos >= 0)[None, :, None, None, None, :], logits, NEG_INF)
    p = sink_softmax(logits, sinks.astype(F32).reshape(D_KV_HEADS, D_REP, 1, 1))
    o = jnp.einsum('bcgrqk,bckgd->bcqgrd', p.astype(vb.dtype), vb).reshape(B, S, -1)
    keep = min(WINDOW, S)
    return o @ w_o, k[:, S - keep:], v[:, S - keep:]


def swa_sample(h, cache_k, cache_v, w_in, sinks, w_o, table):
    B, Ls, _ = h.shape
    Wc = cache_k.shape[1]
    q, k, v = swa_project(h, w_in)
    k_all = jnp.concatenate([cache_k, k], axis=1)
    v_all = jnp.concatenate([cache_v, v], axis=1)
    q_pos = PAST_LEN + jnp.arange(Ls)
    k_pos = PAST_LEN - Wc + jnp.arange(Wc + Ls)
    logits = jnp.einsum('bqgrd,bkgd->bgrqk', q, k_all, preferred_element_type=F32) * HEAD_SCALE
    logits = logits + head_bias(k_pos[None, :] - q_pos[:, None], table).reshape(D_KV_HEADS, D_REP, Ls, Wc + Ls)
    p = sink_softmax(logits, sinks.astype(F32).reshape(D_KV_HEADS, D_REP, 1, 1))
    o = jnp.einsum('bgrqk,bkgd->bqgrd', p.astype(v_all.dtype), v_all).reshape(B, Ls, -1)
    return o @ w_o, k_all[:, Ls:], v_all[:, Ls:]


def route(t, w_router, b_router):
    T = t.shape[0]
    s = jax.nn.sigmoid((t @ w_router).astype(F32))
    sb = s + b_router.astype(F32)
    grp = jnp.sum(lax.top_k(sb.reshape(T, N_GROUPS, EXPERTS_PER_GROUP), TOP_K)[0], axis=-1)
    g = jnp.argmax(grp, axis=-1)
    in_grp = (jnp.arange(N_EXPERTS) // EXPERTS_PER_GROUP)[None, :] == g[:, None]
    _, idx = lax.top_k(jnp.where(in_grp, sb, NEG_INF), TOP_K)
    w = jnp.take_along_axis(s, idx, axis=-1)
    w = w / jnp.sum(w, axis=-1, keepdims=True)
    return jnp.sum(jax.nn.one_hot(idx, N_EXPERTS, dtype=F32) * w[..., None], axis=1)


def moe(h, w_router, b_router, w_gate, w_up, w_down):
    B, L, D = h.shape
    t = h.reshape(B * L, D)
    gates = route(t, w_router, b_router).astype(h.dtype)
    y = jnp.zeros_like(t)
    for e in range(N_EXPERTS):
        y = y + gates[:, e:e + 1] * ((jax.nn.silu(t @ w_gate[e]) * (t @ w_up[e])) @ w_down[e])
    return y.reshape(B, L, D)


def setup_inputs(seed: int = 0) -> dict:
    key = jax.random.key(seed)
    keys = iter(jax.random.split(key, 64))

    def nrm(shape, scale):
        return jax.random.normal(next(keys), shape, jnp.float32) * scale

    def gain(shape):
        return 1.0 + nrm(shape, 0.02)

    D = D_MODEL
    H, hd = N_HEADS, HEAD_DIM
    P = PAST_LEN
    Wc = min(WINDOW, PAST_LEN)
    return {
        'x_prompt': nrm((BATCH, SEQ, D), 1.0),
        'x_sample': nrm((DEC_BATCH, DEC_SEQ, D), 1.0),
        'c_prompt': nrm((BATCH, D), 1.0),
        'c_sample': nrm((DEC_BATCH, D), 1.0),
        'cache_a_latent': nrm((DEC_BATCH, P, A_KV_LORA), 1.0),
        'cache_a_krope': nrm((DEC_BATCH, P, A_ROPE), 1.0),
        'cache_b_k': nrm((DEC_BATCH, P, H, hd), 1.0),
        'cache_b_v': nrm((DEC_BATCH, P, H, hd), 1.0),
        'cache_b_logf': jax.nn.log_sigmoid(2.0 + nrm((DEC_BATCH, P, H), 1.0)),
        'cache_c_k': nrm((DEC_BATCH, P, hd), 1.0),
        'cache_c_v': nrm((DEC_BATCH, P, hd), 1.0),
        'cache_c_kidx': nrm((DEC_BATCH, P, C_IDX_DIM), 1.0),
        'cache_d_k': nrm((DEC_BATCH, Wc, D_KV_HEADS, hd), 1.0),
        'cache_d_v': nrm((DEC_BATCH, Wc, D_KV_HEADS, hd), 1.0),
        'w_ada': nrm((DEPTH, D, 6 * D), 0.5 * D ** -0.5),
        'b_ada': nrm((DEPTH, 6 * D), 0.02),
        'g_mix': gain((DEPTH, D)),
        'g_ffn': gain((DEPTH, D)),
        'g_final': gain((D,)),
        'rel_bias': nrm((N_BUCKETS, H), 0.5),
        'a_w_in': nrm((D, sum(A_IN_SIZES)), D ** -0.5),
        'a_g_q': gain((A_Q_LORA,)),
        'a_w_uq': nrm((A_Q_LORA, H * (A_NOPE + A_ROPE)), A_Q_LORA ** -0.5),
        'a_g_kv': gain((A_KV_LORA,)),
        'a_w_ukv': nrm((A_KV_LORA, H * (A_NOPE + A_V)), A_KV_LORA ** -0.5),
        'a_w_o': nrm((H * A_V, D), (H * A_V) ** -0.5),
        'b_w_in': nrm((D, sum(B_IN_SIZES)), D ** -0.5),
        'b_b_f': 2.0 + nrm((H,), 0.5),
        'b_w_o': nrm((H * hd, D), (H * hd) ** -0.5),
        'c_w_in': nrm((D, sum(C_IN_SIZES)), D ** -0.5),
        'c_w_o': nrm((H * hd, D), (H * hd) ** -0.5),
        'd_w_in': nrm((D, sum(D_IN_SIZES)), D ** -0.5),
        'd_sinks': nrm((H,), 0.5),
        'd_w_o': nrm((H * hd, D), (H * hd) ** -0.5),
        'moe_w_router': nrm((D, N_EXPERTS), D ** -0.5),
        'moe_b_router': nrm((N_EXPERTS,), 0.01),
        'moe_w_gate': nrm((DEPTH, N_EXPERTS, D, D_EXPERT), D ** -0.5),
        'moe_w_up': nrm((DEPTH, N_EXPERTS, D, D_EXPERT), D ** -0.5),
        'moe_w_down': nrm((DEPTH, N_EXPERTS, D_EXPERT, D), D_EXPERT ** -0.5),
    }


def reference(x_prompt, x_sample, c_prompt, c_sample,
              cache_a_latent, cache_a_krope, cache_b_k, cache_b_v, cache_b_logf,
              cache_c_k, cache_c_v, cache_c_kidx, cache_d_k, cache_d_v,
              w_ada, b_ada, g_mix, g_ffn, g_final, rel_bias,
              a_w_in, a_g_q, a_w_uq, a_g_kv, a_w_ukv, a_w_o,
              b_w_in, b_b_f, b_w_o,
              c_w_in, c_w_o,
              d_w_in, d_sinks, d_w_o,
              moe_w_router, moe_b_router, moe_w_gate, moe_w_up, moe_w_down):
    xp, xs = x_prompt, x_sample
    for i in range(DEPTH):
        sh1_p, sc1_p, gt1_p, sh2_p, sc2_p, gt2_p = adaln(c_prompt, w_ada[i], b_ada[i])
        sh1_s, sc1_s, gt1_s, sh2_s, sc2_s, gt2_s = adaln(c_sample, w_ada[i], b_ada[i])
        hp = rmsnorm(xp, g_mix[i]) * (1 + sc1_p) + sh1_p
        hs = rmsnorm(xs, g_mix[i]) * (1 + sc1_s) + sh1_s
        mixer = i % N_MIXERS
        if mixer == 0:
            op, a_lat_p, a_kr_p = mla_prompt(hp, a_w_in, a_g_q, a_w_uq, a_g_kv, a_w_ukv, a_w_o)
            os_, a_lat_s, a_kr_s = mla_sample(hs, cache_a_latent, cache_a_krope,
                                              a_w_in, a_g_q, a_w_uq, a_g_kv, a_w_ukv, a_w_o)
        elif mixer == 1:
            op, b_k_p, b_v_p, b_lf_p = fox_prompt(hp, b_w_in, b_b_f, b_w_o)
            os_, b_k_s, b_v_s, b_lf_s = fox_sample(hs, cache_b_k, cache_b_v, cache_b_logf, b_w_in, b_b_f, b_w_o)
        elif mixer == 2:
            op, c_k_p, c_v_p, c_ki_p = dsa_prompt(hp, c_w_in, c_w_o, rel_bias)
            os_, c_k_s, c_v_s, c_ki_s = dsa_sample(hs, cache_c_k, cache_c_v, cache_c_kidx, c_w_in, c_w_o, rel_bias)
        else:
            op, d_k_p, d_v_p = swa_prompt(hp, d_w_in, d_sinks, d_w_o, rel_bias)
            os_, d_k_s, d_v_s = swa_sample(hs, cache_d_k, cache_d_v, d_w_in, d_sinks, d_w_o, rel_bias)
        xp = xp + gt1_p * op
        xs = xs + gt1_s * os_
        hp = rmsnorm(xp, g_ffn[i]) * (1 + sc2_p) + sh2_p
        hs = rmsnorm(xs, g_ffn[i]) * (1 + sc2_s) + sh2_s
        xp = xp + gt2_p * moe(hp, moe_w_router, moe_b_router, moe_w_gate[i], moe_w_up[i], moe_w_down[i])
        xs = xs + gt2_s * moe(hs, moe_w_router, moe_b_router, moe_w_gate[i], moe_w_up[i], moe_w_down[i])
    y_prompt = rmsnorm(xp, g_final)
    y_sample = rmsnorm(xs, g_final)
    return (y_prompt, y_sample,
            a_lat_p, a_kr_p, b_k_p, b_v_p, b_lf_p, c_k_p, c_v_p, c_ki_p, d_k_p, d_v_p,
            a_lat_s, a_kr_s, b_k_s, b_v_s, b_lf_s, c_k_s, c_v_s, c_ki_s, d_k_s, d_v_s)
```

```python
import functools
import math

import jax
import jax.numpy as jnp
import numpy as np
from jax import lax
from jax.experimental import pallas as pl
from jax.experimental.pallas import tpu as pltpu

F32 = jnp.float32
BF16 = jnp.bfloat16
I32 = jnp.int32

CHUNK = 64
CHUNK_SHIFT = CHUNK.bit_length() - 1
NORM_EPS = 1e-6
NEG_INF = -1e30
N_HEADS = 16
HEAD_DIM = 64
HEAD_SCALE = HEAD_DIM ** -0.5
N_BUCKETS = 32
MAX_DISTANCE = 128
A_Q_LORA = 512
A_KV_LORA = 256
A_NOPE = 64
A_ROPE = 32
A_V = 64
A_SCALE = (A_NOPE + A_ROPE) ** -0.5
ROPE_THETA = 10000.0
C_IDX_HEADS = 8
C_IDX_DIM = 64
C_TOPK_MAX = 256
D_KV_HEADS = 2
D_REP = N_HEADS // D_KV_HEADS
WINDOW = 128
N_WIN_CHUNKS = WINDOW // CHUNK
N_EXPERTS = 16
N_GROUPS = 4
EXPERTS_PER_GROUP = N_EXPERTS // N_GROUPS
D_EXPERT = 256

LANES = 128
VMEM_LIMIT_BYTES = 56 * 1024 * 1024

PROJ_ROWS = 512
FLASH_TQ = 256
FLASH_TK = 512
DSA_TQ = 128
DSA_TK = 256
CUMSUM_TILE = 256

_NT = (((1,), (1,)), ((), ()))
_NN = (((1,), (0,)), ((), ()))


def _params(*sem):
    return pltpu.CompilerParams(dimension_semantics=sem, vmem_limit_bytes=VMEM_LIMIT_BYTES)


def _split2(a):
    hi = a.astype(BF16)
    lo = (a - hi.astype(F32)).astype(BF16)
    return hi, lo


def _dot3(a, b, dims):
    ah, al = _split2(a)
    bh, bl = _split2(b)
    d = lambda x, y: lax.dot_general(x, y, dims, preferred_element_type=F32)
    return d(ah, bh) + (d(ah, bl) + d(al, bh))


def _rms(x):
    return x * lax.rsqrt(jnp.mean(x * x, axis=-1, keepdims=True) + NORM_EPS)


def _modnorm(x, g, sc, sh):
    return _rms(x) * g * (1.0 + sc) + sh


def _sigmoid(z):
    return 1.0 / (1.0 + jnp.exp(-z))


def _adaln_kernel(c_ref, w_ref, b_ref, o_ref):
    c = c_ref[...]
    s = c * _sigmoid(c)
    o_ref[0] = _dot3(s, w_ref[0], _NN) + b_ref[0]


def _adaln(c_all, w_ada, b_ada):
    depth, d, d6 = w_ada.shape
    bc = c_all.shape[0]
    return pl.pallas_call(
        _adaln_kernel,
        grid=(depth, d6 // d),
        in_specs=[pl.BlockSpec((bc, d), lambda i, j: (0, 0)),
                  pl.BlockSpec((1, d, d), lambda i, j: (i, 0, j)),
                  pl.BlockSpec((1, 1, d), lambda i, j: (i, 0, j))],
        out_specs=pl.BlockSpec((1, bc, d), lambda i, j: (i, 0, j)),
        out_shape=jax.ShapeDtypeStruct((depth, bc, d6), F32),
        compiler_params=_params("parallel", "parallel"),
        name="adaln",
    )(c_all, w_ada, b_ada.reshape(depth, 1, d6))


_FAR_THRESHOLDS = (12, 16, 23, 32, 46, 64, 91)


def _rel_bucket_tile(rows, cols, off):
    r = lax.broadcasted_iota(I32, (rows, cols), 0)
    c = lax.broadcasted_iota(I32, (rows, cols), 1)
    rel = c - r + off
    n = jnp.abs(rel)
    nb = N_BUCKETS // 2
    max_exact = nb // 2
    far = jnp.full((rows, cols), max_exact, I32)
    for t in _FAR_THRESHOLDS:
        far = far + (n >= t).astype(I32)
    return jnp.where(rel > 0, nb, 0) + jnp.where(n < max_exact, n, far)


def _bias_kernel(tab_ref, dsa_ref, swa_ref):
    def fill(bucket, write):
        def body(h, _):
            acc = jnp.zeros(bucket.shape, F32)
            for b in range(N_BUCKETS):
                acc = jnp.where(bucket == b, tab_ref[b, h], acc)
            write(h, acc)
            return 0
        lax.fori_loop(0, N_HEADS, body, 0)

    for t, off in enumerate((-3 * DSA_TQ, -DSA_TQ, 0)):
        def write_dsa(h, acc, t=t):
            dsa_ref[t, h] = acc
        fill(_rel_bucket_tile(DSA_TQ, DSA_TQ, off), write_dsa)

    def write_swa(h, acc):
        swa_ref[h] = acc
    fill(_rel_bucket_tile(CHUNK, WINDOW + CHUNK, -WINDOW), write_swa)


def _bias_tiles(rel_bias):
    return pl.pallas_call(
        _bias_kernel,
        in_specs=[pl.BlockSpec(memory_space=pltpu.SMEM)],
        out_specs=[pl.BlockSpec(memory_space=pltpu.VMEM), pl.BlockSpec(memory_space=pltpu.VMEM)],
        out_shape=[jax.ShapeDtypeStruct((3, N_HEADS, DSA_TQ, DSA_TQ), F32),
                   jax.ShapeDtypeStruct((N_HEADS, CHUNK, WINDOW + CHUNK), F32)],
        name="bias_tiles",
    )(rel_bias)


def _mla_proj_kernel(x_ref, g_ref, sc_ref, sh_ref, win_ref, gq_ref, gkv_ref, wq_ref, wqs_ref,
                     cos_ref, sin_ref, q_ref, lat_ref, kr_ref):
    h = _modnorm(x_ref[0], g_ref[...], sc_ref[0], sh_ref[0]).astype(BF16)
    hw = jnp.dot(h, win_ref[...], preferred_element_type=F32)
    o1 = A_Q_LORA
    o2 = o1 + A_KV_LORA
    o3 = o2 + A_ROPE
    cqn = (_rms(hw[:, :o1]) * gq_ref[...]).astype(BF16)
    lat_ref[0] = _rms(hw[:, o1:o2]) * gkv_ref[...]
    cosp = cos_ref[...]
    sinp = sin_ref[...]
    kr_ref[0] = (hw[:, o2:o3] * cosp[:, A_NOPE:A_NOPE + A_ROPE]
                 + hw[:, o3:o3 + A_ROPE] * sinp[:, A_NOPE:A_NOPE + A_ROPE])
    a = jnp.dot(cqn, wq_ref[...], preferred_element_type=F32)
    b = jnp.dot(cqn, wqs_ref[...], preferred_element_type=F32)
    for hd in range(N_HEADS):
        sl = slice(hd * LANES, (hd + 1) * LANES)
        q_ref[0, :, sl] = ((a[:, sl] * cosp + b[:, sl] * sinp) * A_SCALE).astype(BF16)


def _mla_proj(x, g, sc, sh, w, cosp, sinp, tm):
    bsz, L, d = x.shape
    row = lambda b, i: (b, i, 0)
    per_b = lambda b, i: (b, 0, 0)
    const = lambda b, i: (0, 0)
    nin = w["a_in"].shape[1]
    return pl.pallas_call(
        _mla_proj_kernel,
        grid=(bsz, L // tm),
        in_specs=[pl.BlockSpec((1, tm, d), row),
                  pl.BlockSpec((1, d), const),
                  pl.BlockSpec((1, 1, d), per_b),
                  pl.BlockSpec((1, 1, d), per_b),
                  pl.BlockSpec((d, nin), const),
                  pl.BlockSpec((1, A_Q_LORA), const),
                  pl.BlockSpec((1, A_KV_LORA), const),
                  pl.BlockSpec((A_Q_LORA, N_HEADS * LANES), const),
                  pl.BlockSpec((A_Q_LORA, N_HEADS * LANES), const),
                  pl.BlockSpec((tm, LANES), lambda b, i: (i, 0)),
                  pl.BlockSpec((tm, LANES), lambda b, i: (i, 0))],
        out_specs=[pl.BlockSpec((1, tm, N_HEADS * LANES), row),
                   pl.BlockSpec((1, tm, A_KV_LORA), row),
                   pl.BlockSpec((1, tm, A_ROPE), row)],
        out_shape=[jax.ShapeDtypeStruct((bsz, L, N_HEADS * LANES), BF16),
                   jax.ShapeDtypeStruct((bsz, L, A_KV_LORA), F32),
                   jax.ShapeDtypeStruct((bsz, L, A_ROPE), F32)],
        compiler_params=_params("parallel", "parallel"),
        name="mla_proj",
    )(x, g, sc, sh, w["a_in"], w["a_gq"], w["a_gkv"], w["a_wq"], w["a_wqs"], cosp, sinp)


def _mla_expand_kernel(lat_ref, kr_ref, wk_ref, sel_ref, wv_ref, k_ref, v_ref):
    lat = lat_ref[0].astype(BF16)
    kr = kr_ref[0].astype(BF16)
    k = (jnp.dot(lat, wk_ref[...], preferred_element_type=F32)
         + jnp.dot(kr, sel_ref[...], preferred_element_type=F32))
    k_ref[0] = k.astype(BF16)
    v_ref[0] = jnp.dot(lat, wv_ref[...], preferred_element_type=F32).astype(BF16)


def _mla_expand(lat, kr, w, tm):
    bsz, L, _ = lat.shape
    row = lambda b, i: (b, i, 0)
    const = lambda b, i: (0, 0)
    return pl.pallas_call(
        _mla_expand_kernel,
        grid=(bsz, L // tm),
        in_specs=[pl.BlockSpec((1, tm, A_KV_LORA), row),
                  pl.BlockSpec((1, tm, A_ROPE), row),
                  pl.BlockSpec((A_KV_LORA, N_HEADS * LANES), const),
                  pl.BlockSpec((A_ROPE, N_HEADS * LANES), const),
                  pl.BlockSpec((A_KV_LORA, N_HEADS * A_V), const)],
        out_specs=[pl.BlockSpec((1, tm, N_HEADS * LANES), row),
                   pl.BlockSpec((1, tm, N_HEADS * A_V), row)],
        out_shape=[jax.ShapeDtypeStruct((bsz, L, N_HEADS * LANES), BF16),
                   jax.ShapeDtypeStruct((bsz, L, N_HEADS * A_V), BF16)],
        compiler_params=_params("parallel", "parallel"),
        name="mla_expand",
    )(lat, kr, w["a_wk"], w["a_sel"], w["a_wv"])


def _flash_kernel(*refs, tq, tk, qpos0, per_frame, wide, decay, nk_tiles):
    if decay:
        q_ref, k_ref, v_ref, qd_ref, kd_ref, o_ref, m_sc, l_sc, acc_sc = refs
    else:
        q_ref, k_ref, v_ref, o_ref, m_sc, l_sc, acc_sc = refs
    i = pl.program_id(2)
    q0 = qpos0 + i * tq
    if per_frame:
        vis_all = q0 + 1
        vis_any = q0 + tq
    else:
        vis_all = ((q0 >> CHUNK_SHIFT) + 1) * CHUNK
        vis_any = (((q0 + tq - 1) >> CHUNK_SHIFT) + 1) * CHUNK
    n_full = jnp.minimum(vis_all // tk, nk_tiles)
    n_tot = jnp.minimum((vis_any + tk - 1) // tk, nk_tiles)

    q = q_ref[0]
    lane = lax.broadcasted_iota(I32, (1, LANES), 1)
    first = lane < HEAD_DIM
    if wide:
        qs = (q[:, :LANES], q[:, LANES:])
    else:
        zero = jnp.zeros_like(q)
        qs = (jnp.where(first, q, zero), jnp.where(first, zero, q))

    m_sc[...] = jnp.full(m_sc.shape, NEG_INF, F32)
    l_sc[...] = jnp.zeros(l_sc.shape, F32)
    acc_sc[...] = jnp.zeros(acc_sc.shape, F32)
    qp = q0 + lax.broadcasted_iota(I32, (tq, 1), 0)

    def tile(j, masked):
        ks = pl.multiple_of(j * tk, tk)
        kt = k_ref[0, pl.ds(ks, tk), :]
        vt = v_ref[0, pl.ds(ks, tk), :]
        if masked:
            kp = ks + lax.broadcasted_iota(I32, (1, tk), 1)
            if per_frame:
                mask = kp <= qp
            else:
                mask = (kp >> CHUNK_SHIFT) <= (qp >> CHUNK_SHIFT)
        for hh in range(2):
            kk = kt[:, hh * LANES:(hh + 1) * LANES] if wide else kt
            s = lax.dot_general(qs[hh], kk, _NT, preferred_element_type=F32)
            if decay:
                s = s + (qd_ref[0, hh] - kd_ref[0, hh, :, pl.ds(ks, tk)])
            if masked:
                s = jnp.where(mask, s, NEG_INF)
            m_old = m_sc[hh]
            m_new = jnp.maximum(m_old, jnp.max(s, axis=1, keepdims=True))
            p = jnp.exp(s - m_new)
            alpha = jnp.exp(m_old - m_new)
            l_sc[hh] = alpha * l_sc[hh] + jnp.sum(p, axis=1, keepdims=True)
            acc_sc[hh] = alpha * acc_sc[hh] + jnp.dot(p.astype(BF16), vt, preferred_element_type=F32)
            m_sc[hh] = m_new

    def full_body(j, c):
        tile(j, False)
        return c

    def diag_body(j, c):
        tile(j, True)
        return c

    lax.fori_loop(0, n_full, full_body, 0)
    lax.fori_loop(n_full, n_tot, diag_body, 0)
    o0 = acc_sc[0] / l_sc[0]
    o1 = acc_sc[1] / l_sc[1]
    o_ref[0] = jnp.where(first, o0, o1).astype(BF16)


def _flash(q, k, v, *, tq, tk, qpos0, per_frame, wide, qd=None, kd=None):
    bsz, lq, _ = q.shape
    lk = k.shape[1]
    qw = 2 * LANES if wide else LANES
    decay = qd is not None
    in_specs = [pl.BlockSpec((1, tq, qw), lambda b, hp, i: (b, i, hp)),
                pl.BlockSpec((1, lk, qw), lambda b, hp, i: (b, 0, hp)),
                pl.BlockSpec((1, lk, LANES), lambda b, hp, i: (b, 0, hp))]
    args = [q, k, v]
    if decay:
        in_specs += [pl.BlockSpec((1, 2, tq, 1), lambda b, hp, i: (b, hp, i, 0)),
                     pl.BlockSpec((1, 2, 1, lk), lambda b, hp, i: (b, hp, 0, 0))]
        args += [qd, kd]
    kern = functools.partial(_flash_kernel, tq=tq, tk=tk, qpos0=qpos0, per_frame=per_frame,
                             wide=wide, decay=decay, nk_tiles=lk // tk)
    return pl.pallas_call(
        kern,
        grid=(bsz, N_HEADS // 2, lq // tq),
        in_specs=in_specs,
        out_specs=pl.BlockSpec((1, tq, LANES), lambda b, hp, i: (b, i, hp)),
        out_shape=jax.ShapeDtypeStruct((bsz, lq, N_HEADS * HEAD_DIM), BF16),
        scratch_shapes=[pltpu.VMEM((2, tq, 1), F32), pltpu.VMEM((2, tq, 1), F32),
                        pltpu.VMEM((2, tq, LANES), F32)],
        compiler_params=_params("parallel", "parallel", "arbitrary"),
        name="flash_attention",
    )(*args)


def _fox_proj_kernel(x_ref, g_ref, sc_ref, sh_ref, wq_ref, wk_ref, wv_ref, wf_ref, bf_ref,
                     q_ref, k_ref, v_ref, kb_ref, vb_ref, lf_ref):
    h = _modnorm(x_ref[0], g_ref[...], sc_ref[0], sh_ref[0]).astype(BF16)
    q_ref[0] = (jnp.dot(h, wq_ref[...], preferred_element_type=F32) * HEAD_SCALE).astype(BF16)
    k = jnp.dot(h, wk_ref[...], preferred_element_type=F32)
    k_ref[0] = k
    kb_ref[0] = k.astype(BF16)
    v = jnp.dot(h, wv_ref[...], preferred_element_type=F32)
    v_ref[0] = v
    vb_ref[0] = v.astype(BF16)
    f = jnp.dot(h, wf_ref[...], preferred_element_type=F32)[:, :N_HEADS] + bf_ref[...]
    lf_ref[0] = jnp.minimum(f, 0.0) - jnp.log1p(jnp.exp(-jnp.abs(f)))


def _fox_proj(x, g, sc, sh, w, tm):
    bsz, L, d = x.shape
    hd = N_HEADS * HEAD_DIM
    row = lambda b, i: (b, i, 0)
    per_b = lambda b, i: (b, 0, 0)
    const = lambda b, i: (0, 0)
    return pl.pallas_call(
        _fox_proj_kernel,
        grid=(bsz, L // tm),
        in_specs=[pl.BlockSpec((1, tm, d), row),
                  pl.BlockSpec((1, d), const),
                  pl.BlockSpec((1, 1, d), per_b),
                  pl.BlockSpec((1, 1, d), per_b),
                  pl.BlockSpec((d, hd), const),
                  pl.BlockSpec((d, hd), const),
                  pl.BlockSpec((d, hd), const),
                  pl.BlockSpec((d, LANES), const),
                  pl.BlockSpec((1, N_HEADS), const)],
        out_specs=[pl.BlockSpec((1, tm, hd), row)] * 5 + [pl.BlockSpec((1, tm, N_HEADS), row)],
        out_shape=[jax.ShapeDtypeStruct((bsz, L, hd), BF16),
                   jax.ShapeDtypeStruct((bsz, L, hd), F32),
                   jax.ShapeDtypeStruct((bsz, L, hd), F32),
                   jax.ShapeDtypeStruct((bsz, L, hd), BF16),
                   jax.ShapeDtypeStruct((bsz, L, hd), BF16),
                   jax.ShapeDtypeStruct((bsz, L, N_HEADS), F32)],
        compiler_params=_params("parallel", "parallel"),
        name="fox_proj",
    )(x, g, sc, sh, w["b_wq"], w["b_wk"], w["b_wv"], w["b_wf"], w["b_bf"])


def _cumsum_kernel(x_ref, o_ref, carry_ref, *, tc):
    @pl.when(pl.program_id(1) == 0)
    def _():
        carry_ref[...] = jnp.zeros(carry_ref.shape, F32)

    x = x_ref[0]
    r = lax.broadcasted_iota(I32, (tc, tc), 0)
    c = lax.broadcasted_iota(I32, (tc, tc), 1)
    upper = (r <= c).astype(BF16)
    h1 = x.astype(BF16)
    r1 = x - h1.astype(F32)
    h2 = r1.astype(BF16)
    h3 = (r1 - h2.astype(F32)).astype(BF16)
    d = lambda a: jnp.dot(a, upper, preferred_element_type=F32)
    cum = ((d(h3) + d(h2)) + d(h1)) + carry_ref[...]
    o_ref[0] = cum
    carry_ref[...] = cum[:, tc - 1:tc]


def _cumsum_rows(x, tc):
    bsz, nh, L = x.shape
    return pl.pallas_call(
        functools.partial(_cumsum_kernel, tc=tc),
        grid=(bsz, L // tc),
        in_specs=[pl.BlockSpec((1, nh, tc), lambda b, j: (b, 0, j))],
        out_specs=pl.BlockSpec((1, nh, tc), lambda b, j: (b, 0, j)),
        out_shape=jax.ShapeDtypeStruct((bsz, nh, L), F32),
        scratch_shapes=[pltpu.VMEM((nh, 1), F32)],
        compiler_params=_params("parallel", "arbitrary"),
        name="cumsum",
    )(x)


def _dsa_proj_kernel(x_ref, g_ref, sc_ref, sh_ref, wq_ref, wqi_ref, wsm_ref,
                     q_ref, qi_ref, k_ref, v_ref, ki_ref, wi_ref, kb_ref, vb_ref, kib_ref):
    h = _modnorm(x_ref[0], g_ref[...], sc_ref[0], sh_ref[0]).astype(BF16)
    q_ref[0] = (jnp.dot(h, wq_ref[...], preferred_element_type=F32) * HEAD_SCALE).astype(BF16)
    qi_ref[0] = (jnp.dot(h, wqi_ref[...], preferred_element_type=F32) * (C_IDX_DIM ** -0.5)).astype(BF16)
    sm = jnp.dot(h, wsm_ref[...], preferred_element_type=F32)
    k = sm[:, :HEAD_DIM]
    v = sm[:, HEAD_DIM:2 * HEAD_DIM]
    ki = sm[:, 2 * HEAD_DIM:2 * HEAD_DIM + C_IDX_DIM]
    o = 2 * HEAD_DIM + C_IDX_DIM
    k_ref[0] = k
    v_ref[0] = v
    ki_ref[0] = ki
    kb_ref[0] = k.astype(BF16)
    vb_ref[0] = v.astype(BF16)
    kib_ref[0] = ki.astype(BF16)
    wi_ref[0] = sm[:, o:o + C_IDX_HEADS] * (C_IDX_HEADS ** -0.5)


def _dsa_proj(x, g, sc, sh, w, tm):
    bsz, L, d = x.shape
    hd = N_HEADS * HEAD_DIM
    hi = C_IDX_HEADS * C_IDX_DIM
    row = lambda b, i: (b, i, 0)
    per_b = lambda b, i: (b, 0, 0)
    const = lambda b, i: (0, 0)
    small = lambda n, dt: jax.ShapeDtypeStruct((bsz, L, n), dt)
    return pl.pallas_call(
        _dsa_proj_kernel,
        grid=(bsz, L // tm),
        in_specs=[pl.BlockSpec((1, tm, d), row),
                  pl.BlockSpec((1, d), const),
                  pl.BlockSpec((1, 1, d), per_b),
                  pl.BlockSpec((1, 1, d), per_b),
                  pl.BlockSpec((d, hd), const),
                  pl.BlockSpec((d, hi), const),
                  pl.BlockSpec((d, 2 * LANES), const)],
        out_specs=[pl.BlockSpec((1, tm, hd), row), pl.BlockSpec((1, tm, hi), row),
                   pl.BlockSpec((1, tm, HEAD_DIM), row), pl.BlockSpec((1, tm, HEAD_DIM), row),
                   pl.BlockSpec((1, tm, C_IDX_DIM), row), pl.BlockSpec((1, tm, C_IDX_HEADS), row),
                   pl.BlockSpec((1, tm, HEAD_DIM), row), pl.BlockSpec((1, tm, HEAD_DIM), row),
                   pl.BlockSpec((1, tm, C_IDX_DIM), row)],
        out_shape=[small(hd, BF16), small(hi, BF16), small(HEAD_DIM, F32), small(HEAD_DIM, F32),
                   small(C_IDX_DIM, F32), small(C_IDX_HEADS, F32),
                   small(HEAD_DIM, BF16), small(HEAD_DIM, BF16), small(C_IDX_DIM, BF16)],
        compiler_params=_params("parallel", "parallel"),
        name="dsa_proj",
    )(x, g, sc, sh, w["c_wq"], w["c_wqi"], w["c_wsm"])


_INT_MIN = -2 ** 31


def _dsa_kernel(q_ref, qi_ref, wi_ref, k_ref, v_ref, ki_ref, nb_ref, o_ref,
                sk_sc, m_sc, l_sc, acc_sc, *, qpos0, n_sel, nk_tiles):
    tq, tk = DSA_TQ, DSA_TK
    i = pl.program_id(1)
    q0 = qpos0 + i * tq
    nt = jnp.minimum((q0 + tq + tk - 1) // tk, nk_tiles)
    qp = q0 + lax.broadcasted_iota(I32, (tq, 1), 0)
    qch = qp >> CHUNK_SHIFT

    qi = qi_ref[0]
    qis = jnp.concatenate([qi[:, h * C_IDX_DIM:(h + 1) * C_IDX_DIM] for h in range(C_IDX_HEADS)], axis=0)
    wi = wi_ref[0]

    def score_body(j, c):
        ks = pl.multiple_of(j * tk, tk)
        kit = ki_ref[0, pl.ds(ks, tk), :]
        d = lax.dot_general(qis, kit, _NT, preferred_element_type=F32)
        sc = jnp.zeros((tq, tk), F32)
        for h in range(C_IDX_HEADS):
            sc = sc + jnp.maximum(d[h * tq:(h + 1) * tq], 0.0) * wi[:, h:h + 1]
        sc = jnp.where(sc == 0.0, 0.0, sc)
        kp = ks + lax.broadcasted_iota(I32, (1, tk), 1)
        sc = jnp.where((kp >> CHUNK_SHIFT) <= qch, sc, NEG_INF)
        bits = pltpu.bitcast(sc, I32)
        sk_sc[:, pl.ds(ks, tk)] = bits ^ ((bits >> 31) & 0x7FFFFFFF)
        return c

    lax.fori_loop(0, nt, score_body, 0)

    def count(pred):
        def body(j, c):
            kt = sk_sc[:, pl.ds(pl.multiple_of(j * tk, tk), tk)]
            g = pred(kt).astype(F32)
            return c + (g[:, :LANES] + g[:, LANES:])
        c = lax.fori_loop(0, nt, body, jnp.zeros((tq, LANES), F32))
        return jnp.sum(c, axis=1, keepdims=True)

    nsel = float(n_sel)
    lo = jnp.where(count(lambda kt: kt >= 0) >= nsel, 0, _INT_MIN).astype(I32)

    def bit_body(t, lo):
        cand = lo | jnp.left_shift(jnp.int32(1), 30 - t)
        return jnp.where(count(lambda kt: kt >= cand) >= nsel, cand, lo)

    thr = lax.fori_loop(0, 31, bit_body, lo)
    need = nsel - count(lambda kt: kt > thr)

    q = q_ref[0]
    qs = jnp.concatenate([q[:, h * HEAD_DIM:(h + 1) * HEAD_DIM] for h in range(N_HEADS)], axis=0)
    m_sc[...] = jnp.full(m_sc.shape, NEG_INF, F32)
    l_sc[...] = jnp.zeros(l_sc.shape, F32)
    acc_sc[...] = jnp.zeros(acc_sc.shape, F32)
    ra = lax.broadcasted_iota(I32, (tk, tk), 0)
    ca = lax.broadcasted_iota(I32, (tk, tk), 1)
    before = (ra < ca).astype(BF16)

    def attend(ks, w, bias, run):
        kt = sk_sc[:, pl.ds(ks, w)]
        eq = kt == thr
        rank = run + jnp.dot(eq.astype(BF16), before[:w, :w], preferred_element_type=F32)
        kp = ks + lax.broadcasted_iota(I32, (1, w), 1)
        sel = ((kt > thr) | (eq & (rank < need))) & ((kp >> CHUNK_SHIFT) <= qch)
        run = run + jnp.sum(eq.astype(F32), axis=1, keepdims=True)
        kk = k_ref[0, pl.ds(ks, w), :]
        vv = v_ref[0, pl.ds(ks, w), :]
        s = lax.dot_general(qs, kk, _NT, preferred_element_type=F32).reshape(N_HEADS, tq, w)
        s = jnp.where(sel[None], s + bias, NEG_INF)
        m_old = m_sc[...]
        m_new = jnp.maximum(m_old, jnp.max(s, axis=2, keepdims=True))
        p = jnp.exp(s - m_new)
        alpha = jnp.exp(m_old - m_new)
        l_sc[...] = alpha * l_sc[...] + jnp.sum(p, axis=2, keepdims=True)
        pv = jnp.dot(p.reshape(N_HEADS * tq, w).astype(BF16), vv, preferred_element_type=F32)
        acc_sc[...] = alpha.reshape(N_HEADS * tq, 1) * acc_sc[...] + pv
        m_sc[...] = m_new
        return run

    n_far = jnp.maximum(q0 - tq, 0) // tk
    far_bias = nb_ref[0, :, 0:1, 0:1]

    def far_body(j, run):
        return attend(pl.multiple_of(j * tk, tk), tk, far_bias, run)

    run = lax.fori_loop(0, n_far, far_body, jnp.zeros((tq, 1), F32))
    ks0 = n_far * tk
    n_tail = (q0 + tq - ks0) // tq

    def tail_body(t, run):
        ks = pl.multiple_of(ks0 + t * tq, tq)
        kind = jnp.clip((ks - q0) // tq + 2, 0, 2)
        return attend(ks, tq, nb_ref[kind], run)

    lax.fori_loop(0, n_tail, tail_body, run)
    o = acc_sc[...] / l_sc[...].reshape(N_HEADS * tq, 1)
    for h in range(N_HEADS):
        o_ref[0, :, h * HEAD_DIM:(h + 1) * HEAD_DIM] = o[h * tq:(h + 1) * tq].astype(BF16)


def _dsa_attention(q, qi, wi, k, v, ki, nb, *, qpos0, n_sel):
    bsz, lq, hd = q.shape
    lk = k.shape[1]
    hi = qi.shape[2]
    row = lambda b, i: (b, i, 0)
    whole = lambda b, i: (b, 0, 0)
    kern = functools.partial(_dsa_kernel, qpos0=qpos0, n_sel=n_sel, nk_tiles=lk // DSA_TK)
    return pl.pallas_call(
        kern,
        grid=(bsz, lq // DSA_TQ),
        in_specs=[pl.BlockSpec((1, DSA_TQ, hd), row),
                  pl.BlockSpec((1, DSA_TQ, hi), row),
                  pl.BlockSpec((1, DSA_TQ, C_IDX_HEADS), row),
                  pl.BlockSpec((1, lk, HEAD_DIM), whole),
                  pl.BlockSpec((1, lk, HEAD_DIM), whole),
                  pl.BlockSpec((1, lk, C_IDX_DIM), whole),
                  pl.BlockSpec((3, N_HEADS, DSA_TQ, DSA_TQ), lambda b, i: (0, 0, 0, 0))],
        out_specs=pl.BlockSpec((1, DSA_TQ, hd), row),
        out_shape=jax.ShapeDtypeStruct((bsz, lq, hd), BF16),
        scratch_shapes=[pltpu.VMEM((DSA_TQ, lk), I32),
                        pltpu.VMEM((N_HEADS, DSA_TQ, 1), F32),
                        pltpu.VMEM((N_HEADS, DSA_TQ, 1), F32),
                        pltpu.VMEM((N_HEADS * DSA_TQ, HEAD_DIM), F32)],
        compiler_params=_params("parallel", "arbitrary"),
        name="dsa_attention",
    )(q, qi, wi, k, v, ki, nb)


def _swa_proj_kernel(x_ref, g_ref, sc_ref, sh_ref, wq_ref, wkv_ref, q_ref, kv_ref, kvb_ref):
    h = _modnorm(x_ref[0], g_ref[...], sc_ref[0], sh_ref[0]).astype(BF16)
    q_ref[0] = (jnp.dot(h, wq_ref[...], preferred_element_type=F32) * HEAD_SCALE).astype(BF16)
    kv = jnp.dot(h, wkv_ref[...], preferred_element_type=F32)
    kv_ref[0] = kv
    kvb_ref[0] = kv.astype(BF16)


def _swa_proj(x, g, sc, sh, w, tm):
    bsz, L, d = x.shape
    hd = N_HEADS * HEAD_DIM
    kvw = 2 * D_KV_HEADS * HEAD_DIM
    row = lambda b, i: (b, i, 0)
    per_b = lambda b, i: (b, 0, 0)
    const = lambda b, i: (0, 0)
    return pl.pallas_call(
        _swa_proj_kernel,
        grid=(bsz, L // tm),
        in_specs=[pl.BlockSpec((1, tm, d), row),
                  pl.BlockSpec((1, d), const),
                  pl.BlockSpec((1, 1, d), per_b),
                  pl.BlockSpec((1, 1, d), per_b),
                  pl.BlockSpec((d, hd), const),
                  pl.BlockSpec((d, kvw), const)],
        out_specs=[pl.BlockSpec((1, tm, hd), row), pl.BlockSpec((1, tm, kvw), row),
                   pl.BlockSpec((1, tm, kvw), row)],
        out_shape=[jax.ShapeDtypeStruct((bsz, L, hd), BF16),
                   jax.ShapeDtypeStruct((bsz, L, kvw), F32),
                   jax.ShapeDtypeStruct((bsz, L, kvw), BF16)],
        compiler_params=_params("parallel", "parallel"),
        name="swa_proj",
    )(x, g, sc, sh, w["d_wq"], w["d_wkv"])


def _swa_kernel(q_ref, kv_ref, nb_ref, sink_ref, o_ref, *, mask_front):
    band = WINDOW + CHUNK
    c = pl.program_id(1)
    start = pl.multiple_of(c * CHUNK, CHUNK)
    kvb = kv_ref[0, pl.ds(start, band), :]
    q = q_ref[0]
    if mask_front:
        ok = (start - WINDOW + lax.broadcasted_iota(I32, (1, 1, band), 2)) >= 0
    kw = D_KV_HEADS * HEAD_DIM
    for g in range(D_KV_HEADS):
        kg = kvb[:, g * HEAD_DIM:(g + 1) * HEAD_DIM]
        vg = kvb[:, kw + g * HEAD_DIM:kw + (g + 1) * HEAD_DIM]
        qg = jnp.concatenate([q[:, (g * D_REP + r) * HEAD_DIM:(g * D_REP + r + 1) * HEAD_DIM]
                              for r in range(D_REP)], axis=0)
        s = lax.dot_general(qg, kg, _NT, preferred_element_type=F32).reshape(D_REP, CHUNK, band)
        s = s + nb_ref[g * D_REP:(g + 1) * D_REP]
        if mask_front:
            s = jnp.where(ok, s, NEG_INF)
        sink = sink_ref[g * D_REP:(g + 1) * D_REP]
        m = jnp.maximum(jnp.max(s, axis=2, keepdims=True), sink)
        e = jnp.exp(s - m)
        p = e / (jnp.sum(e, axis=2, keepdims=True) + jnp.exp(sink - m))
        o = jnp.dot(p.reshape(D_REP * CHUNK, band).astype(BF16), vg, preferred_element_type=F32)
        for r in range(D_REP):
            hh = g * D_REP + r
            o_ref[0, :, hh * HEAD_DIM:(hh + 1) * HEAD_DIM] = o[r * CHUNK:(r + 1) * CHUNK].astype(BF16)


def _swa_attention(q, kv, nb, sinks, *, mask_front):
    bsz, L, hd = q.shape
    lkv, kvw = kv.shape[1], kv.shape[2]
    return pl.pallas_call(
        functools.partial(_swa_kernel, mask_front=mask_front),
        grid=(bsz, L // CHUNK),
        in_specs=[pl.BlockSpec((1, CHUNK, hd), lambda b, c: (b, c, 0)),
                  pl.BlockSpec((1, lkv, kvw), lambda b, c: (b, 0, 0)),
                  pl.BlockSpec((N_HEADS, CHUNK, WINDOW + CHUNK), lambda b, c: (0, 0, 0)),
                  pl.BlockSpec((N_HEADS, 1, 1), lambda b, c: (0, 0, 0))],
        out_specs=pl.BlockSpec((1, CHUNK, hd), lambda b, c: (b, c, 0)),
        out_shape=jax.ShapeDtypeStruct((bsz, L, hd), BF16),
        compiler_params=_params("parallel", "parallel"),
        name="swa_attention",
    )(q, kv, nb, sinks)


def _route_rows(s, sb):
    n, m = N_GROUPS, EXPERTS_PER_GROUP
    grp = []
    for g in range(n):
        x = sb[g * m:(g + 1) * m]
        best = None
        for a in range(m):
            for b in range(a + 1, m):
                pair = x[a] + x[b]
                best = pair if best is None else jnp.maximum(best, pair)
        grp.append(best)
    chosen_g = []
    taken = None
    for g in range(n):
        is_g = None
        for o in range(g + 1, n):
            c = grp[g] >= grp[o]
            is_g = c if is_g is None else (is_g & c)
        if is_g is None:
            is_g = ~taken
        elif taken is not None:
            is_g = is_g & (~taken)
        taken = is_g if taken is None else (taken | is_g)
        chosen_g.append(is_g)
    picked = []
    for e in range(N_EXPERTS):
        g, a = divmod(e, m)
        beaten = jnp.zeros(sb[e].shape, F32)
        for b in range(m):
            if b == a:
                continue
            o = g * m + b
            wins = (sb[o] > sb[e]) | ((sb[o] == sb[e]) & (b < a))
            beaten = beaten + wins.astype(F32)
        picked.append(chosen_g[g] & (beaten < 2.0))
    tops = [jnp.where(picked[e], s[e], 0.0) for e in range(N_EXPERTS)]
    denom = tops[0]
    for e in range(1, N_EXPERTS):
        denom = denom + tops[e]
    return [t / denom for t in tops]


def _post_mix_kernel(o_ref, wo_ref, x_ref, gt_ref, g_ref, sc_ref, sh_ref, wr_ref, br_ref,
                     x1_ref, h2_ref, gates_ref):
    mixed = jnp.dot(o_ref[0], wo_ref[...], preferred_element_type=F32)
    x1 = x_ref[0] + gt_ref[0] * mixed
    x1_ref[0] = x1
    h2 = _modnorm(x1, g_ref[...], sc_ref[0], sh_ref[0])
    h2_ref[0] = h2.astype(BF16)
    logits = _dot3(wr_ref[...], h2, _NT)
    s = _sigmoid(logits)
    sb = s + br_ref[...]
    rows = _route_rows([s[e:e + 1] for e in range(N_EXPERTS)], [sb[e:e + 1] for e in range(N_EXPERTS)])
    gates_ref[0] = jnp.concatenate(rows, axis=0)


def _post_mix(o, wo, x, gt, g, sc, sh, wr_t, br, tm):
    bsz, L, d = x.shape
    hd = o.shape[2]
    row = lambda b, i: (b, i, 0)
    per_b = lambda b, i: (b, 0, 0)
    const = lambda b, i: (0, 0)
    return pl.pallas_call(
        _post_mix_kernel,
        grid=(bsz, L // tm),
        in_specs=[pl.BlockSpec((1, tm, hd), row),
                  pl.BlockSpec((hd, d), const),
                  pl.BlockSpec((1, tm, d), row),
                  pl.BlockSpec((1, 1, d), per_b),
                  pl.BlockSpec((1, d), const),
                  pl.BlockSpec((1, 1, d), per_b),
                  pl.BlockSpec((1, 1, d), per_b),
                  pl.BlockSpec((N_EXPERTS, d), const),
                  pl.BlockSpec((N_EXPERTS, 1), const)],
        out_specs=[pl.BlockSpec((1, tm, d), row), pl.BlockSpec((1, tm, d), row),
                   pl.BlockSpec((1, N_EXPERTS, tm), lambda b, i: (b, 0, i))],
        out_shape=[jax.ShapeDtypeStruct((bsz, L, d), F32),
                   jax.ShapeDtypeStruct((bsz, L, d), BF16),
                   jax.ShapeDtypeStruct((bsz, N_EXPERTS, L), F32)],
        compiler_params=_params("parallel", "parallel"),
        name="post_mix_route",
    )(o, wo, x, gt, g, sc, sh, wr_t, br)


def _moe_kernel(x_ref, h_ref, gates_ref, gt_ref, wgu_ref, wd_ref, o_ref, acc_sc):
    e = pl.program_id(2)

    @pl.when(e == 0)
    def _():
        acc_sc[...] = jnp.zeros(acc_sc.shape, F32)

    gu = jnp.dot(h_ref[0], wgu_ref[0], preferred_element_type=F32)
    gate = gu[:, :D_EXPERT]
    act = (gate * _sigmoid(gate)) * gu[:, D_EXPERT:]
    y = jnp.dot(act.astype(BF16), wd_ref[0], preferred_element_type=F32)
    gates = gates_ref[0]
    lane = lax.broadcasted_iota(I32, gates.shape, 1)
    ge = jnp.sum(jnp.where(lane == e, gates, 0.0), axis=1, keepdims=True)
    acc_sc[...] += ge * y

    @pl.when(e == N_EXPERTS - 1)
    def _():
        o_ref[0] = x_ref[0] + gt_ref[0] * acc_sc[...]


def _moe(x, h, gates, gt, wgu, wd, tm):
    bsz, L, d = x.shape
    row = lambda b, i, e: (b, i, 0)
    return pl.pallas_call(
        _moe_kernel,
        grid=(bsz, L // tm, N_EXPERTS),
        in_specs=[pl.BlockSpec((1, tm, d), row),
                  pl.BlockSpec((1, tm, d), row),
                  pl.BlockSpec((1, tm, N_EXPERTS), row),
                  pl.BlockSpec((1, 1, d), lambda b, i, e: (b, 0, 0)),
                  pl.BlockSpec((1, d, 2 * D_EXPERT), lambda b, i, e: (e, 0, 0)),
                  pl.BlockSpec((1, D_EXPERT, d), lambda b, i, e: (e, 0, 0))],
        out_specs=pl.BlockSpec((1, tm, d), row),
        out_shape=jax.ShapeDtypeStruct((bsz, L, d), F32),
        scratch_shapes=[pltpu.VMEM((tm, d), F32)],
        compiler_params=_params("parallel", "parallel", "arbitrary"),
        name="moe_experts",
    )(x, h, gates, gt, wgu, wd)


def _final_norm_kernel(x_ref, g_ref, o_ref):
    o_ref[0] = _rms(x_ref[0]) * g_ref[...]


def _final_norm(x, g, tm):
    bsz, L, d = x.shape
    row = lambda b, i: (b, i, 0)
    return pl.pallas_call(
        _final_norm_kernel,
        grid=(bsz, L // tm),
        in_specs=[pl.BlockSpec((1, tm, d), row), pl.BlockSpec((1, d), lambda b, i: (0, 0))],
        out_specs=pl.BlockSpec((1, tm, d), row),
        out_shape=jax.ShapeDtypeStruct((bsz, L, d), F32),
        compiler_params=_params("parallel", "parallel"),
        name="final_norm",
    )(x, g)


def _rot_cols(w):
    half = w.shape[-1] // 2
    return jnp.concatenate([-w[..., half:], w[..., :half]], axis=-1)


def _prep_weights(a_w_in, a_g_q, a_w_uq, a_g_kv, a_w_ukv, a_w_o, b_w_in, b_b_f, b_w_o,
                  c_w_in, c_w_o, d_w_in, d_sinks, d_w_o, moe_w_router, moe_b_router,
                  moe_w_gate, moe_w_up, moe_w_down):
    w = {}
    hd = N_HEADS * HEAD_DIM
    kr = a_w_in[:, A_Q_LORA + A_KV_LORA:]
    w["a_in"] = jnp.concatenate([a_w_in, _rot_cols(kr)], axis=1).astype(BF16)
    w["a_gq"] = a_g_q.reshape(1, -1)
    w["a_gkv"] = a_g_kv.reshape(1, -1)
    uq = a_w_uq.reshape(A_Q_LORA, N_HEADS, A_NOPE + A_ROPE)
    pad = LANES - A_NOPE - A_ROPE
    zq = lambda n: jnp.zeros((A_Q_LORA, N_HEADS, n), F32)
    w["a_wq"] = jnp.concatenate([uq, zq(pad)], axis=-1).reshape(A_Q_LORA, N_HEADS * LANES).astype(BF16)
    w["a_wqs"] = jnp.concatenate([zq(A_NOPE), _rot_cols(uq[..., A_NOPE:]), zq(pad)],
                                 axis=-1).reshape(A_Q_LORA, N_HEADS * LANES).astype(BF16)
    ukv = a_w_ukv.reshape(A_KV_LORA, N_HEADS, A_NOPE + A_V)
    w["a_wk"] = jnp.concatenate([ukv[..., :A_NOPE], jnp.zeros((A_KV_LORA, N_HEADS, LANES - A_NOPE), F32)],
                                axis=-1).reshape(A_KV_LORA, N_HEADS * LANES).astype(BF16)
    w["a_wv"] = ukv[..., A_NOPE:].reshape(A_KV_LORA, N_HEADS * A_V).astype(BF16)
    sel = np.zeros((A_ROPE, N_HEADS, LANES), np.float32)
    for r in range(A_ROPE):
        sel[r, :, A_NOPE + r] = 1.0
    w["a_sel"] = jnp.asarray(sel.reshape(A_ROPE, N_HEADS * LANES), BF16)
    w["a_wo"] = a_w_o.astype(BF16)
    w["b_wq"] = b_w_in[:, :hd].astype(BF16)
    w["b_wk"] = b_w_in[:, hd:2 * hd].astype(BF16)
    w["b_wv"] = b_w_in[:, 2 * hd:3 * hd].astype(BF16)
    w["b_wf"] = jnp.pad(b_w_in[:, 3 * hd:], ((0, 0), (0, LANES - N_HEADS))).astype(BF16)
    w["b_bf"] = b_b_f.reshape(1, N_HEADS)
    w["b_wo"] = b_w_o.astype(BF16)
    hi = C_IDX_HEADS * C_IDX_DIM
    w["c_wq"] = c_w_in[:, :hd].astype(BF16)
    kv = c_w_in[:, hd:hd + 2 * HEAD_DIM]
    qi = c_w_in[:, hd + 2 * HEAD_DIM:hd + 2 * HEAD_DIM + hi]
    rest = c_w_in[:, hd + 2 * HEAD_DIM + hi:]
    w["c_wqi"] = qi.astype(BF16)
    sm = jnp.concatenate([kv, rest], axis=1)
    w["c_wsm"] = jnp.pad(sm, ((0, 0), (0, 2 * LANES - sm.shape[1]))).astype(BF16)
    w["c_wo"] = c_w_o.astype(BF16)
    w["d_wq"] = d_w_in[:, :hd].astype(BF16)
    w["d_wkv"] = d_w_in[:, hd:].astype(BF16)
    w["d_sinks"] = d_sinks.astype(F32).reshape(N_HEADS, 1, 1)
    w["d_wo"] = d_w_o.astype(BF16)
    w["wr_t"] = moe_w_router.T
    w["br"] = moe_b_router.reshape(N_EXPERTS, 1)
    w["wgu"] = jnp.concatenate([moe_w_gate, moe_w_up], axis=-1).astype(BF16)
    w["wd"] = moe_w_down.astype(BF16)
    return w


def _rope_tables(pos):
    half = A_ROPE // 2
    inv = ROPE_THETA ** (-jnp.arange(half, dtype=F32) / half)
    ang = pos.astype(F32)[:, None] * inv[None, :]
    cos, sin = jnp.cos(ang), jnp.sin(ang)
    n = pos.shape[0]
    pad = LANES - A_NOPE - A_ROPE
    cosp = jnp.concatenate([jnp.ones((n, A_NOPE), F32), cos, cos, jnp.zeros((n, pad), F32)], axis=1)
    sinp = jnp.concatenate([jnp.zeros((n, A_NOPE), F32), sin, sin, jnp.zeros((n, pad), F32)], axis=1)
    return cosp, sinp


def _pad_rows(a, n):
    return jnp.pad(a, ((0, 0), (0, n - a.shape[1])) + ((0, 0),) * (a.ndim - 2))


def _round_up(n, m):
    return (n + m - 1) // m * m


def kernel(x_prompt, x_sample, c_prompt, c_sample, cache_a_latent, cache_a_krope, cache_b_k, cache_b_v, cache_b_logf, cache_c_k, cache_c_v, cache_c_kidx, cache_d_k, cache_d_v, w_ada, b_ada, g_mix, g_ffn, g_final, rel_bias, a_w_in, a_g_q, a_w_uq, a_g_kv, a_w_ukv, a_w_o, b_w_in, b_b_f, b_w_o, c_w_in, c_w_o, d_w_in, d_sinks, d_w_o, moe_w_router, moe_b_router, moe_w_gate, moe_w_up, moe_w_down):
    bp, S, d = x_prompt.shape
    bs, Ls, _ = x_sample.shape
    P = cache_a_latent.shape[1]
    depth = w_ada.shape[0]
    hd = N_HEADS * HEAD_DIM
    assert S % FLASH_TK == 0 and S % PROJ_ROWS == 0 and Ls == CHUNK and P % DSA_TQ == 0
    tm_p, tm_s = PROJ_ROWS, Ls
    lk_s = _round_up(P + Ls, DSA_TK)

    w = _prep_weights(a_w_in, a_g_q, a_w_uq, a_g_kv, a_w_ukv, a_w_o, b_w_in, b_b_f, b_w_o,
                      c_w_in, c_w_o, d_w_in, d_sinks, d_w_o, moe_w_router, moe_b_router,
                      moe_w_gate, moe_w_up, moe_w_down)
    mod = _adaln(jnp.concatenate([c_prompt, c_sample], axis=0), w_ada, b_ada)
    nb_dsa, nb_swa = _bias_tiles(rel_bias)

    def mods(i, lo, hi):
        return [mod[i, lo:hi, k * d:(k + 1) * d][:, None, :] for k in range(6)]

    xp, xs = x_prompt, x_sample
    outs = {}
    for i in range(depth):
        sh1_p, sc1_p, gt1_p, sh2_p, sc2_p, gt2_p = mods(i, 0, bp)
        sh1_s, sc1_s, gt1_s, sh2_s, sc2_s, gt2_s = mods(i, bp, bp + bs)
        g1 = g_mix[i].reshape(1, d)
        mixer = i % 4
        if mixer == 0:
            cos_p, sin_p = _rope_tables(jnp.arange(S))
            cos_s, sin_s = _rope_tables(P + jnp.arange(Ls))
            q_p, lat_p, kr_p = _mla_proj(xp, g1, sc1_p, sh1_p, w, cos_p, sin_p, tm_p)
            k_p, v_p = _mla_expand(lat_p, kr_p, w, tm_p)
            op = _flash(q_p, k_p, v_p, tq=FLASH_TQ, tk=FLASH_TK, qpos0=0, per_frame=False, wide=True)
            q_s, lat_s, kr_s = _mla_proj(xs, g1, sc1_s, sh1_s, w, cos_s, sin_s, tm_s)
            lat_all = _pad_rows(jnp.concatenate([cache_a_latent, lat_s], axis=1), lk_s)
            kr_all = _pad_rows(jnp.concatenate([cache_a_krope, kr_s], axis=1), lk_s)
            k_s, v_s = _mla_expand(lat_all, kr_all, w, DSA_TK)
            os_ = _flash(q_s, k_s, v_s, tq=Ls, tk=DSA_TK, qpos0=P, per_frame=False, wide=True)
            outs["a"] = (lat_p, kr_p, lat_s, kr_s)
            wo = w["a_wo"]
        elif mixer == 1:
            q_p, k_p, v_p, kb_p, vb_p, lf_p = _fox_proj(xp, g1, sc1_p, sh1_p, w, tm_p)
            cum_p = _cumsum_rows(jnp.swapaxes(lf_p, 1, 2), CUMSUM_TILE)
            op = _flash(q_p, kb_p, vb_p, tq=FLASH_TQ, tk=FLASH_TK, qpos0=0, per_frame=True, wide=False,
                        qd=cum_p[..., None], kd=cum_p[:, :, None, :])
            q_s, k_s, v_s, kb_s, vb_s, lf_s = _fox_proj(xs, g1, sc1_s, sh1_s, w, tm_s)
            lf_all = jnp.concatenate([cache_b_logf.astype(F32), lf_s], axis=1)
            cum_s = _cumsum_rows(_pad_rows(lf_all, lk_s).swapaxes(1, 2), CUMSUM_TILE)
            kb_all = _pad_rows(jnp.concatenate([cache_b_k.reshape(bs, P, hd).astype(BF16), kb_s], axis=1), lk_s)
            vb_all = _pad_rows(jnp.concatenate([cache_b_v.reshape(bs, P, hd).astype(BF16), vb_s], axis=1), lk_s)
            os_ = _flash(q_s, kb_all, vb_all, tq=Ls, tk=DSA_TK, qpos0=P, per_frame=True, wide=False,
                         qd=cum_s[:, :, P:P + Ls, None], kd=cum_s[:, :, None, :])
            shp = lambda a: a.reshape(a.shape[0], a.shape[1], N_HEADS, HEAD_DIM)
            outs["b"] = (shp(k_p), shp(v_p), lf_p, shp(k_s), shp(v_s), lf_s)
            wo = w["b_wo"]
        elif mixer == 2:
            q_p, qi_p, k_p, v_p, ki_p, wi_p, kb_p, vb_p, kib_p = _dsa_proj(xp, g1, sc1_p, sh1_p, w, tm_p)
            op = _dsa_attention(q_p, qi_p, wi_p, kb_p, vb_p, kib_p, nb_dsa, qpos0=0,
                                n_sel=min(C_TOPK_MAX, S // 4))
            q_s, qi_s, k_s, v_s, ki_s, wi_s, kb_s, vb_s, kib_s = _dsa_proj(xs, g1, sc1_s, sh1_s, w, tm_s)
            cat = lambda cache, new: _pad_rows(jnp.concatenate([cache.astype(BF16), new], axis=1), lk_s)
            os_ = _dsa_attention(_pad_rows(q_s, DSA_TQ), _pad_rows(qi_s, DSA_TQ), _pad_rows(wi_s, DSA_TQ),
                                 cat(cache_c_k, kb_s), cat(cache_c_v, vb_s), cat(cache_c_kidx, kib_s),
                                 nb_dsa, qpos0=P, n_sel=min(C_TOPK_MAX, (P + Ls) // 4))[:, :Ls]
            outs["c"] = (k_p, v_p, ki_p, k_s, v_s, ki_s)
            wo = w["c_wo"]
        else:
            kvw = D_KV_HEADS * HEAD_DIM
            q_p, kv_p, kvb_p = _swa_proj(xp, g1, sc1_p, sh1_p, w, tm_p)
            op = _swa_attention(q_p, jnp.pad(kvb_p, ((0, 0), (WINDOW, 0), (0, 0))), nb_swa, w["d_sinks"],
                                mask_front=True)
            q_s, kv_s, kvb_s = _swa_proj(xs, g1, sc1_s, sh1_s, w, tm_s)
            wc = cache_d_k.shape[1]
            ck = cache_d_k.reshape(bs, wc, kvw)
            cv = cache_d_v.reshape(bs, wc, kvw)
            kv_cache = jnp.concatenate([ck, cv], axis=-1)
            kv_all = jnp.concatenate([kv_cache.astype(BF16), kvb_s], axis=1)
            kv_all = jnp.pad(kv_all, ((0, 0), (WINDOW - wc, 0), (0, 0)))
            os_ = _swa_attention(q_s, kv_all, nb_swa, w["d_sinks"], mask_front=(wc < WINDOW))
            keep = min(WINDOW, S)
            shp = lambda a: a.reshape(a.shape[0], a.shape[1], D_KV_HEADS, HEAD_DIM)
            k_roll = jnp.concatenate([ck, kv_s[..., :kvw]], axis=1)[:, Ls:]
            v_roll = jnp.concatenate([cv, kv_s[..., kvw:]], axis=1)[:, Ls:]
            outs["d"] = (shp(kv_p[:, S - keep:, :kvw]), shp(kv_p[:, S - keep:, kvw:]), shp(k_roll), shp(v_roll))
            wo = w["d_wo"]

        g2 = g_ffn[i].reshape(1, d)
        x1_p, h2_p, gates_p = _post_mix(op, wo, xp, gt1_p, g2, sc2_p, sh2_p, w["wr_t"], w["br"], tm_p)
        xp = _moe(x1_p, h2_p, jnp.swapaxes(gates_p, 1, 2), gt2_p, w["wgu"][i], w["wd"][i], tm_p)
        x1_s, h2_s, gates_s = _post_mix(os_, wo, xs, gt1_s, g2, sc2_s, sh2_s, w["wr_t"], w["br"], tm_s)
        xs = _moe(x1_s, h2_s, jnp.swapaxes(gates_s, 1, 2), gt2_s, w["wgu"][i], w["wd"][i], tm_s)

    gf = g_final.reshape(1, d)
    y_p = _final_norm(xp, gf, tm_p)
    y_s = _final_norm(xs, gf, tm_s)
    a_lat_p, a_kr_p, a_lat_s, a_kr_s = outs["a"]
    b_k_p, b_v_p, b_lf_p, b_k_s, b_v_s, b_lf_s = outs["b"]
    c_k_p, c_v_p, c_ki_p, c_k_s, c_v_s, c_ki_s = outs["c"]
    d_k_p, d_v_p, d_k_s, d_v_s = outs["d"]
    return (y_p, y_s,
            a_lat_p, a_kr_p, b_k_p, b_v_p, b_lf_p, c_k_p, c_v_p, c_ki_p, d_k_p, d_v_p,
            a_lat_s, a_kr_s, b_k_s, b_v_s, b_lf_s, c_k_s, c_v_s, c_ki_s, d_k_s, d_v_s)
```

```python
import functools
import math

import jax
import jax.numpy as jnp
import numpy as np
from jax import lax
from jax.experimental import pallas as pl
from jax.experimental.pallas import tpu as pltpu

F32 = jnp.float32
BF16 = jnp.bfloat16
I32 = jnp.int32

CHUNK = 64
CHUNK_SHIFT = CHUNK.bit_length() - 1
NORM_EPS = 1e-6
NEG_INF = -1e30
LOG2E = math.log2(math.e)
N_HEADS = 16
HEAD_DIM = 64
HEAD_SCALE = HEAD_DIM ** -0.5
N_BUCKETS = 32
MAX_DISTANCE = 128
A_Q_LORA = 512
A_KV_LORA = 256
A_NOPE = 64
A_ROPE = 32
A_V = 64
A_SCALE = (A_NOPE + A_ROPE) ** -0.5
ROPE_THETA = 10000.0
C_IDX_HEADS = 8
C_IDX_DIM = 64
C_TOPK_MAX = 256
D_KV_HEADS = 2
D_REP = N_HEADS // D_KV_HEADS
WINDOW = 128
N_WIN_CHUNKS = WINDOW // CHUNK
N_EXPERTS = 16
N_GROUPS = 4
EXPERTS_PER_GROUP = N_EXPERTS // N_GROUPS
D_EXPERT = 256

LANES = 128
VMEM_LIMIT_BYTES = 56 * 1024 * 1024

PROJ_ROWS = 512
FLASH_TQ = 256
FLASH_TK = 512
DSA_TQ = 128
DSA_TK = 256
DSA_SEARCH_ROWS = 512
V_ROWS = HEAD_DIM + 16
CUMSUM_TILE = 256
FLASH_T_TILE = 512
FOX_PROJ_ROWS = 256

_NT = (((1,), (1,)), ((), ()))
_NN = (((1,), (0,)), ((), ()))
_TN = (((0,), (0,)), ((), ()))


def _params(*sem):
    return pltpu.CompilerParams(dimension_semantics=sem, vmem_limit_bytes=VMEM_LIMIT_BYTES)


def _split2(a):
    hi = a.astype(BF16)
    lo = (a - hi.astype(F32)).astype(BF16)
    return hi, lo


def _split3(c):
    hi = c.astype(BF16).astype(F32)
    r1 = c - hi
    mid = r1.astype(BF16).astype(F32)
    lo = (r1 - mid).astype(BF16).astype(F32)
    return hi, mid, lo


def _dot3(a, b, dims):
    ah, al = _split2(a)
    bh, bl = _split2(b)
    d = lambda x, y: lax.dot_general(x, y, dims, preferred_element_type=F32)
    return d(ah, bh) + (d(ah, bl) + d(al, bh))


def _rms(x):
    return x * lax.rsqrt(jnp.mean(x * x, axis=-1, keepdims=True) + NORM_EPS)


def _modnorm(x, g, sc, sh):
    return _rms(x) * g * (1.0 + sc) + sh


def _sigmoid(z):
    return 1.0 / (1.0 + jnp.exp(-z))


def _log_sigmoid(z):
    return jnp.minimum(z, 0.0) - jnp.log1p(jnp.exp(-jnp.abs(z)))


def _adaln_kernel(c_ref, w_ref, b_ref, o_ref):
    c = c_ref[...]
    s = c * _sigmoid(c)
    o_ref[0] = _dot3(s, w_ref[0], _NN) + b_ref[0]


def _adaln(c_all, w_ada, b_ada):
    depth, d, d6 = w_ada.shape
    bc = c_all.shape[0]
    return pl.pallas_call(
        _adaln_kernel,
        grid=(depth, d6 // d),
        in_specs=[pl.BlockSpec((bc, d), lambda i, j: (0, 0)),
                  pl.BlockSpec((1, d, d), lambda i, j: (i, 0, j)),
                  pl.BlockSpec((1, 1, d), lambda i, j: (i, 0, j))],
        out_specs=pl.BlockSpec((1, bc, d), lambda i, j: (i, 0, j)),
        out_shape=jax.ShapeDtypeStruct((depth, bc, d6), F32),
        compiler_params=_params("parallel", "parallel"),
        name="adaln",
    )(c_all, w_ada, b_ada.reshape(depth, 1, d6))


_FAR_THRESHOLDS = (12, 16, 23, 32, 46, 64, 91)
_FAR_BUCKET = N_BUCKETS // 2 - 1
FAR_HI_ROW = HEAD_DIM
FAR_LO_ROW = HEAD_DIM + 1


def _rel_bucket(rel):
    n = jnp.abs(rel)
    nb = N_BUCKETS // 2
    max_exact = nb // 2
    far = jnp.full(rel.shape, max_exact, I32)
    for t in _FAR_THRESHOLDS:
        far = far + (n >= t).astype(I32)
    return jnp.where(rel > 0, nb, 0) + jnp.where(n < max_exact, n, far)


def _bias_kernel(tab_ref, farq_ref, dsa_ref, swa_ref):
    def lookup(bucket, h):
        acc = jnp.zeros(bucket.shape, F32)
        for b in range(N_BUCKETS):
            acc = jnp.where(bucket == b, tab_ref[b, h], acc)
        return acc

    kr = lax.broadcasted_iota(I32, (DSA_TQ, DSA_TQ), 0)
    qc = lax.broadcasted_iota(I32, (DSA_TQ, DSA_TQ), 1)
    for t, off in enumerate((-2 * DSA_TQ, -DSA_TQ, 0)):
        bucket = _rel_bucket(kr - qc + off)
        for h in range(N_HEADS):
            far = tab_ref[_FAR_BUCKET, h]
            dsa_ref[t, :, h * DSA_TQ:(h + 1) * DSA_TQ] = (lookup(bucket, h) - far) * LOG2E
    row = lax.broadcasted_iota(I32, (LANES, DSA_TQ), 0)
    for h in range(N_HEADS):
        c = jnp.full((LANES, DSA_TQ), tab_ref[_FAR_BUCKET, h] * LOG2E, F32)
        hi = c.astype(BF16).astype(F32)
        lo = (c - hi).astype(BF16).astype(F32)
        blk = jnp.where(row == FAR_HI_ROW, hi, jnp.where(row == FAR_LO_ROW, lo, 0.0))
        farq_ref[:, h * DSA_TQ:(h + 1) * DSA_TQ] = blk.astype(BF16)
    qr = lax.broadcasted_iota(I32, (CHUNK, WINDOW + CHUNK), 0)
    kc = lax.broadcasted_iota(I32, (CHUNK, WINDOW + CHUNK), 1)
    bucket = _rel_bucket(kc - qr - WINDOW)
    for h in range(N_HEADS):
        swa_ref[h] = lookup(bucket, h)


def _bias_tiles(rel_bias):
    vm = pl.BlockSpec(memory_space=pltpu.VMEM)
    return pl.pallas_call(
        _bias_kernel,
        in_specs=[pl.BlockSpec(memory_space=pltpu.SMEM)],
        out_specs=[vm, vm, vm],
        out_shape=[jax.ShapeDtypeStruct((LANES, N_HEADS * DSA_TQ), BF16),
                   jax.ShapeDtypeStruct((3, DSA_TQ, N_HEADS * DSA_TQ), F32),
                   jax.ShapeDtypeStruct((N_HEADS, CHUNK, WINDOW + CHUNK), F32)],
        compiler_params=pltpu.CompilerParams(vmem_limit_bytes=VMEM_LIMIT_BYTES),
        name="bias_tiles",
    )(rel_bias)


def _mla_in(x_ref, g_ref, sc_ref, sh_ref, win_ref, gq_ref, gkv_ref, cos_ref, sin_ref, lat_ref, kr_ref):
    h = _modnorm(x_ref[0], g_ref[...], sc_ref[0], sh_ref[0]).astype(BF16)
    hw = jnp.dot(h, win_ref[...], preferred_element_type=F32)
    o1 = A_Q_LORA
    o2 = o1 + A_KV_LORA
    o3 = o2 + A_ROPE
    lat_ref[0] = _rms(hw[:, o1:o2]) * gkv_ref[...]
    cosr = cos_ref[...][:, A_NOPE:A_NOPE + A_ROPE]
    sinr = sin_ref[...][:, A_NOPE:A_NOPE + A_ROPE]
    kr_ref[0] = hw[:, o2:o3] * cosr + hw[:, o3:o3 + A_ROPE] * sinr
    return (_rms(hw[:, :o1]) * gq_ref[...]).astype(BF16)


def _mla_proj_kernel(x_ref, g_ref, sc_ref, sh_ref, win_ref, gq_ref, gkv_ref, wq_ref, wqs_ref,
                     cos_ref, sin_ref, q_ref, lat_ref, kr_ref):
    cqn = _mla_in(x_ref, g_ref, sc_ref, sh_ref, win_ref, gq_ref, gkv_ref, cos_ref, sin_ref, lat_ref, kr_ref)
    cosp = cos_ref[...]
    sinp = sin_ref[...]
    a = jnp.dot(cqn, wq_ref[...], preferred_element_type=F32)
    b = jnp.dot(cqn, wqs_ref[...], preferred_element_type=F32)
    for hd in range(N_HEADS):
        sl = slice(hd * LANES, (hd + 1) * LANES)
        q_ref[0, :, sl] = ((a[:, sl] * cosp + b[:, sl] * sinp) * A_SCALE).astype(BF16)


def _mla_proj(x, g, sc, sh, w, cosp, sinp, tm):
    bsz, L, d = x.shape
    row = lambda b, i: (b, i, 0)
    per_b = lambda b, i: (b, 0, 0)
    const = lambda b, i: (0, 0)
    nin = w["a_in"].shape[1]
    return pl.pallas_call(
        _mla_proj_kernel,
        grid=(bsz, L // tm),
        in_specs=[pl.BlockSpec((1, tm, d), row),
                  pl.BlockSpec((1, d), const),
                  pl.BlockSpec((1, 1, d), per_b),
                  pl.BlockSpec((1, 1, d), per_b),
                  pl.BlockSpec((d, nin), const),
                  pl.BlockSpec((1, A_Q_LORA), const),
                  pl.BlockSpec((1, A_KV_LORA), const),
                  pl.BlockSpec((A_Q_LORA, N_HEADS * LANES), const),
                  pl.BlockSpec((A_Q_LORA, N_HEADS * LANES), const),
                  pl.BlockSpec((tm, LANES), lambda b, i: (i, 0)),
                  pl.BlockSpec((tm, LANES), lambda b, i: (i, 0))],
        out_specs=[pl.BlockSpec((1, tm, N_HEADS * LANES), row),
                   pl.BlockSpec((1, tm, A_KV_LORA), row),
                   pl.BlockSpec((1, tm, A_ROPE), row)],
        out_shape=[jax.ShapeDtypeStruct((bsz, L, N_HEADS * LANES), BF16),
                   jax.ShapeDtypeStruct((bsz, L, A_KV_LORA), F32),
                   jax.ShapeDtypeStruct((bsz, L, A_ROPE), F32)],
        compiler_params=_params("parallel", "parallel"),
        name="mla_proj",
    )(x, g, sc, sh, w["a_in"], w["a_gq"], w["a_gkv"], w["a_wq"], w["a_wqs"], cosp, sinp)


def _mla_expand_kernel(lat_ref, kr_ref, wk_ref, sel_ref, wv_ref, k_ref, v_ref):
    lat = lat_ref[0].astype(BF16)
    kr = kr_ref[0].astype(BF16)
    k = (jnp.dot(lat, wk_ref[...], preferred_element_type=F32)
         + jnp.dot(kr, sel_ref[...], preferred_element_type=F32))
    k_ref[0] = k.astype(BF16)
    v_ref[0] = jnp.dot(lat, wv_ref[...], preferred_element_type=F32).astype(BF16)


def _mla_expand(lat, kr, w, tm):
    bsz, L, _ = lat.shape
    row = lambda b, i: (b, i, 0)
    const = lambda b, i: (0, 0)
    return pl.pallas_call(
        _mla_expand_kernel,
        grid=(bsz, L // tm),
        in_specs=[pl.BlockSpec((1, tm, A_KV_LORA), row),
                  pl.BlockSpec((1, tm, A_ROPE), row),
                  pl.BlockSpec((A_KV_LORA, N_HEADS * LANES), const),
                  pl.BlockSpec((A_ROPE, N_HEADS * LANES), const),
                  pl.BlockSpec((A_KV_LORA, N_HEADS * A_V), const)],
        out_specs=[pl.BlockSpec((1, tm, N_HEADS * LANES), row),
                   pl.BlockSpec((1, tm, N_HEADS * A_V), row)],
        out_shape=[jax.ShapeDtypeStruct((bsz, L, N_HEADS * LANES), BF16),
                   jax.ShapeDtypeStruct((bsz, L, N_HEADS * A_V), BF16)],
        compiler_params=_params("parallel", "parallel"),
        name="mla_expand",
    )(lat, kr, w["a_wk"], w["a_sel"], w["a_wv"])


def _mla_proj_t_kernel(x_ref, g_ref, sc_ref, sh_ref, win_ref, gq_ref, gkv_ref, wqt_ref, wqst_ref,
                       cos_ref, sin_ref, cost_ref, sint_ref, qt_ref, lat_ref, kr_ref):
    cqn = _mla_in(x_ref, g_ref, sc_ref, sh_ref, win_ref, gq_ref, gkv_ref, cos_ref, sin_ref, lat_ref, kr_ref)
    a = lax.dot_general(wqt_ref[...], cqn, _NT, preferred_element_type=F32)
    b = lax.dot_general(wqst_ref[...], cqn, _NT, preferred_element_type=F32)
    cost = cost_ref[...]
    sint = sint_ref[...]
    for hd in range(N_HEADS):
        sl = slice(hd * LANES, (hd + 1) * LANES)
        qt_ref[0, sl, :] = ((a[sl] * cost + b[sl] * sint) * (A_SCALE * LOG2E)).astype(BF16)


def _mla_proj_t(x, g, sc, sh, w, cosp, sinp, tm):
    bsz, L, d = x.shape
    row = lambda b, i: (b, i, 0)
    per_b = lambda b, i: (b, 0, 0)
    const = lambda b, i: (0, 0)
    nin = w["a_in"].shape[1]
    hq = N_HEADS * LANES
    return pl.pallas_call(
        _mla_proj_t_kernel,
        grid=(bsz, L // tm),
        in_specs=[pl.BlockSpec((1, tm, d), row),
                  pl.BlockSpec((1, d), const),
                  pl.BlockSpec((1, 1, d), per_b),
                  pl.BlockSpec((1, 1, d), per_b),
                  pl.BlockSpec((d, nin), const),
                  pl.BlockSpec((1, A_Q_LORA), const),
                  pl.BlockSpec((1, A_KV_LORA), const),
                  pl.BlockSpec((hq, A_Q_LORA), const),
                  pl.BlockSpec((hq, A_Q_LORA), const),
                  pl.BlockSpec((tm, LANES), lambda b, i: (i, 0)),
                  pl.BlockSpec((tm, LANES), lambda b, i: (i, 0)),
                  pl.BlockSpec((LANES, tm), lambda b, i: (0, i)),
                  pl.BlockSpec((LANES, tm), lambda b, i: (0, i))],
        out_specs=[pl.BlockSpec((1, hq, tm), lambda b, i: (b, 0, i)),
                   pl.BlockSpec((1, tm, A_KV_LORA), row),
                   pl.BlockSpec((1, tm, A_ROPE), row)],
        out_shape=[jax.ShapeDtypeStruct((bsz, hq, L), BF16),
                   jax.ShapeDtypeStruct((bsz, L, A_KV_LORA), F32),
                   jax.ShapeDtypeStruct((bsz, L, A_ROPE), F32)],
        compiler_params=_params("parallel", "parallel"),
        name="mla_proj_t",
    )(x, g, sc, sh, w["a_in"], w["a_gq"], w["a_gkv"], w["a_wq"].T, w["a_wqs"].T, cosp, sinp, cosp.T, sinp.T)


def _mla_expand_t_kernel(lat_ref, kr_ref, wk_ref, sel_ref, wvt_ref, ones_ref, k_ref, vt_ref):
    lat = lat_ref[0].astype(BF16)
    kr = kr_ref[0].astype(BF16)
    k = (jnp.dot(lat, wk_ref[...], preferred_element_type=F32)
         + jnp.dot(kr, sel_ref[...], preferred_element_type=F32))
    k_ref[0] = k.astype(BF16)
    vt = lax.dot_general(wvt_ref[...], lat, _NT, preferred_element_type=F32) + ones_ref[...]
    vt_ref[0] = vt.astype(BF16)


def _mla_expand_t(lat, kr, w, tm):
    bsz, L, _ = lat.shape
    row = lambda b, i: (b, i, 0)
    const = lambda b, i: (0, 0)
    hv = N_HEADS * V_ROWS
    return pl.pallas_call(
        _mla_expand_t_kernel,
        grid=(bsz, L // tm),
        in_specs=[pl.BlockSpec((1, tm, A_KV_LORA), row),
                  pl.BlockSpec((1, tm, A_ROPE), row),
                  pl.BlockSpec((A_KV_LORA, N_HEADS * LANES), const),
                  pl.BlockSpec((A_ROPE, N_HEADS * LANES), const),
                  pl.BlockSpec((hv, A_KV_LORA), const),
                  pl.BlockSpec((hv, 1), const)],
        out_specs=[pl.BlockSpec((1, tm, N_HEADS * LANES), row),
                   pl.BlockSpec((1, hv, tm), lambda b, i: (b, 0, i))],
        out_shape=[jax.ShapeDtypeStruct((bsz, L, N_HEADS * LANES), BF16),
                   jax.ShapeDtypeStruct((bsz, hv, L), BF16)],
        compiler_params=_params("parallel", "parallel"),
        name="mla_expand_t",
    )(lat, kr, w["a_wk"], w["a_sel"], w["a_wvt"], w["ones_rows"])


def _flash_kernel(*refs, tq, tk, qpos0, per_frame, wide, decay, nk_tiles):
    if decay:
        q_ref, k_ref, v_ref, qd_ref, kd_ref, o_ref, m_sc, l_sc, acc_sc = refs
    else:
        q_ref, k_ref, v_ref, o_ref, m_sc, l_sc, acc_sc = refs
    i = pl.program_id(2)
    q0 = qpos0 + i * tq
    if per_frame:
        vis_all = q0 + 1
        vis_any = q0 + tq
    else:
        vis_all = ((q0 >> CHUNK_SHIFT) + 1) * CHUNK
        vis_any = (((q0 + tq - 1) >> CHUNK_SHIFT) + 1) * CHUNK
    n_full = jnp.minimum(vis_all // tk, nk_tiles)
    n_tot = jnp.minimum((vis_any + tk - 1) // tk, nk_tiles)

    q = q_ref[0]
    lane = lax.broadcasted_iota(I32, (1, LANES), 1)
    first = lane < HEAD_DIM
    if wide:
        qs = (q[:, :LANES], q[:, LANES:])
    else:
        zero = jnp.zeros_like(q)
        qs = (jnp.where(first, q, zero), jnp.where(first, zero, q))

    m_sc[...] = jnp.full(m_sc.shape, NEG_INF, F32)
    l_sc[...] = jnp.zeros(l_sc.shape, F32)
    acc_sc[...] = jnp.zeros(acc_sc.shape, F32)
    qp = q0 + lax.broadcasted_iota(I32, (tq, 1), 0)

    def tile(j, masked):
        ks = pl.multiple_of(j * tk, tk)
        kt = k_ref[0, pl.ds(ks, tk), :]
        vt = v_ref[0, pl.ds(ks, tk), :]
        if masked:
            kp = ks + lax.broadcasted_iota(I32, (1, tk), 1)
            if per_frame:
                mask = kp <= qp
            else:
                mask = (kp >> CHUNK_SHIFT) <= (qp >> CHUNK_SHIFT)
        for hh in range(2):
            kk = kt[:, hh * LANES:(hh + 1) * LANES] if wide else kt
            s = lax.dot_general(qs[hh], kk, _NT, preferred_element_type=F32)
            if decay:
                s = s + (qd_ref[0, hh] - kd_ref[0, hh, :, pl.ds(ks, tk)])
            if masked:
                s = jnp.where(mask, s, NEG_INF)
            m_old = m_sc[hh]
            m_new = jnp.maximum(m_old, jnp.max(s, axis=1, keepdims=True))
            p = jnp.exp(s - m_new)
            alpha = jnp.exp(m_old - m_new)
            l_sc[hh] = alpha * l_sc[hh] + jnp.sum(p, axis=1, keepdims=True)
            acc_sc[hh] = alpha * acc_sc[hh] + jnp.dot(p.astype(BF16), vt, preferred_element_type=F32)
            m_sc[hh] = m_new

    def full_body(j, c):
        tile(j, False)
        return c

    def diag_body(j, c):
        tile(j, True)
        return c

    lax.fori_loop(0, n_full, full_body, 0)
    lax.fori_loop(n_full, n_tot, diag_body, 0)
    o0 = acc_sc[0] / l_sc[0]
    o1 = acc_sc[1] / l_sc[1]
    o_ref[0] = jnp.where(first, o0, o1).astype(BF16)


def _flash(q, k, v, *, tq, tk, qpos0, per_frame, wide, qd=None, kd=None):
    bsz, lq, _ = q.shape
    lk = k.shape[1]
    qw = 2 * LANES if wide else LANES
    decay = qd is not None
    in_specs = [pl.BlockSpec((1, tq, qw), lambda b, hp, i: (b, i, hp)),
                pl.BlockSpec((1, lk, qw), lambda b, hp, i: (b, 0, hp)),
                pl.BlockSpec((1, lk, LANES), lambda b, hp, i: (b, 0, hp))]
    args = [q, k, v]
    if decay:
        in_specs += [pl.BlockSpec((1, 2, tq, 1), lambda b, hp, i: (b, hp, i, 0)),
                     pl.BlockSpec((1, 2, 1, lk), lambda b, hp, i: (b, hp, 0, 0))]
        args += [qd, kd]
    kern = functools.partial(_flash_kernel, tq=tq, tk=tk, qpos0=qpos0, per_frame=per_frame,
                             wide=wide, decay=decay, nk_tiles=lk // tk)
    return pl.pallas_call(
        kern,
        grid=(bsz, N_HEADS // 2, lq // tq),
        in_specs=in_specs,
        out_specs=pl.BlockSpec((1, tq, LANES), lambda b, hp, i: (b, i, hp)),
        out_shape=jax.ShapeDtypeStruct((bsz, lq, N_HEADS * HEAD_DIM), BF16),
        scratch_shapes=[pltpu.VMEM((2, tq, 1), F32), pltpu.VMEM((2, tq, 1), F32),
                        pltpu.VMEM((2, tq, LANES), F32)],
        compiler_params=_params("parallel", "parallel", "arbitrary"),
        name="flash_attention",
    )(*args)


DECAY_K_ROW = HEAD_DIM
DECAY_Q_ROW = HEAD_DIM + 3


def _flash_t_kernel(*refs, tile, per_frame, decay):
    if decay:
        qt_ref, k_ref, vt_ref, qd_ref, o_ref, m_sc, acc_sc = refs
    else:
        qt_ref, k_ref, vt_ref, o_ref, m_sc, acc_sc = refs
    i = pl.program_id(2)
    q0 = i * tile
    qt = qt_ref[0]
    qs = []
    for hh in range(2):
        blk = qt[hh * LANES:(hh + 1) * LANES]
        if decay:
            hi, mid, lo = _split3(qd_ref[0, 0, hh:hh + 1, :])
            row = lax.broadcasted_iota(I32, (LANES, tile), 0)
            aug = jnp.where(row == DECAY_Q_ROW, hi, jnp.where(row == DECAY_Q_ROW + 1, mid,
                            jnp.where(row == DECAY_Q_ROW + 2, lo, 0.0)))
            aug = jnp.where((row >= DECAY_K_ROW) & (row < DECAY_Q_ROW), 1.0, aug)
            blk = (blk.astype(F32) + aug).astype(BF16)
        qs.append(blk)
    m_sc[...] = jnp.full(m_sc.shape, NEG_INF, F32)
    acc_sc[...] = jnp.zeros(acc_sc.shape, F32)
    qp = q0 + lax.broadcasted_iota(I32, (1, tile), 1)

    def step(j, masked):
        ks = pl.multiple_of(j * tile, tile)
        kt = k_ref[0, pl.ds(ks, tile), :]
        vt = vt_ref[0, :, pl.ds(ks, tile)]
        if masked:
            kp = ks + lax.broadcasted_iota(I32, (tile, 1), 0)
            mask = (kp <= qp) if per_frame else ((kp >> CHUNK_SHIFT) <= (qp >> CHUNK_SHIFT))
        ss = [jnp.dot(kt[:, hh * LANES:(hh + 1) * LANES], qs[hh], preferred_element_type=F32) for hh in range(2)]
        if masked:
            ss = [jnp.where(mask, s, NEG_INF) for s in ss]
        m_old = m_sc[...]
        m_new = [jnp.maximum(m_old[hh:hh + 1], jnp.max(ss[hh], axis=0, keepdims=True)) for hh in range(2)]
        ps = [jnp.exp2(ss[hh] - m_new[hh]).astype(BF16) for hh in range(2)]
        pvs = [jnp.dot(vt[hh * V_ROWS:(hh + 1) * V_ROWS], ps[hh], preferred_element_type=F32) for hh in range(2)]
        for hh in range(2):
            acc_sc[hh] = jnp.exp2(m_old[hh:hh + 1] - m_new[hh]) * acc_sc[hh] + pvs[hh]
        m_sc[...] = jnp.concatenate(m_new, axis=0)

    def full_body(j, c):
        step(j, False)
        return c

    lax.fori_loop(0, i, full_body, 0)
    step(i, True)
    for hh in range(2):
        a = acc_sc[hh]
        o_ref[0, hh * HEAD_DIM:(hh + 1) * HEAD_DIM, :] = (a[:HEAD_DIM] / a[HEAD_DIM:HEAD_DIM + 1]).astype(BF16)


def _flash_t(qt, k, vt, *, per_frame, qd=None):
    bsz, _, L = qt.shape
    tile = FLASH_T_TILE
    decay = qd is not None
    in_specs = [pl.BlockSpec((1, 2 * LANES, tile), lambda b, hp, i: (b, hp, i)),
                pl.BlockSpec((1, L, 2 * LANES), lambda b, hp, i: (b, 0, hp)),
                pl.BlockSpec((1, 2 * V_ROWS, L), lambda b, hp, i: (b, hp, 0))]
    args = [qt, k, vt]
    if decay:
        in_specs.append(pl.BlockSpec((1, 1, 2, tile), lambda b, hp, i: (b, hp, 0, i)))
        args.append(qd.reshape(bsz, N_HEADS // 2, 2, L))
    return pl.pallas_call(
        functools.partial(_flash_t_kernel, tile=tile, per_frame=per_frame, decay=decay),
        grid=(bsz, N_HEADS // 2, L // tile),
        in_specs=in_specs,
        out_specs=pl.BlockSpec((1, 2 * HEAD_DIM, tile), lambda b, hp, i: (b, hp, i)),
        out_shape=jax.ShapeDtypeStruct((bsz, N_HEADS * HEAD_DIM, L), BF16),
        scratch_shapes=[pltpu.VMEM((2, tile), F32), pltpu.VMEM((2, V_ROWS, tile), F32)],
        compiler_params=_params("parallel", "parallel", "arbitrary"),
        name="flash_attention_t",
    )(*args)


def _fox_proj_kernel(x_ref, g_ref, sc_ref, sh_ref, wq_ref, wk_ref, wv_ref, wf_ref, bf_ref,
                     q_ref, k_ref, v_ref, kb_ref, vb_ref, lf_ref):
    h = _modnorm(x_ref[0], g_ref[...], sc_ref[0], sh_ref[0]).astype(BF16)
    q_ref[0] = (jnp.dot(h, wq_ref[...], preferred_element_type=F32) * HEAD_SCALE).astype(BF16)
    k = jnp.dot(h, wk_ref[...], preferred_element_type=F32)
    k_ref[0] = k
    kb_ref[0] = k.astype(BF16)
    v = jnp.dot(h, wv_ref[...], preferred_element_type=F32)
    v_ref[0] = v
    vb_ref[0] = v.astype(BF16)
    f = jnp.dot(h, wf_ref[...], preferred_element_type=F32)[:, :N_HEADS] + bf_ref[...]
    lf_ref[0] = _log_sigmoid(f)


def _fox_proj(x, g, sc, sh, w, tm):
    bsz, L, d = x.shape
    hd = N_HEADS * HEAD_DIM
    row = lambda b, i: (b, i, 0)
    per_b = lambda b, i: (b, 0, 0)
    const = lambda b, i: (0, 0)
    return pl.pallas_call(
        _fox_proj_kernel,
        grid=(bsz, L // tm),
        in_specs=[pl.BlockSpec((1, tm, d), row),
                  pl.BlockSpec((1, d), const),
                  pl.BlockSpec((1, 1, d), per_b),
                  pl.BlockSpec((1, 1, d), per_b),
                  pl.BlockSpec((d, hd), const),
                  pl.BlockSpec((d, hd), const),
                  pl.BlockSpec((d, hd), const),
                  pl.BlockSpec((d, LANES), const),
                  pl.BlockSpec((1, N_HEADS), const)],
        out_specs=[pl.BlockSpec((1, tm, hd), row)] * 5 + [pl.BlockSpec((1, tm, N_HEADS), row)],
        out_shape=[jax.ShapeDtypeStruct((bsz, L, hd), BF16),
                   jax.ShapeDtypeStruct((bsz, L, hd), F32),
                   jax.ShapeDtypeStruct((bsz, L, hd), F32),
                   jax.ShapeDtypeStruct((bsz, L, hd), BF16),
                   jax.ShapeDtypeStruct((bsz, L, hd), BF16),
                   jax.ShapeDtypeStruct((bsz, L, N_HEADS), F32)],
        compiler_params=_params("parallel", "parallel"),
        name="fox_proj",
    )(x, g, sc, sh, w["b_wq"], w["b_wk"], w["b_wv"], w["b_wf"], w["b_bf"])


def _cumsum_kernel(x_ref, o_ref, carry_ref, *, tc):
    @pl.when(pl.program_id(1) == 0)
    def _():
        carry_ref[...] = jnp.zeros(carry_ref.shape, F32)

    r = lax.broadcasted_iota(I32, (tc, tc), 0)
    c = lax.broadcasted_iota(I32, (tc, tc), 1)
    upper = (r <= c).astype(BF16)
    h1, h2, h3 = [a.astype(BF16) for a in _split3(x_ref[0])]
    d = lambda a: jnp.dot(a, upper, preferred_element_type=F32)
    cum = ((d(h3) + d(h2)) + d(h1)) + carry_ref[...]
    o_ref[0] = cum
    carry_ref[...] = cum[:, tc - 1:tc]


def _cumsum_rows(x, tc):
    bsz, nh, L = x.shape
    return pl.pallas_call(
        functools.partial(_cumsum_kernel, tc=tc),
        grid=(bsz, L // tc),
        in_specs=[pl.BlockSpec((1, nh, tc), lambda b, j: (b, 0, j))],
        out_specs=pl.BlockSpec((1, nh, tc), lambda b, j: (b, 0, j)),
        out_shape=jax.ShapeDtypeStruct((bsz, nh, L), F32),
        scratch_shapes=[pltpu.VMEM((nh, 1), F32)],
        compiler_params=_params("parallel", "arbitrary"),
        name="cumsum",
    )(x)


def _fox_proj_t_kernel(x_ref, g_ref, sc_ref, sh_ref, wqt_ref, wk_ref, wv_ref, wvt_ref, ones_ref,
                       wf_ref, bf_ref, wft_ref, bft_ref, qt_ref, k_ref, v_ref, vt_ref, lf_ref, lft_ref):
    h = _modnorm(x_ref[0], g_ref[...], sc_ref[0], sh_ref[0]).astype(BF16)
    qt = lax.dot_general(wqt_ref[...], h, _NT, preferred_element_type=F32)
    qt_ref[0] = (qt * (HEAD_SCALE * LOG2E)).astype(BF16)
    k_ref[0] = jnp.dot(h, wk_ref[...], preferred_element_type=F32)
    v_ref[0] = jnp.dot(h, wv_ref[...], preferred_element_type=F32)
    vt = lax.dot_general(wvt_ref[...], h, _NT, preferred_element_type=F32) + ones_ref[...]
    vt_ref[0] = vt.astype(BF16)
    f = jnp.dot(h, wf_ref[...], preferred_element_type=F32)[:, :N_HEADS] + bf_ref[...]
    lf_ref[0] = _log_sigmoid(f)
    ft = lax.dot_general(wft_ref[...], h, _NT, preferred_element_type=F32)[:N_HEADS] + bft_ref[...]
    lft_ref[0] = _log_sigmoid(ft)


def _fox_proj_t(x, g, sc, sh, w, tm):
    bsz, L, d = x.shape
    hd = N_HEADS * HEAD_DIM
    hq = N_HEADS * LANES
    hv = N_HEADS * V_ROWS
    row = lambda b, i: (b, i, 0)
    col = lambda b, i: (b, 0, i)
    per_b = lambda b, i: (b, 0, 0)
    const = lambda b, i: (0, 0)
    return pl.pallas_call(
        _fox_proj_t_kernel,
        grid=(bsz, L // tm),
        in_specs=[pl.BlockSpec((1, tm, d), row),
                  pl.BlockSpec((1, d), const),
                  pl.BlockSpec((1, 1, d), per_b),
                  pl.BlockSpec((1, 1, d), per_b),
                  pl.BlockSpec((hq, d), const),
                  pl.BlockSpec((d, hd), const),
                  pl.BlockSpec((d, hd), const),
                  pl.BlockSpec((hv, d), const),
                  pl.BlockSpec((hv, 1), const),
                  pl.BlockSpec((d, LANES), const),
                  pl.BlockSpec((1, N_HEADS), const),
                  pl.BlockSpec((LANES, d), const),
                  pl.BlockSpec((N_HEADS, 1), const)],
        out_specs=[pl.BlockSpec((1, hq, tm), col), pl.BlockSpec((1, tm, hd), row), pl.BlockSpec((1, tm, hd), row),
                   pl.BlockSpec((1, hv, tm), col), pl.BlockSpec((1, tm, N_HEADS), row),
                   pl.BlockSpec((1, N_HEADS, tm), col)],
        out_shape=[jax.ShapeDtypeStruct((bsz, hq, L), BF16),
                   jax.ShapeDtypeStruct((bsz, L, hd), F32),
                   jax.ShapeDtypeStruct((bsz, L, hd), F32),
                   jax.ShapeDtypeStruct((bsz, hv, L), BF16),
                   jax.ShapeDtypeStruct((bsz, L, N_HEADS), F32),
                   jax.ShapeDtypeStruct((bsz, N_HEADS, L), F32)],
        compiler_params=_params("parallel", "parallel"),
        name="fox_proj_t",
    )(x, g, sc, sh, w["b_wqt"], w["b_wk"], w["b_wv"], w["b_wvt"], w["ones_rows"],
      w["b_wf"], w["b_bf"], w["b_wf"].T, w["b_bf"].T)


def _fox_cumaug_kernel(lft_ref, lf_ref, k_ref, cumt_ref, kaug_ref, crow_sc, ccol_sc, *, tc):
    @pl.when(pl.program_id(1) == 0)
    def _():
        crow_sc[...] = jnp.zeros(crow_sc.shape, F32)
        ccol_sc[...] = jnp.zeros(ccol_sc.shape, F32)

    r = lax.broadcasted_iota(I32, (tc, tc), 0)
    c = lax.broadcasted_iota(I32, (tc, tc), 1)
    upper = (r <= c).astype(BF16)
    lower = (c <= r).astype(BF16)
    xh, xm, xl = [a.astype(BF16) for a in _split3(lft_ref[0])]
    dr = lambda a: jnp.dot(a, upper, preferred_element_type=F32)
    cumt = ((dr(xl) + dr(xm)) + dr(xh)) + crow_sc[...]
    crow_sc[...] = cumt[:, tc - 1:tc]
    cumt_ref[0] = cumt * LOG2E
    yh, ym, yl = [a.astype(BF16) for a in _split3(lf_ref[0])]
    dc = lambda a: jnp.dot(lower, a, preferred_element_type=F32)
    cum = ((dc(yl) + dc(ym)) + dc(yh)) + ccol_sc[...]
    ccol_sc[...] = cum[tc - 1:tc, :]
    neg = cum * (-LOG2E)
    k = k_ref[0]
    lane = lax.broadcasted_iota(I32, (tc, LANES - HEAD_DIM), 1)
    for h in range(N_HEADS):
        hi, mid, lo = _split3(neg[:, h:h + 1])
        aug = jnp.where(lane == 0, hi, jnp.where(lane == 1, mid, jnp.where(lane == 2, lo, 0.0)))
        aug = jnp.where((lane >= DECAY_Q_ROW - HEAD_DIM) & (lane < DECAY_Q_ROW - HEAD_DIM + 3), 1.0, aug)
        kaug_ref[0, :, h * LANES:(h + 1) * LANES] = jnp.concatenate(
            [k[:, h * HEAD_DIM:(h + 1) * HEAD_DIM], aug], axis=1).astype(BF16)


def _fox_cumaug(lft, lf, k, tc):
    bsz, nh, L = lft.shape
    hd = k.shape[2]
    return pl.pallas_call(
        functools.partial(_fox_cumaug_kernel, tc=tc),
        grid=(bsz, L // tc),
        in_specs=[pl.BlockSpec((1, nh, tc), lambda b, j: (b, 0, j)),
                  pl.BlockSpec((1, tc, nh), lambda b, j: (b, j, 0)),
                  pl.BlockSpec((1, tc, hd), lambda b, j: (b, j, 0))],
        out_specs=[pl.BlockSpec((1, nh, tc), lambda b, j: (b, 0, j)),
                   pl.BlockSpec((1, tc, nh * LANES), lambda b, j: (b, j, 0))],
        out_shape=[jax.ShapeDtypeStruct((bsz, nh, L), F32),
                   jax.ShapeDtypeStruct((bsz, L, nh * LANES), BF16)],
        scratch_shapes=[pltpu.VMEM((nh, 1), F32), pltpu.VMEM((1, nh), F32)],
        compiler_params=_params("parallel", "arbitrary"),
        name="fox_cumsum_aug",
    )(lft, lf, k)


def _dsa_proj_kernel(x_ref, g_ref, sc_ref, sh_ref, wqt_ref, wqit_ref, wsm_ref, wsmt_ref,
                     qt_ref, qit_ref, k_ref, v_ref, ki_ref, wit_ref, kaug_ref, vt_ref, kib_ref):
    h = _modnorm(x_ref[0], g_ref[...], sc_ref[0], sh_ref[0]).astype(BF16)
    tm = h.shape[0]
    qt = lax.dot_general(wqt_ref[...], h, _NT, preferred_element_type=F32)
    qt_ref[0] = (qt * (HEAD_SCALE * LOG2E)).astype(BF16)
    qit = lax.dot_general(wqit_ref[...], h, _NT, preferred_element_type=F32)
    qit_ref[0] = (qit * (C_IDX_DIM ** -0.5)).astype(BF16)
    sm = jnp.dot(h, wsm_ref[...], preferred_element_type=F32)
    smt = lax.dot_general(wsmt_ref[...], h, _NT, preferred_element_type=F32)
    k = sm[:, :HEAD_DIM]
    ki = sm[:, 2 * HEAD_DIM:2 * HEAD_DIM + C_IDX_DIM]
    k_ref[0] = k
    v_ref[0] = sm[:, HEAD_DIM:2 * HEAD_DIM]
    ki_ref[0] = ki
    kib_ref[0] = ki.astype(BF16)
    lane = lax.broadcasted_iota(I32, (tm, LANES - HEAD_DIM), 1)
    ones_cols = jnp.where(lane < 2, 1.0, 0.0)
    kaug_ref[0] = jnp.concatenate([k, ones_cols], axis=1).astype(BF16)
    row = lax.broadcasted_iota(I32, (V_ROWS - HEAD_DIM, tm), 0)
    ones_row = jnp.where(row == 0, 1.0, 0.0)
    vt_ref[0] = jnp.concatenate([smt[HEAD_DIM:2 * HEAD_DIM], ones_row], axis=0).astype(BF16)
    o = 2 * HEAD_DIM + C_IDX_DIM
    wit_ref[0] = smt[o:o + C_IDX_HEADS] * (C_IDX_HEADS ** -0.5)


def _dsa_proj(x, g, sc, sh, w, tm):
    bsz, L, d = x.shape
    hi = C_IDX_HEADS * C_IDX_DIM
    hq = N_HEADS * LANES
    row = lambda b, i: (b, i, 0)
    col = lambda b, i: (b, 0, i)
    per_b = lambda b, i: (b, 0, 0)
    const = lambda b, i: (0, 0)
    small = lambda n, dt: jax.ShapeDtypeStruct((bsz, L, n), dt)
    tall = lambda n, dt: jax.ShapeDtypeStruct((bsz, n, L), dt)
    return pl.pallas_call(
        _dsa_proj_kernel,
        grid=(bsz, L // tm),
        in_specs=[pl.BlockSpec((1, tm, d), row),
                  pl.BlockSpec((1, d), const),
                  pl.BlockSpec((1, 1, d), per_b),
                  pl.BlockSpec((1, 1, d), per_b),
                  pl.BlockSpec((hq, d), const),
                  pl.BlockSpec((hi, d), const),
                  pl.BlockSpec((d, 2 * LANES), const),
                  pl.BlockSpec((2 * LANES, d), const)],
        out_specs=[pl.BlockSpec((1, hq, tm), col), pl.BlockSpec((1, hi, tm), col),
                   pl.BlockSpec((1, tm, HEAD_DIM), row), pl.BlockSpec((1, tm, HEAD_DIM), row),
                   pl.BlockSpec((1, tm, C_IDX_DIM), row), pl.BlockSpec((1, C_IDX_HEADS, tm), col),
                   pl.BlockSpec((1, tm, LANES), row), pl.BlockSpec((1, V_ROWS, tm), col),
                   pl.BlockSpec((1, tm, C_IDX_DIM), row)],
        out_shape=[tall(hq, BF16), tall(hi, BF16), small(HEAD_DIM, F32), small(HEAD_DIM, F32),
                   small(C_IDX_DIM, F32), tall(C_IDX_HEADS, F32),
                   small(LANES, BF16), tall(V_ROWS, BF16), small(C_IDX_DIM, BF16)],
        compiler_params=_params("parallel", "parallel"),
        name="dsa_proj",
    )(x, g, sc, sh, w["c_wqt"], w["c_wqit"], w["c_wsm"], w["c_wsmt"])


_INT_MIN = -2 ** 31
_COUNT_ROWS = 64


def _dsa_kernel(qt_ref, qit_ref, wit_ref, k_ref, vt_ref, ki_ref, farq_ref, nb_ref, o_ref,
                sk_sc, m_sc, acc_sc, *, qpos0, n_sel, nk_tiles):
    tq, tk = DSA_TQ, DSA_TK
    i = pl.program_id(1)
    q0 = qpos0 + i * tq
    nt = jnp.minimum((q0 + tq + tk - 1) // tk, nk_tiles)
    qch = (q0 + lax.broadcasted_iota(I32, (1, tq), 1)) >> CHUNK_SHIFT

    def admissible(ks, w):
        kp = ks + lax.broadcasted_iota(I32, (w, 1), 0)
        return (kp >> CHUNK_SHIFT) <= qch

    qit = qit_ref[0]
    qis = jnp.concatenate([qit[h * C_IDX_DIM:(h + 1) * C_IDX_DIM] for h in range(C_IDX_HEADS)], axis=1)
    wit = wit_ref[0]

    def score_body(j, c):
        ks = pl.multiple_of(j * tk, tk)
        d = jnp.dot(ki_ref[0, pl.ds(ks, tk), :], qis, preferred_element_type=F32)
        sc = jnp.zeros((tk, tq), F32)
        for h in range(C_IDX_HEADS):
            sc = sc + jnp.maximum(d[:, h * tq:(h + 1) * tq], 0.0) * wit[h:h + 1]
        sc = jnp.where(sc == 0.0, 0.0, sc)
        sc = jnp.where(admissible(ks, tk), sc, NEG_INF)
        bits = pltpu.bitcast(sc, I32)
        sk_sc[pl.ds(ks, tk), :] = bits ^ ((bits >> 31) & 0x7FFFFFFF)
        return c

    lax.fori_loop(0, nt, score_body, 0)
    tiles_per_step = DSA_SEARCH_ROWS // tk
    n_steps = (nt + tiles_per_step - 1) // tiles_per_step

    @pl.when(nt < n_steps * tiles_per_step)
    def _():
        sk_sc[pl.ds(pl.multiple_of(nt * tk, tk), tk), :] = jnp.full((tk, tq), _INT_MIN, I32)

    def count(pred):
        def body(j, c):
            kt = sk_sc[pl.ds(pl.multiple_of(j * DSA_SEARCH_ROWS, DSA_SEARCH_ROWS), DSA_SEARCH_ROWS), :]
            g = jnp.where(pred(kt), 1.0, 0.0)
            parts = [g[r * _COUNT_ROWS:(r + 1) * _COUNT_ROWS] for r in range(DSA_SEARCH_ROWS // _COUNT_ROWS)]
            while len(parts) > 1:
                parts = [parts[a] + parts[a + 1] for a in range(0, len(parts), 2)]
            return c + parts[0]
        c = lax.fori_loop(0, n_steps, body, jnp.zeros((_COUNT_ROWS, tq), F32))
        return jnp.sum(c, axis=0, keepdims=True)

    nsel = float(n_sel)
    lo = jnp.where(count(lambda kt: kt >= 0) >= nsel, 0, _INT_MIN).astype(I32)

    def bit_body(t, lo):
        cand = lo | jnp.left_shift(jnp.int32(1), 30 - t)
        return jnp.where(count(lambda kt: kt >= cand) >= nsel, cand, lo)

    thr = lax.fori_loop(0, 31, bit_body, lo)
    need = nsel - count(lambda kt: kt > thr)

    qt = qt_ref[0]
    qs = jnp.concatenate([qt[h * LANES:(h + 1) * LANES] for h in range(N_HEADS)], axis=1) + farq_ref[...]
    m_sc[...] = jnp.full(m_sc.shape, NEG_INF, F32)
    acc_sc[...] = jnp.zeros(acc_sc.shape, F32)
    ra = lax.broadcasted_iota(I32, (tk, tk), 0)
    ca = lax.broadcasted_iota(I32, (tk, tk), 1)
    earlier = (ca < ra).astype(BF16)

    def attend(ks, w, kind, run):
        kt = sk_sc[pl.ds(ks, w), :]
        eq = kt == thr
        rank = run + jnp.dot(earlier[:w, :w], jnp.where(eq, 1.0, 0.0).astype(BF16), preferred_element_type=F32)
        sel = ((kt > thr) | (eq & (rank < need))) & admissible(ks, w)
        run = run + jnp.sum(jnp.where(eq, 1.0, 0.0), axis=0, keepdims=True)
        s = jnp.dot(k_ref[0, pl.ds(ks, w), :], qs, preferred_element_type=F32)
        ps, alphas = [], []
        for h in range(N_HEADS):
            sl = slice(h * tq, (h + 1) * tq)
            sh = s[:, sl]
            if kind is not None:
                sh = sh + nb_ref[kind, :, sl]
            sh = jnp.where(sel, sh, NEG_INF)
            m_old = m_sc[:, sl]
            m_new = jnp.maximum(m_old, jnp.max(sh, axis=0, keepdims=True))
            m_sc[:, sl] = m_new
            alphas.append(jnp.exp2(m_old - m_new))
            ps.append(jnp.exp2(sh - m_new).astype(BF16))
        p = jnp.concatenate(ps, axis=1)
        alpha = jnp.concatenate(alphas, axis=1)
        pv = jnp.dot(vt_ref[0, :, pl.ds(ks, w)], p, preferred_element_type=F32)
        acc_sc[...] = alpha * acc_sc[...] + pv
        return run

    n_far = jnp.maximum(q0 - tq, 0) // tk

    def far_body(j, run):
        return attend(pl.multiple_of(j * tk, tk), tk, None, run)

    run = lax.fori_loop(0, n_far, far_body, jnp.zeros((1, tq), F32))
    ks0 = n_far * tk
    n_tail = (q0 + tq - ks0) // tq

    def tail_body(t, run):
        ks = pl.multiple_of(ks0 + t * tq, tq)
        kind = jnp.clip((ks - q0) // tq + 2, 0, 2)
        return attend(ks, tq, kind, run)

    lax.fori_loop(0, n_tail, tail_body, run)
    acc = acc_sc[...]
    ot = jnp.concatenate([acc[:HEAD_DIM, h * tq:(h + 1) * tq] / acc[HEAD_DIM:HEAD_DIM + 1, h * tq:(h + 1) * tq]
                          for h in range(N_HEADS)], axis=0)
    o_ref[0] = ot.T.astype(BF16)


def _dsa_attention(qt, qit, wit, k, vt, ki, farq, nb, *, qpos0, n_sel):
    bsz, hq, lq = qt.shape
    lk = k.shape[1]
    hi = qit.shape[1]
    hd = N_HEADS * HEAD_DIM
    col = lambda b, i: (b, 0, i)
    whole = lambda b, i: (b, 0, 0)
    kern = functools.partial(_dsa_kernel, qpos0=qpos0, n_sel=n_sel, nk_tiles=lk // DSA_TK)
    return pl.pallas_call(
        kern,
        grid=(bsz, lq // DSA_TQ),
        in_specs=[pl.BlockSpec((1, hq, DSA_TQ), col),
                  pl.BlockSpec((1, hi, DSA_TQ), col),
                  pl.BlockSpec((1, C_IDX_HEADS, DSA_TQ), col),
                  pl.BlockSpec((1, lk, LANES), whole),
                  pl.BlockSpec((1, V_ROWS, lk), whole),
                  pl.BlockSpec((1, lk, C_IDX_DIM), whole),
                  pl.BlockSpec((LANES, N_HEADS * DSA_TQ), lambda b, i: (0, 0)),
                  pl.BlockSpec((3, DSA_TQ, N_HEADS * DSA_TQ), lambda b, i: (0, 0, 0))],
        out_specs=pl.BlockSpec((1, DSA_TQ, hd), lambda b, i: (b, i, 0)),
        out_shape=jax.ShapeDtypeStruct((bsz, lq, hd), BF16),
        scratch_shapes=[pltpu.VMEM((lk, DSA_TQ), I32),
                        pltpu.VMEM((1, N_HEADS * DSA_TQ), F32),
                        pltpu.VMEM((V_ROWS, N_HEADS * DSA_TQ), F32)],
        compiler_params=_params("parallel", "arbitrary"),
        name="dsa_attention",
    )(qt, qit, wit, k, vt, ki, farq, nb)


def _swa_proj_kernel(x_ref, g_ref, sc_ref, sh_ref, wq_ref, wkv_ref, q_ref, kv_ref, kvb_ref):
    h = _modnorm(x_ref[0], g_ref[...], sc_ref[0], sh_ref[0]).astype(BF16)
    q_ref[0] = (jnp.dot(h, wq_ref[...], preferred_element_type=F32) * HEAD_SCALE).astype(BF16)
    kv = jnp.dot(h, wkv_ref[...], preferred_element_type=F32)
    kv_ref[0] = kv
    kvb_ref[0] = kv.astype(BF16)


def _swa_proj(x, g, sc, sh, w, tm):
    bsz, L, d = x.shape
    hd = N_HEADS * HEAD_DIM
    kvw = 2 * D_KV_HEADS * HEAD_DIM
    row = lambda b, i: (b, i, 0)
    per_b = lambda b, i: (b, 0, 0)
    const = lambda b, i: (0, 0)
    return pl.pallas_call(
        _swa_proj_kernel,
        grid=(bsz, L // tm),
        in_specs=[pl.BlockSpec((1, tm, d), row),
                  pl.BlockSpec((1, d), const),
                  pl.BlockSpec((1, 1, d), per_b),
                  pl.BlockSpec((1, 1, d), per_b),
                  pl.BlockSpec((d, hd), const),
                  pl.BlockSpec((d, kvw), const)],
        out_specs=[pl.BlockSpec((1, tm, hd), row), pl.BlockSpec((1, tm, kvw), row),
                   pl.BlockSpec((1, tm, kvw), row)],
        out_shape=[jax.ShapeDtypeStruct((bsz, L, hd), BF16),
                   jax.ShapeDtypeStruct((bsz, L, kvw), F32),
                   jax.ShapeDtypeStruct((bsz, L, kvw), BF16)],
        compiler_params=_params("parallel", "parallel"),
        name="swa_proj",
    )(x, g, sc, sh, w["d_wq"], w["d_wkv"])


def _swa_kernel(q_ref, kv_ref, nb_ref, sink_ref, o_ref, *, mask_front):
    band = WINDOW + CHUNK
    c = pl.program_id(1)
    start = pl.multiple_of(c * CHUNK, CHUNK)
    kvb = kv_ref[0, pl.ds(start, band), :]
    q = q_ref[0]
    if mask_front:
        ok = (start - WINDOW + lax.broadcasted_iota(I32, (1, 1, band), 2)) >= 0
    kw = D_KV_HEADS * HEAD_DIM
    for g in range(D_KV_HEADS):
        kg = kvb[:, g * HEAD_DIM:(g + 1) * HEAD_DIM]
        vg = kvb[:, kw + g * HEAD_DIM:kw + (g + 1) * HEAD_DIM]
        qg = jnp.concatenate([q[:, (g * D_REP + r) * HEAD_DIM:(g * D_REP + r + 1) * HEAD_DIM]
                              for r in range(D_REP)], axis=0)
        s = lax.dot_general(qg, kg, _NT, preferred_element_type=F32).reshape(D_REP, CHUNK, band)
        s = s + nb_ref[g * D_REP:(g + 1) * D_REP]
        if mask_front:
            s = jnp.where(ok, s, NEG_INF)
        sink = sink_ref[g * D_REP:(g + 1) * D_REP]
        m = jnp.maximum(jnp.max(s, axis=2, keepdims=True), sink)
        e = jnp.exp(s - m)
        p = e / (jnp.sum(e, axis=2, keepdims=True) + jnp.exp(sink - m))
        o = jnp.dot(p.reshape(D_REP * CHUNK, band).astype(BF16), vg, preferred_element_type=F32)
        for r in range(D_REP):
            hh = g * D_REP + r
            o_ref[0, :, hh * HEAD_DIM:(hh + 1) * HEAD_DIM] = o[r * CHUNK:(r + 1) * CHUNK].astype(BF16)


def _swa_attention(q, kv, nb, sinks, *, mask_front):
    bsz, L, hd = q.shape
    lkv, kvw = kv.shape[1], kv.shape[2]
    return pl.pallas_call(
        functools.partial(_swa_kernel, mask_front=mask_front),
        grid=(bsz, L // CHUNK),
        in_specs=[pl.BlockSpec((1, CHUNK, hd), lambda b, c: (b, c, 0)),
                  pl.BlockSpec((1, lkv, kvw), lambda b, c: (b, 0, 0)),
                  pl.BlockSpec((N_HEADS, CHUNK, WINDOW + CHUNK), lambda b, c: (0, 0, 0)),
                  pl.BlockSpec((N_HEADS, 1, 1), lambda b, c: (0, 0, 0))],
        out_specs=pl.BlockSpec((1, CHUNK, hd), lambda b, c: (b, c, 0)),
        out_shape=jax.ShapeDtypeStruct((bsz, L, hd), BF16),
        compiler_params=_params("parallel", "parallel"),
        name="swa_attention",
    )(q, kv, nb, sinks)


def _route_rows(s, sb):
    n, m = N_GROUPS, EXPERTS_PER_GROUP
    grp = []
    for g in range(n):
        x = sb[g * m:(g + 1) * m]
        best = None
        for a in range(m):
            for b in range(a + 1, m):
                pair = x[a] + x[b]
                best = pair if best is None else jnp.maximum(best, pair)
        grp.append(best)
    chosen_g = []
    taken = None
    for g in range(n):
        is_g = None
        for o in range(g + 1, n):
            c = grp[g] >= grp[o]
            is_g = c if is_g is None else (is_g & c)
        if is_g is None:
            is_g = ~taken
        elif taken is not None:
            is_g = is_g & (~taken)
        taken = is_g if taken is None else (taken | is_g)
        chosen_g.append(is_g)
    picked = []
    for e in range(N_EXPERTS):
        g, a = divmod(e, m)
        beaten = jnp.zeros(sb[e].shape, F32)
        for b in range(m):
            if b == a:
                continue
            o = g * m + b
            wins = (sb[o] > sb[e]) | ((sb[o] == sb[e]) & (b < a))
            beaten = beaten + wins.astype(F32)
        picked.append(chosen_g[g] & (beaten < 2.0))
    tops = [jnp.where(picked[e], s[e], 0.0) for e in range(N_EXPERTS)]
    denom = tops[0]
    for e in range(1, N_EXPERTS):
        denom = denom + tops[e]
    return [t / denom for t in tops]


def _post_mix_kernel(o_ref, wo_ref, x_ref, gt_ref, g_ref, sc_ref, sh_ref, wr_ref, br_ref,
                     x1_ref, h2_ref, gates_ref, *, o_transposed):
    mixed = lax.dot_general(o_ref[0], wo_ref[...], _TN if o_transposed else _NN, preferred_element_type=F32)
    x1 = x_ref[0] + gt_ref[0] * mixed
    x1_ref[0] = x1
    h2 = _modnorm(x1, g_ref[...], sc_ref[0], sh_ref[0])
    h2_ref[0] = h2.astype(BF16)
    logits = _dot3(wr_ref[...], h2, _NT)
    s = _sigmoid(logits)
    sb = s + br_ref[...]
    rows = _route_rows([s[e:e + 1] for e in range(N_EXPERTS)], [sb[e:e + 1] for e in range(N_EXPERTS)])
    gates_ref[0] = jnp.concatenate(rows, axis=0)


def _post_mix(o, wo, x, gt, g, sc, sh, wr_t, br, tm, o_transposed=False):
    bsz, L, d = x.shape
    hd = wo.shape[0]
    o_spec = (pl.BlockSpec((1, hd, tm), lambda b, i: (b, 0, i)) if o_transposed
              else pl.BlockSpec((1, tm, hd), lambda b, i: (b, i, 0)))
    row = lambda b, i: (b, i, 0)
    per_b = lambda b, i: (b, 0, 0)
    const = lambda b, i: (0, 0)
    return pl.pallas_call(
        functools.partial(_post_mix_kernel, o_transposed=o_transposed),
        grid=(bsz, L // tm),
        in_specs=[o_spec,
                  pl.BlockSpec((hd, d), const),
                  pl.BlockSpec((1, tm, d), row),
                  pl.BlockSpec((1, 1, d), per_b),
                  pl.BlockSpec((1, d), const),
                  pl.BlockSpec((1, 1, d), per_b),
                  pl.BlockSpec((1, 1, d), per_b),
                  pl.BlockSpec((N_EXPERTS, d), const),
                  pl.BlockSpec((N_EXPERTS, 1), const)],
        out_specs=[pl.BlockSpec((1, tm, d), row), pl.BlockSpec((1, tm, d), row),
                   pl.BlockSpec((1, N_EXPERTS, tm), lambda b, i: (b, 0, i))],
        out_shape=[jax.ShapeDtypeStruct((bsz, L, d), F32),
                   jax.ShapeDtypeStruct((bsz, L, d), BF16),
                   jax.ShapeDtypeStruct((bsz, N_EXPERTS, L), F32)],
        compiler_params=_params("parallel", "parallel"),
        name="post_mix_route",
    )(o, wo, x, gt, g, sc, sh, wr_t, br)


def _moe_kernel(x_ref, h_ref, gates_ref, gt_ref, wgu_ref, wd_ref, o_ref, acc_sc):
    e = pl.program_id(2)

    @pl.when(e == 0)
    def _():
        acc_sc[...] = jnp.zeros(acc_sc.shape, F32)

    gu = jnp.dot(h_ref[0], wgu_ref[0], preferred_element_type=F32)
    gate = gu[:, :D_EXPERT]
    act = (gate * _sigmoid(gate)) * gu[:, D_EXPERT:]
    y = jnp.dot(act.astype(BF16), wd_ref[0], preferred_element_type=F32)
    gates = gates_ref[0]
    lane = lax.broadcasted_iota(I32, gates.shape, 1)
    ge = jnp.sum(jnp.where(lane == e, gates, 0.0), axis=1, keepdims=True)
    acc_sc[...] += ge * y

    @pl.when(e == N_EXPERTS - 1)
    def _():
        o_ref[0] = x_ref[0] + gt_ref[0] * acc_sc[...]


def _moe(x, h, gates, gt, wgu, wd, tm):
    bsz, L, d = x.shape
    row = lambda b, i, e: (b, i, 0)
    return pl.pallas_call(
        _moe_kernel,
        grid=(bsz, L // tm, N_EXPERTS),
        in_specs=[pl.BlockSpec((1, tm, d), row),
                  pl.BlockSpec((1, tm, d), row),
                  pl.BlockSpec((1, tm, N_EXPERTS), row),
                  pl.BlockSpec((1, 1, d), lambda b, i, e: (b, 0, 0)),
                  pl.BlockSpec((1, d, 2 * D_EXPERT), lambda b, i, e: (e, 0, 0)),
                  pl.BlockSpec((1, D_EXPERT, d), lambda b, i, e: (e, 0, 0))],
        out_specs=pl.BlockSpec((1, tm, d), row),
        out_shape=jax.ShapeDtypeStruct((bsz, L, d), F32),
        scratch_shapes=[pltpu.VMEM((tm, d), F32)],
        compiler_params=_params("parallel", "parallel", "arbitrary"),
        name="moe_experts",
    )(x, h, gates, gt, wgu, wd)


def _final_norm_kernel(x_ref, g_ref, o_ref):
    o_ref[0] = _rms(x_ref[0]) * g_ref[...]


def _final_norm(x, g, tm):
    bsz, L, d = x.shape
    row = lambda b, i: (b, i, 0)
    return pl.pallas_call(
        _final_norm_kernel,
        grid=(bsz, L // tm),
        in_specs=[pl.BlockSpec((1, tm, d), row), pl.BlockSpec((1, d), lambda b, i: (0, 0))],
        out_specs=pl.BlockSpec((1, tm, d), row),
        out_shape=jax.ShapeDtypeStruct((bsz, L, d), F32),
        compiler_params=_params("parallel", "parallel"),
        name="final_norm",
    )(x, g)


def _rot_cols(w):
    half = w.shape[-1] // 2
    return jnp.concatenate([-w[..., half:], w[..., :half]], axis=-1)


def _head_pad(w2d, width):
    w3 = w2d.reshape(w2d.shape[0], N_HEADS, HEAD_DIM)
    return jnp.pad(w3, ((0, 0), (0, 0), (0, width - HEAD_DIM))).reshape(w2d.shape[0], N_HEADS * width)


def _prep_weights(a_w_in, a_g_q, a_w_uq, a_g_kv, a_w_ukv, a_w_o, b_w_in, b_b_f, b_w_o,
                  c_w_in, c_w_o, d_w_in, d_sinks, d_w_o, moe_w_router, moe_b_router,
                  moe_w_gate, moe_w_up, moe_w_down):
    w = {}
    hd = N_HEADS * HEAD_DIM
    kr = a_w_in[:, A_Q_LORA + A_KV_LORA:]
    w["a_in"] = jnp.concatenate([a_w_in, _rot_cols(kr)], axis=1).astype(BF16)
    w["a_gq"] = a_g_q.reshape(1, -1)
    w["a_gkv"] = a_g_kv.reshape(1, -1)
    uq = a_w_uq.reshape(A_Q_LORA, N_HEADS, A_NOPE + A_ROPE)
    pad = LANES - A_NOPE - A_ROPE
    zq = lambda n: jnp.zeros((A_Q_LORA, N_HEADS, n), F32)
    w["a_wq"] = jnp.concatenate([uq, zq(pad)], axis=-1).reshape(A_Q_LORA, N_HEADS * LANES).astype(BF16)
    w["a_wqs"] = jnp.concatenate([zq(A_NOPE), _rot_cols(uq[..., A_NOPE:]), zq(pad)],
                                 axis=-1).reshape(A_Q_LORA, N_HEADS * LANES).astype(BF16)
    ukv = a_w_ukv.reshape(A_KV_LORA, N_HEADS, A_NOPE + A_V)
    w["a_wk"] = jnp.concatenate([ukv[..., :A_NOPE], jnp.zeros((A_KV_LORA, N_HEADS, LANES - A_NOPE), F32)],
                                axis=-1).reshape(A_KV_LORA, N_HEADS * LANES).astype(BF16)
    w["a_wv"] = ukv[..., A_NOPE:].reshape(A_KV_LORA, N_HEADS * A_V).astype(BF16)
    w["a_wvt"] = _head_pad(ukv[..., A_NOPE:].reshape(A_KV_LORA, N_HEADS * A_V), V_ROWS).T.astype(BF16)
    sel = np.zeros((A_ROPE, N_HEADS, LANES), np.float32)
    for r in range(A_ROPE):
        sel[r, :, A_NOPE + r] = 1.0
    w["a_sel"] = jnp.asarray(sel.reshape(A_ROPE, N_HEADS * LANES), BF16)
    w["a_wo"] = a_w_o.astype(BF16)
    ones_rows = np.zeros((N_HEADS, V_ROWS, 1), np.float32)
    ones_rows[:, HEAD_DIM, 0] = 1.0
    w["ones_rows"] = jnp.asarray(ones_rows.reshape(N_HEADS * V_ROWS, 1))
    w["b_wq"] = b_w_in[:, :hd].astype(BF16)
    w["b_wk"] = b_w_in[:, hd:2 * hd].astype(BF16)
    w["b_wv"] = b_w_in[:, 2 * hd:3 * hd].astype(BF16)
    w["b_wf"] = jnp.pad(b_w_in[:, 3 * hd:], ((0, 0), (0, LANES - N_HEADS))).astype(BF16)
    w["b_bf"] = b_b_f.reshape(1, N_HEADS)
    w["b_wo"] = b_w_o.astype(BF16)
    w["b_wqt"] = _head_pad(b_w_in[:, :hd], LANES).T.astype(BF16)
    w["b_wvt"] = _head_pad(b_w_in[:, 2 * hd:3 * hd], V_ROWS).T.astype(BF16)
    hi = C_IDX_HEADS * C_IDX_DIM
    w["c_wqt"] = _head_pad(c_w_in[:, :hd], LANES).T.astype(BF16)
    kv = c_w_in[:, hd:hd + 2 * HEAD_DIM]
    qi = c_w_in[:, hd + 2 * HEAD_DIM:hd + 2 * HEAD_DIM + hi]
    rest = c_w_in[:, hd + 2 * HEAD_DIM + hi:]
    w["c_wqit"] = qi.T.astype(BF16)
    sm = jnp.concatenate([kv, rest], axis=1)
    w["c_wsm"] = jnp.pad(sm, ((0, 0), (0, 2 * LANES - sm.shape[1]))).astype(BF16)
    w["c_wsmt"] = w["c_wsm"].T
    w["c_wo"] = c_w_o.astype(BF16)
    w["d_wq"] = d_w_in[:, :hd].astype(BF16)
    w["d_wkv"] = d_w_in[:, hd:].astype(BF16)
    w["d_sinks"] = d_sinks.astype(F32).reshape(N_HEADS, 1, 1)
    w["d_wo"] = d_w_o.astype(BF16)
    w["wr_t"] = moe_w_router.T
    w["br"] = moe_b_router.reshape(N_EXPERTS, 1)
    w["wgu"] = jnp.concatenate([moe_w_gate, moe_w_up], axis=-1).astype(BF16)
    w["wd"] = moe_w_down.astype(BF16)
    return w


def _rope_tables(pos):
    half = A_ROPE // 2
    inv = ROPE_THETA ** (-jnp.arange(half, dtype=F32) / half)
    ang = pos.astype(F32)[:, None] * inv[None, :]
    cos, sin = jnp.cos(ang), jnp.sin(ang)
    n = pos.shape[0]
    pad = LANES - A_NOPE - A_ROPE
    cosp = jnp.concatenate([jnp.ones((n, A_NOPE), F32), cos, cos, jnp.zeros((n, pad), F32)], axis=1)
    sinp = jnp.concatenate([jnp.zeros((n, A_NOPE), F32), sin, sin, jnp.zeros((n, pad), F32)], axis=1)
    return cosp, sinp


def _pad_rows(a, n):
    return jnp.pad(a, ((0, 0), (0, n - a.shape[1])) + ((0, 0),) * (a.ndim - 2))


def _round_up(n, m):
    return (n + m - 1) // m * m


def kernel(x_prompt, x_sample, c_prompt, c_sample, cache_a_latent, cache_a_krope, cache_b_k, cache_b_v, cache_b_logf, cache_c_k, cache_c_v, cache_c_kidx, cache_d_k, cache_d_v, w_ada, b_ada, g_mix, g_ffn, g_final, rel_bias, a_w_in, a_g_q, a_w_uq, a_g_kv, a_w_ukv, a_w_o, b_w_in, b_b_f, b_w_o, c_w_in, c_w_o, d_w_in, d_sinks, d_w_o, moe_w_router, moe_b_router, moe_w_gate, moe_w_up, moe_w_down):
    bp, S, d = x_prompt.shape
    bs, Ls, _ = x_sample.shape
    P = cache_a_latent.shape[1]
    depth = w_ada.shape[0]
    hd = N_HEADS * HEAD_DIM
    assert S % PROJ_ROWS == 0 and Ls == CHUNK and P % DSA_TQ == 0
    assert DSA_SEARCH_ROWS % DSA_TK == 0 and S % DSA_SEARCH_ROWS == 0
    assert S % FLASH_T_TILE == 0 and FLASH_T_TILE % CHUNK == 0 and S % FOX_PROJ_ROWS == 0
    tm_p, tm_s = PROJ_ROWS, Ls
    lk_s = _round_up(P + Ls, DSA_SEARCH_ROWS)

    w = _prep_weights(a_w_in, a_g_q, a_w_uq, a_g_kv, a_w_ukv, a_w_o, b_w_in, b_b_f, b_w_o,
                      c_w_in, c_w_o, d_w_in, d_sinks, d_w_o, moe_w_router, moe_b_router,
                      moe_w_gate, moe_w_up, moe_w_down)
    mod = _adaln(jnp.concatenate([c_prompt, c_sample], axis=0), w_ada, b_ada)
    farq, nb_dsa, nb_swa = _bias_tiles(rel_bias)

    def mods(i, lo, hi):
        return [mod[i, lo:hi, k * d:(k + 1) * d][:, None, :] for k in range(6)]

    xp, xs = x_prompt, x_sample
    outs = {}
    for i in range(depth):
        sh1_p, sc1_p, gt1_p, sh2_p, sc2_p, gt2_p = mods(i, 0, bp)
        sh1_s, sc1_s, gt1_s, sh2_s, sc2_s, gt2_s = mods(i, bp, bp + bs)
        g1 = g_mix[i].reshape(1, d)
        mixer = i % 4
        if mixer == 0:
            cos_p, sin_p = _rope_tables(jnp.arange(S))
            cos_s, sin_s = _rope_tables(P + jnp.arange(Ls))
            qt_p, lat_p, kr_p = _mla_proj_t(xp, g1, sc1_p, sh1_p, w, cos_p, sin_p, tm_p)
            k_p, vt_p = _mla_expand_t(lat_p, kr_p, w, tm_p)
            op = _flash_t(qt_p, k_p, vt_p, per_frame=False)
            q_s, lat_s, kr_s = _mla_proj(xs, g1, sc1_s, sh1_s, w, cos_s, sin_s, tm_s)
            lat_all = _pad_rows(jnp.concatenate([cache_a_latent, lat_s], axis=1), lk_s)
            kr_all = _pad_rows(jnp.concatenate([cache_a_krope, kr_s], axis=1), lk_s)
            k_s, v_s = _mla_expand(lat_all, kr_all, w, DSA_TK)
            os_ = _flash(q_s, k_s, v_s, tq=Ls, tk=DSA_TK, qpos0=P, per_frame=False, wide=True)
            outs["a"] = (lat_p, kr_p, lat_s, kr_s)
            wo = w["a_wo"]
        elif mixer == 1:
            qt_p, k_p, v_p, vt_p, lf_p, lft_p = _fox_proj_t(xp, g1, sc1_p, sh1_p, w, FOX_PROJ_ROWS)
            cumt_p, kaug_p = _fox_cumaug(lft_p, lf_p, k_p, CUMSUM_TILE)
            op = _flash_t(qt_p, kaug_p, vt_p, per_frame=True, qd=cumt_p)
            q_s, k_s, v_s, kb_s, vb_s, lf_s = _fox_proj(xs, g1, sc1_s, sh1_s, w, tm_s)
            lf_all = jnp.concatenate([cache_b_logf.astype(F32), lf_s], axis=1)
            cum_s = _cumsum_rows(_pad_rows(lf_all, lk_s).swapaxes(1, 2), CUMSUM_TILE)
            kb_all = _pad_rows(jnp.concatenate([cache_b_k.reshape(bs, P, hd).astype(BF16), kb_s], axis=1), lk_s)
            vb_all = _pad_rows(jnp.concatenate([cache_b_v.reshape(bs, P, hd).astype(BF16), vb_s], axis=1), lk_s)
            os_ = _flash(q_s, kb_all, vb_all, tq=Ls, tk=DSA_TK, qpos0=P, per_frame=True, wide=False,
                         qd=cum_s[:, :, P:P + Ls, None], kd=cum_s[:, :, None, :])
            shp = lambda a: a.reshape(a.shape[0], a.shape[1], N_HEADS, HEAD_DIM)
            outs["b"] = (shp(k_p), shp(v_p), lf_p, shp(k_s), shp(v_s), lf_s)
            wo = w["b_wo"]
        elif mixer == 2:
            qt_p, qit_p, k_p, v_p, ki_p, wit_p, kaug_p, vt_p, kib_p = _dsa_proj(xp, g1, sc1_p, sh1_p, w, tm_p)
            op = _dsa_attention(qt_p, qit_p, wit_p, kaug_p, vt_p, kib_p, farq, nb_dsa, qpos0=0,
                                n_sel=min(C_TOPK_MAX, S // 4))
            qt_s, qit_s, k_s, v_s, ki_s, wit_s, kaug_s, vt_s, kib_s = _dsa_proj(xs, g1, sc1_s, sh1_s, w, tm_s)
            pad_q = lambda a: jnp.pad(a, ((0, 0), (0, 0), (0, DSA_TQ - Ls)))
            ones_cols = jnp.zeros((bs, P, LANES - HEAD_DIM), BF16).at[:, :, :2].set(1.0)
            kaug_c = jnp.concatenate([cache_c_k.astype(BF16), ones_cols], axis=2)
            kaug_all = _pad_rows(jnp.concatenate([kaug_c, kaug_s], axis=1), lk_s)
            ones_row = jnp.zeros((bs, V_ROWS - HEAD_DIM, P), BF16).at[:, 0, :].set(1.0)
            vt_c = jnp.concatenate([jnp.swapaxes(cache_c_v, 1, 2).astype(BF16), ones_row], axis=1)
            vt_all = jnp.pad(jnp.concatenate([vt_c, vt_s], axis=2), ((0, 0), (0, 0), (0, lk_s - P - Ls)))
            ki_all = _pad_rows(jnp.concatenate([cache_c_kidx.astype(BF16), kib_s], axis=1), lk_s)
            os_ = _dsa_attention(pad_q(qt_s), pad_q(qit_s), pad_q(wit_s), kaug_all, vt_all, ki_all,
                                 farq, nb_dsa, qpos0=P, n_sel=min(C_TOPK_MAX, (P + Ls) // 4))[:, :Ls]
            outs["c"] = (k_p, v_p, ki_p, k_s, v_s, ki_s)
            wo = w["c_wo"]
        else:
            kvw = D_KV_HEADS * HEAD_DIM
            q_p, kv_p, kvb_p = _swa_proj(xp, g1, sc1_p, sh1_p, w, tm_p)
            op = _swa_attention(q_p, jnp.pad(kvb_p, ((0, 0), (WINDOW, 0), (0, 0))), nb_swa, w["d_sinks"],
                                mask_front=True)
            q_s, kv_s, kvb_s = _swa_proj(xs, g1, sc1_s, sh1_s, w, tm_s)
            wc = cache_d_k.shape[1]
            ck = cache_d_k.reshape(bs, wc, kvw)
            cv = cache_d_v.reshape(bs, wc, kvw)
            kv_cache = jnp.concatenate([ck, cv], axis=-1)
            kv_all = jnp.concatenate([kv_cache.astype(BF16), kvb_s], axis=1)
            kv_all = jnp.pad(kv_all, ((0, 0), (WINDOW - wc, 0), (0, 0)))
            os_ = _swa_attention(q_s, kv_all, nb_swa, w["d_sinks"], mask_front=(wc < WINDOW))
            keep = min(WINDOW, S)
            shp = lambda a: a.reshape(a.shape[0], a.shape[1], D_KV_HEADS, HEAD_DIM)
            k_roll = jnp.concatenate([ck, kv_s[..., :kvw]], axis=1)[:, Ls:]
            v_roll = jnp.concatenate([cv, kv_s[..., kvw:]], axis=1)[:, Ls:]
            outs["d"] = (shp(kv_p[:, S - keep:, :kvw]), shp(kv_p[:, S - keep:, kvw:]), shp(k_roll), shp(v_roll))
            wo = w["d_wo"]

        g2 = g_ffn[i].reshape(1, d)
        x1_p, h2_p, gates_p = _post_mix(op, wo, xp, gt1_p, g2, sc2_p, sh2_p, w["wr_t"], w["br"], tm_p,
                                        o_transposed=(mixer < 2))
        xp = _moe(x1_p, h2_p, jnp.swapaxes(gates_p, 1, 2), gt2_p, w["wgu"][i], w["wd"][i], tm_p)
        x1_s, h2_s, gates_s = _post_mix(os_, wo, xs, gt1_s, g2, sc2_s, sh2_s, w["wr_t"], w["br"], tm_s)
        xs = _moe(x1_s, h2_s, jnp.swapaxes(gates_s, 1, 2), gt2_s, w["wgu"][i], w["wd"][i], tm_s)

    gf = g_final.reshape(1, d)
    y_p = _final_norm(xp, gf, tm_p)
    y_s = _final_norm(xs, gf, tm_s)
    a_lat_p, a_kr_p, a_lat_s, a_kr_s = outs["a"]
    b_k_p, b_v_p, b_lf_p, b_k_s, b_v_s, b_lf_s = outs["b"]
    c_k_p, c_v_p, c_ki_p, c_k_s, c_v_s, c_ki_s = outs["c"]
    d_k_p, d_v_p, d_k_s, d_v_s = outs["d"]
    return (y_p, y_s,
            a_lat_p, a_kr_p, b_k_p, b_v_p, b_lf_p, c_k_p, c_v_p, c_ki_p, d_k_p, d_v_p,
            a_lat_s, a_kr_s, b_k_s, b_v_s, b_lf_s, c_k_s, c_v_s, c_ki_s, d_k_s, d_v_s)
```

```python
import functools
import math

import jax
import jax.numpy as jnp
import numpy as np
from jax import lax
from jax.experimental import pallas as pl
from jax.experimental.pallas import tpu as pltpu

F32 = jnp.float32
BF16 = jnp.bfloat16
I32 = jnp.int32

CHUNK = 64
CHUNK_SHIFT = CHUNK.bit_length() - 1
NORM_EPS = 1e-6
NEG_INF = -1e30
LOG2E = math.log2(math.e)
N_HEADS = 16
HEAD_DIM = 64
HEAD_SCALE = HEAD_DIM ** -0.5
N_BUCKETS = 32
MAX_DISTANCE = 128
A_Q_LORA = 512
A_KV_LORA = 256
A_NOPE = 64
A_ROPE = 32
A_V = 64
A_SCALE = (A_NOPE + A_ROPE) ** -0.5
ROPE_THETA = 10000.0
C_IDX_HEADS = 8
C_IDX_DIM = 64
C_TOPK_MAX = 256
D_KV_HEADS = 2
D_REP = N_HEADS // D_KV_HEADS
WINDOW = 128
N_WIN_CHUNKS = WINDOW // CHUNK
N_EXPERTS = 16
N_GROUPS = 4
EXPERTS_PER_GROUP = N_EXPERTS // N_GROUPS
D_EXPERT = 256

LANES = 128
VMEM_LIMIT_BYTES = 56 * 1024 * 1024

PROJ_ROWS = 512
FLASH_TQ = 256
FLASH_TK = 512
DSA_TQ = 128
DSA_TK = 256
DSA_SEARCH_ROWS = 512
V_ROWS = HEAD_DIM + 16
CUMSUM_TILE = 256
FLASH_T_TILE = 512
FOX_PROJ_ROWS = 256
SAMPLE_TK = 512

_NT = (((1,), (1,)), ((), ()))
_NN = (((1,), (0,)), ((), ()))
_TN = (((0,), (0,)), ((), ()))


def _params(*sem):
    return pltpu.CompilerParams(dimension_semantics=sem, vmem_limit_bytes=VMEM_LIMIT_BYTES)


def _split2(a):
    hi = a.astype(BF16)
    lo = (a - hi.astype(F32)).astype(BF16)
    return hi, lo


def _split3(c):
    hi = c.astype(BF16).astype(F32)
    r1 = c - hi
    mid = r1.astype(BF16).astype(F32)
    lo = (r1 - mid).astype(BF16).astype(F32)
    return hi, mid, lo


def _dot3(a, b, dims):
    ah, al = _split2(a)
    bh, bl = _split2(b)
    d = lambda x, y: lax.dot_general(x, y, dims, preferred_element_type=F32)
    return d(ah, bh) + (d(ah, bl) + d(al, bh))


def _rms(x):
    return x * lax.rsqrt(jnp.mean(x * x, axis=-1, keepdims=True) + NORM_EPS)


def _modnorm(x, g, sc, sh):
    return _rms(x) * g * (1.0 + sc) + sh


def _sigmoid(z):
    return 1.0 / (1.0 + jnp.exp(-z))


def _log_sigmoid(z):
    return jnp.minimum(z, 0.0) - jnp.log1p(jnp.exp(-jnp.abs(z)))


def _adaln_kernel(c_ref, w_ref, b_ref, o_ref):
    c = c_ref[...]
    s = c * _sigmoid(c)
    o_ref[0] = _dot3(s, w_ref[0], _NN) + b_ref[0]


def _adaln(c_all, w_ada, b_ada):
    depth, d, d6 = w_ada.shape
    bc = c_all.shape[0]
    return pl.pallas_call(
        _adaln_kernel,
        grid=(depth, d6 // d),
        in_specs=[pl.BlockSpec((bc, d), lambda i, j: (0, 0)),
                  pl.BlockSpec((1, d, d), lambda i, j: (i, 0, j)),
                  pl.BlockSpec((1, 1, d), lambda i, j: (i, 0, j))],
        out_specs=pl.BlockSpec((1, bc, d), lambda i, j: (i, 0, j)),
        out_shape=jax.ShapeDtypeStruct((depth, bc, d6), F32),
        compiler_params=_params("parallel", "parallel"),
        name="adaln",
    )(c_all, w_ada, b_ada.reshape(depth, 1, d6))


_FAR_THRESHOLDS = (12, 16, 23, 32, 46, 64, 91)
_FAR_BUCKET = N_BUCKETS // 2 - 1
FAR_HI_ROW = HEAD_DIM
FAR_LO_ROW = HEAD_DIM + 1


def _rel_bucket(rel):
    n = jnp.abs(rel)
    nb = N_BUCKETS // 2
    max_exact = nb // 2
    far = jnp.full(rel.shape, max_exact, I32)
    for t in _FAR_THRESHOLDS:
        far = far + (n >= t).astype(I32)
    return jnp.where(rel > 0, nb, 0) + jnp.where(n < max_exact, n, far)


def _bias_kernel(tab_ref, farq_ref, dsa_ref, swa_ref):
    def lookup(bucket, h):
        acc = jnp.zeros(bucket.shape, F32)
        for b in range(N_BUCKETS):
            acc = jnp.where(bucket == b, tab_ref[b, h], acc)
        return acc

    kr = lax.broadcasted_iota(I32, (DSA_TQ, DSA_TQ), 0)
    qc = lax.broadcasted_iota(I32, (DSA_TQ, DSA_TQ), 1)
    for t, off in enumerate((-2 * DSA_TQ, -DSA_TQ, 0)):
        bucket = _rel_bucket(kr - qc + off)
        for h in range(N_HEADS):
            far = tab_ref[_FAR_BUCKET, h]
            dsa_ref[t, :, h * DSA_TQ:(h + 1) * DSA_TQ] = (lookup(bucket, h) - far) * LOG2E
    row = lax.broadcasted_iota(I32, (LANES, DSA_TQ), 0)
    for h in range(N_HEADS):
        c = jnp.full((LANES, DSA_TQ), tab_ref[_FAR_BUCKET, h] * LOG2E, F32)
        hi = c.astype(BF16).astype(F32)
        lo = (c - hi).astype(BF16).astype(F32)
        blk = jnp.where(row == FAR_HI_ROW, hi, jnp.where(row == FAR_LO_ROW, lo, 0.0))
        farq_ref[:, h * DSA_TQ:(h + 1) * DSA_TQ] = blk.astype(BF16)
    qr = lax.broadcasted_iota(I32, (CHUNK, WINDOW + CHUNK), 0)
    kc = lax.broadcasted_iota(I32, (CHUNK, WINDOW + CHUNK), 1)
    bucket = _rel_bucket(kc - qr - WINDOW)
    for h in range(N_HEADS):
        swa_ref[h] = lookup(bucket, h)


def _bias_tiles(rel_bias):
    vm = pl.BlockSpec(memory_space=pltpu.VMEM)
    return pl.pallas_call(
        _bias_kernel,
        in_specs=[pl.BlockSpec(memory_space=pltpu.SMEM)],
        out_specs=[vm, vm, vm],
        out_shape=[jax.ShapeDtypeStruct((LANES, N_HEADS * DSA_TQ), BF16),
                   jax.ShapeDtypeStruct((3, DSA_TQ, N_HEADS * DSA_TQ), F32),
                   jax.ShapeDtypeStruct((N_HEADS, CHUNK, WINDOW + CHUNK), F32)],
        compiler_params=pltpu.CompilerParams(vmem_limit_bytes=VMEM_LIMIT_BYTES),
        name="bias_tiles",
    )(rel_bias)


def _mla_in(x_ref, g_ref, sc_ref, sh_ref, win_ref, gq_ref, gkv_ref, cos_ref, sin_ref, lat_ref, kr_ref):
    h = _modnorm(x_ref[0], g_ref[...], sc_ref[0], sh_ref[0]).astype(BF16)
    hw = jnp.dot(h, win_ref[...], preferred_element_type=F32)
    o1 = A_Q_LORA
    o2 = o1 + A_KV_LORA
    o3 = o2 + A_ROPE
    lat_ref[0] = _rms(hw[:, o1:o2]) * gkv_ref[...]
    cosr = cos_ref[...][:, A_NOPE:A_NOPE + A_ROPE]
    sinr = sin_ref[...][:, A_NOPE:A_NOPE + A_ROPE]
    kr_ref[0] = hw[:, o2:o3] * cosr + hw[:, o3:o3 + A_ROPE] * sinr
    return (_rms(hw[:, :o1]) * gq_ref[...]).astype(BF16)


def _mla_proj_kernel(x_ref, g_ref, sc_ref, sh_ref, win_ref, gq_ref, gkv_ref, wq_ref, wqs_ref,
                     cos_ref, sin_ref, q_ref, lat_ref, kr_ref):
    cqn = _mla_in(x_ref, g_ref, sc_ref, sh_ref, win_ref, gq_ref, gkv_ref, cos_ref, sin_ref, lat_ref, kr_ref)
    cosp = cos_ref[...]
    sinp = sin_ref[...]
    a = jnp.dot(cqn, wq_ref[...], preferred_element_type=F32)
    b = jnp.dot(cqn, wqs_ref[...], preferred_element_type=F32)
    for hd in range(N_HEADS):
        sl = slice(hd * LANES, (hd + 1) * LANES)
        q_ref[0, :, sl] = ((a[:, sl] * cosp + b[:, sl] * sinp) * A_SCALE).astype(BF16)


def _mla_proj(x, g, sc, sh, w, cosp, sinp, tm):
    bsz, L, d = x.shape
    row = lambda b, i: (b, i, 0)
    per_b = lambda b, i: (b, 0, 0)
    const = lambda b, i: (0, 0)
    nin = w["a_in"].shape[1]
    return pl.pallas_call(
        _mla_proj_kernel,
        grid=(bsz, L // tm),
        in_specs=[pl.BlockSpec((1, tm, d), row),
                  pl.BlockSpec((1, d), const),
                  pl.BlockSpec((1, 1, d), per_b),
                  pl.BlockSpec((1, 1, d), per_b),
                  pl.BlockSpec((d, nin), const),
                  pl.BlockSpec((1, A_Q_LORA), const),
                  pl.BlockSpec((1, A_KV_LORA), const),
                  pl.BlockSpec((A_Q_LORA, N_HEADS * LANES), const),
                  pl.BlockSpec((A_Q_LORA, N_HEADS * LANES), const),
                  pl.BlockSpec((tm, LANES), lambda b, i: (i, 0)),
                  pl.BlockSpec((tm, LANES), lambda b, i: (i, 0))],
        out_specs=[pl.BlockSpec((1, tm, N_HEADS * LANES), row),
                   pl.BlockSpec((1, tm, A_KV_LORA), row),
                   pl.BlockSpec((1, tm, A_ROPE), row)],
        out_shape=[jax.ShapeDtypeStruct((bsz, L, N_HEADS * LANES), BF16),
                   jax.ShapeDtypeStruct((bsz, L, A_KV_LORA), F32),
                   jax.ShapeDtypeStruct((bsz, L, A_ROPE), F32)],
        compiler_params=_params("parallel", "parallel"),
        name="mla_proj",
    )(x, g, sc, sh, w["a_in"], w["a_gq"], w["a_gkv"], w["a_wq"], w["a_wqs"], cosp, sinp)


def _mla_expand_kernel(lat_ref, kr_ref, wk_ref, sel_ref, wv_ref, k_ref, v_ref):
    lat = lat_ref[0].astype(BF16)
    kr = kr_ref[0].astype(BF16)
    k = (jnp.dot(lat, wk_ref[...], preferred_element_type=F32)
         + jnp.dot(kr, sel_ref[...], preferred_element_type=F32))
    k_ref[0] = k.astype(BF16)
    v_ref[0] = jnp.dot(lat, wv_ref[...], preferred_element_type=F32).astype(BF16)


def _mla_expand(lat, kr, w, tm):
    bsz, L, _ = lat.shape
    row = lambda b, i: (b, i, 0)
    const = lambda b, i: (0, 0)
    return pl.pallas_call(
        _mla_expand_kernel,
        grid=(bsz, L // tm),
        in_specs=[pl.BlockSpec((1, tm, A_KV_LORA), row),
                  pl.BlockSpec((1, tm, A_ROPE), row),
                  pl.BlockSpec((A_KV_LORA, N_HEADS * LANES), const),
                  pl.BlockSpec((A_ROPE, N_HEADS * LANES), const),
                  pl.BlockSpec((A_KV_LORA, N_HEADS * A_V), const)],
        out_specs=[pl.BlockSpec((1, tm, N_HEADS * LANES), row),
                   pl.BlockSpec((1, tm, N_HEADS * A_V), row)],
        out_shape=[jax.ShapeDtypeStruct((bsz, L, N_HEADS * LANES), BF16),
                   jax.ShapeDtypeStruct((bsz, L, N_HEADS * A_V), BF16)],
        compiler_params=_params("parallel", "parallel"),
        name="mla_expand",
    )(lat, kr, w["a_wk"], w["a_sel"], w["a_wv"])


def _mla_proj_t_kernel(x_ref, g_ref, sc_ref, sh_ref, win_ref, gq_ref, gkv_ref, wqt_ref, wqst_ref,
                       cos_ref, sin_ref, cost_ref, sint_ref, qt_ref, lat_ref, kr_ref):
    cqn = _mla_in(x_ref, g_ref, sc_ref, sh_ref, win_ref, gq_ref, gkv_ref, cos_ref, sin_ref, lat_ref, kr_ref)
    a = lax.dot_general(wqt_ref[...], cqn, _NT, preferred_element_type=F32)
    b = lax.dot_general(wqst_ref[...], cqn, _NT, preferred_element_type=F32)
    cost = cost_ref[...]
    sint = sint_ref[...]
    for hd in range(N_HEADS):
        sl = slice(hd * LANES, (hd + 1) * LANES)
        qt_ref[0, sl, :] = ((a[sl] * cost + b[sl] * sint) * (A_SCALE * LOG2E)).astype(BF16)


def _mla_proj_t(x, g, sc, sh, w, cosp, sinp, tm):
    bsz, L, d = x.shape
    row = lambda b, i: (b, i, 0)
    per_b = lambda b, i: (b, 0, 0)
    const = lambda b, i: (0, 0)
    nin = w["a_in"].shape[1]
    hq = N_HEADS * LANES
    return pl.pallas_call(
        _mla_proj_t_kernel,
        grid=(bsz, L // tm),
        in_specs=[pl.BlockSpec((1, tm, d), row),
                  pl.BlockSpec((1, d), const),
                  pl.BlockSpec((1, 1, d), per_b),
                  pl.BlockSpec((1, 1, d), per_b),
                  pl.BlockSpec((d, nin), const),
                  pl.BlockSpec((1, A_Q_LORA), const),
                  pl.BlockSpec((1, A_KV_LORA), const),
                  pl.BlockSpec((hq, A_Q_LORA), const),
                  pl.BlockSpec((hq, A_Q_LORA), const),
                  pl.BlockSpec((tm, LANES), lambda b, i: (i, 0)),
                  pl.BlockSpec((tm, LANES), lambda b, i: (i, 0)),
                  pl.BlockSpec((LANES, tm), lambda b, i: (0, i)),
                  pl.BlockSpec((LANES, tm), lambda b, i: (0, i))],
        out_specs=[pl.BlockSpec((1, hq, tm), lambda b, i: (b, 0, i)),
                   pl.BlockSpec((1, tm, A_KV_LORA), row),
                   pl.BlockSpec((1, tm, A_ROPE), row)],
        out_shape=[jax.ShapeDtypeStruct((bsz, hq, L), BF16),
                   jax.ShapeDtypeStruct((bsz, L, A_KV_LORA), F32),
                   jax.ShapeDtypeStruct((bsz, L, A_ROPE), F32)],
        compiler_params=_params("parallel", "parallel"),
        name="mla_proj_t",
    )(x, g, sc, sh, w["a_in"], w["a_gq"], w["a_gkv"], w["a_wq"].T, w["a_wqs"].T, cosp, sinp, cosp.T, sinp.T)


def _mla_expand_t_kernel(lat_ref, kr_ref, wk_ref, sel_ref, wvt_ref, ones_ref, k_ref, vt_ref):
    lat = lat_ref[0].astype(BF16)
    kr = kr_ref[0].astype(BF16)
    k = (jnp.dot(lat, wk_ref[...], preferred_element_type=F32)
         + jnp.dot(kr, sel_ref[...], preferred_element_type=F32))
    k_ref[0] = k.astype(BF16)
    vt = lax.dot_general(wvt_ref[...], lat, _NT, preferred_element_type=F32) + ones_ref[...]
    vt_ref[0] = vt.astype(BF16)


def _mla_expand_t(lat, kr, w, tm):
    bsz, L, _ = lat.shape
    row = lambda b, i: (b, i, 0)
    const = lambda b, i: (0, 0)
    hv = N_HEADS * V_ROWS
    return pl.pallas_call(
        _mla_expand_t_kernel,
        grid=(bsz, L // tm),
        in_specs=[pl.BlockSpec((1, tm, A_KV_LORA), row),
                  pl.BlockSpec((1, tm, A_ROPE), row),
                  pl.BlockSpec((A_KV_LORA, N_HEADS * LANES), const),
                  pl.BlockSpec((A_ROPE, N_HEADS * LANES), const),
                  pl.BlockSpec((hv, A_KV_LORA), const),
                  pl.BlockSpec((hv, 1), const)],
        out_specs=[pl.BlockSpec((1, tm, N_HEADS * LANES), row),
                   pl.BlockSpec((1, hv, tm), lambda b, i: (b, 0, i))],
        out_shape=[jax.ShapeDtypeStruct((bsz, L, N_HEADS * LANES), BF16),
                   jax.ShapeDtypeStruct((bsz, hv, L), BF16)],
        compiler_params=_params("parallel", "parallel"),
        name="mla_expand_t",
    )(lat, kr, w["a_wk"], w["a_sel"], w["a_wvt"], w["ones_rows"])


def _flash_kernel(*refs, tq, tk, qpos0, per_frame, wide, decay, nk_tiles):
    if decay:
        q_ref, k_ref, v_ref, qd_ref, kd_ref, o_ref, m_sc, l_sc, acc_sc = refs
    else:
        q_ref, k_ref, v_ref, o_ref, m_sc, l_sc, acc_sc = refs
    i = pl.program_id(2)
    q0 = qpos0 + i * tq
    if per_frame:
        vis_all = q0 + 1
        vis_any = q0 + tq
    else:
        vis_all = ((q0 >> CHUNK_SHIFT) + 1) * CHUNK
        vis_any = (((q0 + tq - 1) >> CHUNK_SHIFT) + 1) * CHUNK
    n_full = jnp.minimum(vis_all // tk, nk_tiles)
    n_tot = jnp.minimum((vis_any + tk - 1) // tk, nk_tiles)

    q = q_ref[0]
    lane = lax.broadcasted_iota(I32, (1, LANES), 1)
    first = lane < HEAD_DIM
    if wide:
        qs = (q[:, :LANES], q[:, LANES:])
    else:
        zero = jnp.zeros_like(q)
        qs = (jnp.where(first, q, zero), jnp.where(first, zero, q))

    m_sc[...] = jnp.full(m_sc.shape, NEG_INF, F32)
    l_sc[...] = jnp.zeros(l_sc.shape, F32)
    acc_sc[...] = jnp.zeros(acc_sc.shape, F32)
    qp = q0 + lax.broadcasted_iota(I32, (tq, 1), 0)

    def tile(j, masked):
        ks = pl.multiple_of(j * tk, tk)
        kt = k_ref[0, pl.ds(ks, tk), :]
        vt = v_ref[0, pl.ds(ks, tk), :]
        if masked:
            kp = ks + lax.broadcasted_iota(I32, (1, tk), 1)
            if per_frame:
                mask = kp <= qp
            else:
                mask = (kp >> CHUNK_SHIFT) <= (qp >> CHUNK_SHIFT)
        for hh in range(2):
            kk = kt[:, hh * LANES:(hh + 1) * LANES] if wide else kt
            s = lax.dot_general(qs[hh], kk, _NT, preferred_element_type=F32)
            if decay:
                s = s + (qd_ref[0, hh] - kd_ref[0, hh, :, pl.ds(ks, tk)])
            if masked:
                s = jnp.where(mask, s, NEG_INF)
            m_old = m_sc[hh]
            m_new = jnp.maximum(m_old, jnp.max(s, axis=1, keepdims=True))
            p = jnp.exp(s - m_new)
            alpha = jnp.exp(m_old - m_new)
            l_sc[hh] = alpha * l_sc[hh] + jnp.sum(p, axis=1, keepdims=True)
            acc_sc[hh] = alpha * acc_sc[hh] + jnp.dot(p.astype(BF16), vt, preferred_element_type=F32)
            m_sc[hh] = m_new

    def full_body(j, c):
        tile(j, False)
        return c

    def diag_body(j, c):
        tile(j, True)
        return c

    lax.fori_loop(0, n_full, full_body, 0)
    lax.fori_loop(n_full, n_tot, diag_body, 0)
    o0 = acc_sc[0] / l_sc[0]
    o1 = acc_sc[1] / l_sc[1]
    o_ref[0] = jnp.where(first, o0, o1).astype(BF16)


def _flash(q, k, v, *, tq, tk, qpos0, per_frame, wide, qd=None, kd=None):
    bsz, lq, _ = q.shape
    lk = k.shape[1]
    qw = 2 * LANES if wide else LANES
    decay = qd is not None
    in_specs = [pl.BlockSpec((1, tq, qw), lambda b, hp, i: (b, i, hp)),
                pl.BlockSpec((1, lk, qw), lambda b, hp, i: (b, 0, hp)),
                pl.BlockSpec((1, lk, LANES), lambda b, hp, i: (b, 0, hp))]
    args = [q, k, v]
    if decay:
        in_specs += [pl.BlockSpec((1, 2, tq, 1), lambda b, hp, i: (b, hp, i, 0)),
                     pl.BlockSpec((1, 2, 1, lk), lambda b, hp, i: (b, hp, 0, 0))]
        args += [qd, kd]
    kern = functools.partial(_flash_kernel, tq=tq, tk=tk, qpos0=qpos0, per_frame=per_frame,
                             wide=wide, decay=decay, nk_tiles=lk // tk)
    return pl.pallas_call(
        kern,
        grid=(bsz, N_HEADS // 2, lq // tq),
        in_specs=in_specs,
        out_specs=pl.BlockSpec((1, tq, LANES), lambda b, hp, i: (b, i, hp)),
        out_shape=jax.ShapeDtypeStruct((bsz, lq, N_HEADS * HEAD_DIM), BF16),
        scratch_shapes=[pltpu.VMEM((2, tq, 1), F32), pltpu.VMEM((2, tq, 1), F32),
                        pltpu.VMEM((2, tq, LANES), F32)],
        compiler_params=_params("parallel", "parallel", "arbitrary"),
        name="flash_attention",
    )(*args)


DECAY_K_ROW = HEAD_DIM
DECAY_Q_ROW = HEAD_DIM + 3


def _flash_t_kernel(*refs, tile, per_frame, decay):
    if decay:
        qt_ref, k_ref, vt_ref, qd_ref, o_ref, m_sc, acc_sc, p_sc, alpha_sc = refs
    else:
        qt_ref, k_ref, vt_ref, o_ref, m_sc, acc_sc, p_sc, alpha_sc = refs
    i = pl.program_id(2)
    q0 = i * tile
    qt = qt_ref[0]
    qs = []
    for hh in range(2):
        blk = qt[hh * LANES:(hh + 1) * LANES]
        if decay:
            hi, mid, lo = _split3(qd_ref[0, 0, hh:hh + 1, :])
            row = lax.broadcasted_iota(I32, (LANES, tile), 0)
            aug = jnp.where(row == DECAY_Q_ROW, hi, jnp.where(row == DECAY_Q_ROW + 1, mid,
                            jnp.where(row == DECAY_Q_ROW + 2, lo, 0.0)))
            aug = jnp.where((row >= DECAY_K_ROW) & (row < DECAY_Q_ROW), 1.0, aug)
            blk = (blk.astype(F32) + aug).astype(BF16)
        qs.append(blk)
    m_sc[...] = jnp.full(m_sc.shape, NEG_INF, F32)
    acc_sc[...] = jnp.zeros(acc_sc.shape, F32)
    qp = q0 + lax.broadcasted_iota(I32, (1, tile), 1)

    def scores(ks, masked):
        kt = k_ref[0, pl.ds(ks, tile), :]
        ss = [jnp.dot(kt[:, hh * LANES:(hh + 1) * LANES], qs[hh], preferred_element_type=F32) for hh in range(2)]
        if masked:
            kp = ks + lax.broadcasted_iota(I32, (tile, 1), 0)
            mask = (kp <= qp) if per_frame else ((kp >> CHUNK_SHIFT) <= (qp >> CHUNK_SHIFT))
            ss = [jnp.where(mask, s, NEG_INF) for s in ss]
        return ss

    def softmax(ss):
        m_old = m_sc[...]
        m_new = [jnp.maximum(m_old[hh:hh + 1], jnp.max(ss[hh], axis=0, keepdims=True)) for hh in range(2)]
        m_sc[...] = jnp.concatenate(m_new, axis=0)
        for hh in range(2):
            p_sc[hh] = jnp.exp2(ss[hh] - m_new[hh]).astype(BF16)
        alpha_sc[...] = jnp.concatenate([jnp.exp2(m_old[hh:hh + 1] - m_new[hh]) for hh in range(2)], axis=0)

    def accumulate(jt):
        vt = vt_ref[0, :, pl.ds(pl.multiple_of(jt * tile, tile), tile)]
        alpha = alpha_sc[...]
        for hh in range(2):
            pv = jnp.dot(vt[hh * V_ROWS:(hh + 1) * V_ROWS], p_sc[hh], preferred_element_type=F32)
            acc_sc[hh] = alpha[hh:hh + 1] * acc_sc[hh] + pv

    softmax(scores(pl.multiple_of(i * tile, tile), True))

    def body(j, j_prev):
        ss = scores(pl.multiple_of(j * tile, tile), False)
        accumulate(j_prev)
        softmax(ss)
        return j

    accumulate(lax.fori_loop(0, i, body, i))
    for hh in range(2):
        a = acc_sc[hh]
        o_ref[0, hh * HEAD_DIM:(hh + 1) * HEAD_DIM, :] = (a[:HEAD_DIM] / a[HEAD_DIM:HEAD_DIM + 1]).astype(BF16)


def _flash_t(qt, k, vt, *, per_frame, qd=None):
    bsz, _, L = qt.shape
    tile = FLASH_T_TILE
    decay = qd is not None
    in_specs = [pl.BlockSpec((1, 2 * LANES, tile), lambda b, hp, i: (b, hp, i)),
                pl.BlockSpec((1, L, 2 * LANES), lambda b, hp, i: (b, 0, hp)),
                pl.BlockSpec((1, 2 * V_ROWS, L), lambda b, hp, i: (b, hp, 0))]
    args = [qt, k, vt]
    if decay:
        in_specs.append(pl.BlockSpec((1, 1, 2, tile), lambda b, hp, i: (b, hp, 0, i)))
        args.append(qd.reshape(bsz, N_HEADS // 2, 2, L))
    return pl.pallas_call(
        functools.partial(_flash_t_kernel, tile=tile, per_frame=per_frame, decay=decay),
        grid=(bsz, N_HEADS // 2, L // tile),
        in_specs=in_specs,
        out_specs=pl.BlockSpec((1, 2 * HEAD_DIM, tile), lambda b, hp, i: (b, hp, i)),
        out_shape=jax.ShapeDtypeStruct((bsz, N_HEADS * HEAD_DIM, L), BF16),
        scratch_shapes=[pltpu.VMEM((2, tile), F32), pltpu.VMEM((2, V_ROWS, tile), F32),
                        pltpu.VMEM((2, tile, tile), BF16), pltpu.VMEM((2, tile), F32)],
        compiler_params=_params("parallel", "parallel", "arbitrary"),
        name="flash_attention_t",
    )(*args)


def _fox_proj_kernel(x_ref, g_ref, sc_ref, sh_ref, wq_ref, wk_ref, wv_ref, wf_ref, bf_ref,
                     q_ref, k_ref, v_ref, kb_ref, vb_ref, lf_ref):
    h = _modnorm(x_ref[0], g_ref[...], sc_ref[0], sh_ref[0]).astype(BF16)
    q_ref[0] = (jnp.dot(h, wq_ref[...], preferred_element_type=F32) * HEAD_SCALE).astype(BF16)
    k = jnp.dot(h, wk_ref[...], preferred_element_type=F32)
    k_ref[0] = k
    kb_ref[0] = k.astype(BF16)
    v = jnp.dot(h, wv_ref[...], preferred_element_type=F32)
    v_ref[0] = v
    vb_ref[0] = v.astype(BF16)
    f = jnp.dot(h, wf_ref[...], preferred_element_type=F32)[:, :N_HEADS] + bf_ref[...]
    lf_ref[0] = _log_sigmoid(f)


def _fox_proj(x, g, sc, sh, w, tm):
    bsz, L, d = x.shape
    hd = N_HEADS * HEAD_DIM
    row = lambda b, i: (b, i, 0)
    per_b = lambda b, i: (b, 0, 0)
    const = lambda b, i: (0, 0)
    return pl.pallas_call(
        _fox_proj_kernel,
        grid=(bsz, L // tm),
        in_specs=[pl.BlockSpec((1, tm, d), row),
                  pl.BlockSpec((1, d), const),
                  pl.BlockSpec((1, 1, d), per_b),
                  pl.BlockSpec((1, 1, d), per_b),
                  pl.BlockSpec((d, hd), const),
                  pl.BlockSpec((d, hd), const),
                  pl.BlockSpec((d, hd), const),
                  pl.BlockSpec((d, LANES), const),
                  pl.BlockSpec((1, N_HEADS), const)],
        out_specs=[pl.BlockSpec((1, tm, hd), row)] * 5 + [pl.BlockSpec((1, tm, N_HEADS), row)],
        out_shape=[jax.ShapeDtypeStruct((bsz, L, hd), BF16),
                   jax.ShapeDtypeStruct((bsz, L, hd), F32),
                   jax.ShapeDtypeStruct((bsz, L, hd), F32),
                   jax.ShapeDtypeStruct((bsz, L, hd), BF16),
                   jax.ShapeDtypeStruct((bsz, L, hd), BF16),
                   jax.ShapeDtypeStruct((bsz, L, N_HEADS), F32)],
        compiler_params=_params("parallel", "parallel"),
        name="fox_proj",
    )(x, g, sc, sh, w["b_wq"], w["b_wk"], w["b_wv"], w["b_wf"], w["b_bf"])


def _cumsum_kernel(x_ref, o_ref, carry_ref, *, tc):
    @pl.when(pl.program_id(1) == 0)
    def _():
        carry_ref[...] = jnp.zeros(carry_ref.shape, F32)

    r = lax.broadcasted_iota(I32, (tc, tc), 0)
    c = lax.broadcasted_iota(I32, (tc, tc), 1)
    upper = (r <= c).astype(BF16)
    h1, h2, h3 = [a.astype(BF16) for a in _split3(x_ref[0])]
    d = lambda a: jnp.dot(a, upper, preferred_element_type=F32)
    cum = ((d(h3) + d(h2)) + d(h1)) + carry_ref[...]
    o_ref[0] = cum
    carry_ref[...] = cum[:, tc - 1:tc]


def _cumsum_rows(x, tc):
    bsz, nh, L = x.shape
    return pl.pallas_call(
        functools.partial(_cumsum_kernel, tc=tc),
        grid=(bsz, L // tc),
        in_specs=[pl.BlockSpec((1, nh, tc), lambda b, j: (b, 0, j))],
        out_specs=pl.BlockSpec((1, nh, tc), lambda b, j: (b, 0, j)),
        out_shape=jax.ShapeDtypeStruct((bsz, nh, L), F32),
        scratch_shapes=[pltpu.VMEM((nh, 1), F32)],
        compiler_params=_params("parallel", "arbitrary"),
        name="cumsum",
    )(x)


def _fox_proj_t_kernel(x_ref, g_ref, sc_ref, sh_ref, wqt_ref, wk_ref, wv_ref, wvt_ref, ones_ref,
                       wf_ref, bf_ref, wft_ref, bft_ref, qt_ref, k_ref, v_ref, vt_ref, lf_ref, lft_ref):
    h = _modnorm(x_ref[0], g_ref[...], sc_ref[0], sh_ref[0]).astype(BF16)
    qt = lax.dot_general(wqt_ref[...], h, _NT, preferred_element_type=F32)
    qt_ref[0] = (qt * (HEAD_SCALE * LOG2E)).astype(BF16)
    k_ref[0] = jnp.dot(h, wk_ref[...], preferred_element_type=F32)
    v_ref[0] = jnp.dot(h, wv_ref[...], preferred_element_type=F32)
    vt = lax.dot_general(wvt_ref[...], h, _NT, preferred_element_type=F32) + ones_ref[...]
    vt_ref[0] = vt.astype(BF16)
    f = jnp.dot(h, wf_ref[...], preferred_element_type=F32)[:, :N_HEADS] + bf_ref[...]
    lf_ref[0] = _log_sigmoid(f)
    ft = lax.dot_general(wft_ref[...], h, _NT, preferred_element_type=F32)[:N_HEADS] + bft_ref[...]
    lft_ref[0] = _log_sigmoid(ft)


def _fox_proj_t(x, g, sc, sh, w, tm):
    bsz, L, d = x.shape
    hd = N_HEADS * HEAD_DIM
    hq = N_HEADS * LANES
    hv = N_HEADS * V_ROWS
    row = lambda b, i: (b, i, 0)
    col = lambda b, i: (b, 0, i)
    per_b = lambda b, i: (b, 0, 0)
    const = lambda b, i: (0, 0)
    return pl.pallas_call(
        _fox_proj_t_kernel,
        grid=(bsz, L // tm),
        in_specs=[pl.BlockSpec((1, tm, d), row),
                  pl.BlockSpec((1, d), const),
                  pl.BlockSpec((1, 1, d), per_b),
                  pl.BlockSpec((1, 1, d), per_b),
                  pl.BlockSpec((hq, d), const),
                  pl.BlockSpec((d, hd), const),
                  pl.BlockSpec((d, hd), const),
                  pl.BlockSpec((hv, d), const),
                  pl.BlockSpec((hv, 1), const),
                  pl.BlockSpec((d, LANES), const),
                  pl.BlockSpec((1, N_HEADS), const),
                  pl.BlockSpec((LANES, d), const),
                  pl.BlockSpec((N_HEADS, 1), const)],
        out_specs=[pl.BlockSpec((1, hq, tm), col), pl.BlockSpec((1, tm, hd), row), pl.BlockSpec((1, tm, hd), row),
                   pl.BlockSpec((1, hv, tm), col), pl.BlockSpec((1, tm, N_HEADS), row),
                   pl.BlockSpec((1, N_HEADS, tm), col)],
        out_shape=[jax.ShapeDtypeStruct((bsz, hq, L), BF16),
                   jax.ShapeDtypeStruct((bsz, L, hd), F32),
                   jax.ShapeDtypeStruct((bsz, L, hd), F32),
                   jax.ShapeDtypeStruct((bsz, hv, L), BF16),
                   jax.ShapeDtypeStruct((bsz, L, N_HEADS), F32),
                   jax.ShapeDtypeStruct((bsz, N_HEADS, L), F32)],
        compiler_params=_params("parallel", "parallel"),
        name="fox_proj_t",
    )(x, g, sc, sh, w["b_wqt"], w["b_wk"], w["b_wv"], w["b_wvt"], w["ones_rows"],
      w["b_wf"], w["b_bf"], w["b_wf"].T, w["b_bf"].T)


def _fox_cumaug_kernel(lft_ref, lf_ref, k_ref, cumt_ref, kaug_ref, crow_sc, ccol_sc, *, tc):
    @pl.when(pl.program_id(1) == 0)
    def _():
        crow_sc[...] = jnp.zeros(crow_sc.shape, F32)
        ccol_sc[...] = jnp.zeros(ccol_sc.shape, F32)

    r = lax.broadcasted_iota(I32, (tc, tc), 0)
    c = lax.broadcasted_iota(I32, (tc, tc), 1)
    upper = (r <= c).astype(BF16)
    lower = (c <= r).astype(BF16)
    xh, xm, xl = [a.astype(BF16) for a in _split3(lft_ref[0])]
    dr = lambda a: jnp.dot(a, upper, preferred_element_type=F32)
    cumt = ((dr(xl) + dr(xm)) + dr(xh)) + crow_sc[...]
    crow_sc[...] = cumt[:, tc - 1:tc]
    cumt_ref[0] = cumt * LOG2E
    yh, ym, yl = [a.astype(BF16) for a in _split3(lf_ref[0])]
    dc = lambda a: jnp.dot(lower, a, preferred_element_type=F32)
    cum = ((dc(yl) + dc(ym)) + dc(yh)) + ccol_sc[...]
    ccol_sc[...] = cum[tc - 1:tc, :]
    neg = cum * (-LOG2E)
    k = k_ref[0]
    lane = lax.broadcasted_iota(I32, (tc, LANES - HEAD_DIM), 1)
    for h in range(N_HEADS):
        hi, mid, lo = _split3(neg[:, h:h + 1])
        aug = jnp.where(lane == 0, hi, jnp.where(lane == 1, mid, jnp.where(lane == 2, lo, 0.0)))
        aug = jnp.where((lane >= DECAY_Q_ROW - HEAD_DIM) & (lane < DECAY_Q_ROW - HEAD_DIM + 3), 1.0, aug)
        kaug_ref[0, :, h * LANES:(h + 1) * LANES] = jnp.concatenate(
            [k[:, h * HEAD_DIM:(h + 1) * HEAD_DIM], aug], axis=1).astype(BF16)


def _fox_cumaug(lft, lf, k, tc):
    bsz, nh, L = lft.shape
    hd = k.shape[2]
    return pl.pallas_call(
        functools.partial(_fox_cumaug_kernel, tc=tc),
        grid=(bsz, L // tc),
        in_specs=[pl.BlockSpec((1, nh, tc), lambda b, j: (b, 0, j)),
                  pl.BlockSpec((1, tc, nh), lambda b, j: (b, j, 0)),
                  pl.BlockSpec((1, tc, hd), lambda b, j: (b, j, 0))],
        out_specs=[pl.BlockSpec((1, nh, tc), lambda b, j: (b, 0, j)),
                   pl.BlockSpec((1, tc, nh * LANES), lambda b, j: (b, j, 0))],
        out_shape=[jax.ShapeDtypeStruct((bsz, nh, L), F32),
                   jax.ShapeDtypeStruct((bsz, L, nh * LANES), BF16)],
        scratch_shapes=[pltpu.VMEM((nh, 1), F32), pltpu.VMEM((1, nh), F32)],
        compiler_params=_params("parallel", "arbitrary"),
        name="fox_cumsum_aug",
    )(lft, lf, k)


def _dsa_proj_kernel(x_ref, g_ref, sc_ref, sh_ref, wqt_ref, wqit_ref, wsm_ref, wsmt_ref,
                     qt_ref, qit_ref, k_ref, v_ref, ki_ref, wit_ref, kaug_ref, vt_ref, kib_ref):
    h = _modnorm(x_ref[0], g_ref[...], sc_ref[0], sh_ref[0]).astype(BF16)
    tm = h.shape[0]
    qt = lax.dot_general(wqt_ref[...], h, _NT, preferred_element_type=F32)
    qt_ref[0] = (qt * (HEAD_SCALE * LOG2E)).astype(BF16)
    qit = lax.dot_general(wqit_ref[...], h, _NT, preferred_element_type=F32)
    qit_ref[0] = (qit * (C_IDX_DIM ** -0.5)).astype(BF16)
    sm = jnp.dot(h, wsm_ref[...], preferred_element_type=F32)
    smt = lax.dot_general(wsmt_ref[...], h, _NT, preferred_element_type=F32)
    k = sm[:, :HEAD_DIM]
    ki = sm[:, 2 * HEAD_DIM:2 * HEAD_DIM + C_IDX_DIM]
    k_ref[0] = k
    v_ref[0] = sm[:, HEAD_DIM:2 * HEAD_DIM]
    ki_ref[0] = ki
    kib_ref[0] = ki.astype(BF16)
    lane = lax.broadcasted_iota(I32, (tm, LANES - HEAD_DIM), 1)
    ones_cols = jnp.where(lane < 2, 1.0, 0.0)
    kaug_ref[0] = jnp.concatenate([k, ones_cols], axis=1).astype(BF16)
    row = lax.broadcasted_iota(I32, (V_ROWS - HEAD_DIM, tm), 0)
    ones_row = jnp.where(row == 0, 1.0, 0.0)
    vt_ref[0] = jnp.concatenate([smt[HEAD_DIM:2 * HEAD_DIM], ones_row], axis=0).astype(BF16)
    o = 2 * HEAD_DIM + C_IDX_DIM
    wit_ref[0] = smt[o:o + C_IDX_HEADS] * (C_IDX_HEADS ** -0.5)


def _dsa_proj(x, g, sc, sh, w, tm):
    bsz, L, d = x.shape
    hi = C_IDX_HEADS * C_IDX_DIM
    hq = N_HEADS * LANES
    row = lambda b, i: (b, i, 0)
    col = lambda b, i: (b, 0, i)
    per_b = lambda b, i: (b, 0, 0)
    const = lambda b, i: (0, 0)
    small = lambda n, dt: jax.ShapeDtypeStruct((bsz, L, n), dt)
    tall = lambda n, dt: jax.ShapeDtypeStruct((bsz, n, L), dt)
    return pl.pallas_call(
        _dsa_proj_kernel,
        grid=(bsz, L // tm),
        in_specs=[pl.BlockSpec((1, tm, d), row),
                  pl.BlockSpec((1, d), const),
                  pl.BlockSpec((1, 1, d), per_b),
                  pl.BlockSpec((1, 1, d), per_b),
                  pl.BlockSpec((hq, d), const),
                  pl.BlockSpec((hi, d), const),
                  pl.BlockSpec((d, 2 * LANES), const),
                  pl.BlockSpec((2 * LANES, d), const)],
        out_specs=[pl.BlockSpec((1, hq, tm), col), pl.BlockSpec((1, hi, tm), col),
                   pl.BlockSpec((1, tm, HEAD_DIM), row), pl.BlockSpec((1, tm, HEAD_DIM), row),
                   pl.BlockSpec((1, tm, C_IDX_DIM), row), pl.BlockSpec((1, C_IDX_HEADS, tm), col),
                   pl.BlockSpec((1, tm, LANES), row), pl.BlockSpec((1, V_ROWS, tm), col),
                   pl.BlockSpec((1, tm, C_IDX_DIM), row)],
        out_shape=[tall(hq, BF16), tall(hi, BF16), small(HEAD_DIM, F32), small(HEAD_DIM, F32),
                   small(C_IDX_DIM, F32), tall(C_IDX_HEADS, F32),
                   small(LANES, BF16), tall(V_ROWS, BF16), small(C_IDX_DIM, BF16)],
        compiler_params=_params("parallel", "parallel"),
        name="dsa_proj",
    )(x, g, sc, sh, w["c_wqt"], w["c_wqit"], w["c_wsm"], w["c_wsmt"])


_INT_MIN = -2 ** 31
_COUNT_ROWS = 64


def _dsa_kernel(qt_ref, qit_ref, wit_ref, k_ref, vt_ref, ki_ref, farq_ref, nb_ref, o_ref,
                sk_sc, m_sc, acc_sc, p_sc, alpha_sc, *, qpos0, n_sel, nk_tiles):
    tq, tk = DSA_TQ, DSA_TK
    i = pl.program_id(1)
    q0 = qpos0 + i * tq
    nt = jnp.minimum((q0 + tq + tk - 1) // tk, nk_tiles)
    qch = (q0 + lax.broadcasted_iota(I32, (1, tq), 1)) >> CHUNK_SHIFT

    def admissible(ks, w):
        kp = ks + lax.broadcasted_iota(I32, (w, 1), 0)
        return (kp >> CHUNK_SHIFT) <= qch

    qit = qit_ref[0]
    qis = jnp.concatenate([qit[h * C_IDX_DIM:(h + 1) * C_IDX_DIM] for h in range(C_IDX_HEADS)], axis=1)
    wit = wit_ref[0]

    def score_body(j, c):
        ks = pl.multiple_of(j * tk, tk)
        d = jnp.dot(ki_ref[0, pl.ds(ks, tk), :], qis, preferred_element_type=F32)
        sc = jnp.zeros((tk, tq), F32)
        for h in range(C_IDX_HEADS):
            sc = sc + jnp.maximum(d[:, h * tq:(h + 1) * tq], 0.0) * wit[h:h + 1]
        sc = jnp.where(sc == 0.0, 0.0, sc)
        sc = jnp.where(admissible(ks, tk), sc, NEG_INF)
        bits = pltpu.bitcast(sc, I32)
        sk_sc[pl.ds(ks, tk), :] = bits ^ ((bits >> 31) & 0x7FFFFFFF)
        return c

    lax.fori_loop(0, nt, score_body, 0)
    tiles_per_step = DSA_SEARCH_ROWS // tk
    n_steps = (nt + tiles_per_step - 1) // tiles_per_step

    @pl.when(nt < n_steps * tiles_per_step)
    def _():
        sk_sc[pl.ds(pl.multiple_of(nt * tk, tk), tk), :] = jnp.full((tk, tq), _INT_MIN, I32)

    def count(pred):
        def body(j, c):
            kt = sk_sc[pl.ds(pl.multiple_of(j * DSA_SEARCH_ROWS, DSA_SEARCH_ROWS), DSA_SEARCH_ROWS), :]
            g = jnp.where(pred(kt), 1.0, 0.0)
            parts = [g[r * _COUNT_ROWS:(r + 1) * _COUNT_ROWS] for r in range(DSA_SEARCH_ROWS // _COUNT_ROWS)]
            while len(parts) > 1:
                parts = [parts[a] + parts[a + 1] for a in range(0, len(parts), 2)]
            return c + parts[0]
        c = lax.fori_loop(0, n_steps, body, jnp.zeros((_COUNT_ROWS, tq), F32))
        return jnp.sum(c, axis=0, keepdims=True)

    nsel = float(n_sel)
    lo = jnp.where(count(lambda kt: kt >= 0) >= nsel, 0, _INT_MIN).astype(I32)

    def bit_body(t, lo):
        cand = lo | jnp.left_shift(jnp.int32(1), 30 - t)
        return jnp.where(count(lambda kt: kt >= cand) >= nsel, cand, lo)

    thr = lax.fori_loop(0, 31, bit_body, lo)
    need = nsel - count(lambda kt: kt > thr)

    qt = qt_ref[0]
    qs = jnp.concatenate([qt[h * LANES:(h + 1) * LANES] for h in range(N_HEADS)], axis=1) + farq_ref[...]
    m_sc[...] = jnp.full(m_sc.shape, NEG_INF, F32)
    acc_sc[...] = jnp.zeros(acc_sc.shape, F32)
    ra = lax.broadcasted_iota(I32, (tk, tk), 0)
    ca = lax.broadcasted_iota(I32, (tk, tk), 1)
    earlier = (ca < ra).astype(BF16)

    def select(ks, w, run):
        kt = sk_sc[pl.ds(ks, w), :]
        eq = kt == thr
        rank = run + jnp.dot(earlier[:w, :w], jnp.where(eq, 1.0, 0.0).astype(BF16), preferred_element_type=F32)
        sel = ((kt > thr) | (eq & (rank < need))) & admissible(ks, w)
        return sel, run + jnp.sum(jnp.where(eq, 1.0, 0.0), axis=0, keepdims=True)

    def scores(ks, w):
        return jnp.dot(k_ref[0, pl.ds(ks, w), :], qs, preferred_element_type=F32)

    def softmax(s, sel, w, kind):
        for h in range(N_HEADS):
            sl = slice(h * tq, (h + 1) * tq)
            sh = s[:, sl]
            if kind is not None:
                sh = sh + nb_ref[kind, :, sl]
            sh = jnp.where(sel, sh, NEG_INF)
            m_old = m_sc[:, sl]
            m_new = jnp.maximum(m_old, jnp.max(sh, axis=0, keepdims=True))
            m_sc[:, sl] = m_new
            alpha_sc[:, sl] = jnp.exp2(m_old - m_new)
            p_sc[0:w, sl] = jnp.exp2(sh - m_new).astype(BF16)

    def accumulate(ks, w):
        pv = jnp.dot(vt_ref[0, :, pl.ds(ks, w)], p_sc[0:w, :], preferred_element_type=F32)
        acc_sc[...] = alpha_sc[...] * acc_sc[...] + pv

    p_sc[...] = jnp.zeros(p_sc.shape, BF16)
    alpha_sc[...] = jnp.ones(alpha_sc.shape, F32)
    n_far = jnp.maximum(q0 - tq, 0) // tk

    def far_body(j, carry):
        j_prev, run = carry
        ks = pl.multiple_of(j * tk, tk)
        sel, run = select(ks, tk, run)
        s = scores(ks, tk)
        accumulate(pl.multiple_of(j_prev * tk, tk), tk)
        softmax(s, sel, tk, None)
        return j, run

    j_last, run = lax.fori_loop(0, n_far, far_body, (0, jnp.zeros((1, tq), F32)))
    accumulate(pl.multiple_of(j_last * tk, tk), tk)
    ks0 = n_far * tk
    n_tail = (q0 + tq - ks0) // tq

    def tail_body(t, run):
        ks = pl.multiple_of(ks0 + t * tq, tq)
        kind = jnp.clip((ks - q0) // tq + 2, 0, 2)
        sel, run = select(ks, tq, run)
        softmax(scores(ks, tq), sel, tq, kind)
        accumulate(ks, tq)
        return run

    lax.fori_loop(0, n_tail, tail_body, run)
    acc = acc_sc[...]
    ot = jnp.concatenate([acc[:HEAD_DIM, h * tq:(h + 1) * tq] / acc[HEAD_DIM:HEAD_DIM + 1, h * tq:(h + 1) * tq]
                          for h in range(N_HEADS)], axis=0)
    o_ref[0] = ot.T.astype(BF16)


def _dsa_attention(qt, qit, wit, k, vt, ki, farq, nb, *, qpos0, n_sel):
    bsz, hq, lq = qt.shape
    lk = k.shape[1]
    hi = qit.shape[1]
    hd = N_HEADS * HEAD_DIM
    col = lambda b, i: (b, 0, i)
    whole = lambda b, i: (b, 0, 0)
    kern = functools.partial(_dsa_kernel, qpos0=qpos0, n_sel=n_sel, nk_tiles=lk // DSA_TK)
    return pl.pallas_call(
        kern,
        grid=(bsz, lq // DSA_TQ),
        in_specs=[pl.BlockSpec((1, hq, DSA_TQ), col),
                  pl.BlockSpec((1, hi, DSA_TQ), col),
                  pl.BlockSpec((1, C_IDX_HEADS, DSA_TQ), col),
                  pl.BlockSpec((1, lk, LANES), whole),
                  pl.BlockSpec((1, V_ROWS, lk), whole),
                  pl.BlockSpec((1, lk, C_IDX_DIM), whole),
                  pl.BlockSpec((LANES, N_HEADS * DSA_TQ), lambda b, i: (0, 0)),
                  pl.BlockSpec((3, DSA_TQ, N_HEADS * DSA_TQ), lambda b, i: (0, 0, 0))],
        out_specs=pl.BlockSpec((1, DSA_TQ, hd), lambda b, i: (b, i, 0)),
        out_shape=jax.ShapeDtypeStruct((bsz, lq, hd), BF16),
        scratch_shapes=[pltpu.VMEM((lk, DSA_TQ), I32),
                        pltpu.VMEM((1, N_HEADS * DSA_TQ), F32),
                        pltpu.VMEM((V_ROWS, N_HEADS * DSA_TQ), F32),
                        pltpu.VMEM((DSA_TK, N_HEADS * DSA_TQ), BF16),
                        pltpu.VMEM((1, N_HEADS * DSA_TQ), F32)],
        compiler_params=_params("parallel", "arbitrary"),
        name="dsa_attention",
    )(qt, qit, wit, k, vt, ki, farq, nb)


def _swa_proj_kernel(x_ref, g_ref, sc_ref, sh_ref, wq_ref, wkv_ref, q_ref, kv_ref, kvb_ref):
    h = _modnorm(x_ref[0], g_ref[...], sc_ref[0], sh_ref[0]).astype(BF16)
    q_ref[0] = (jnp.dot(h, wq_ref[...], preferred_element_type=F32) * HEAD_SCALE).astype(BF16)
    kv = jnp.dot(h, wkv_ref[...], preferred_element_type=F32)
    kv_ref[0] = kv
    kvb_ref[0] = kv.astype(BF16)


def _swa_proj(x, g, sc, sh, w, tm):
    bsz, L, d = x.shape
    hd = N_HEADS * HEAD_DIM
    kvw = 2 * D_KV_HEADS * HEAD_DIM
    row = lambda b, i: (b, i, 0)
    per_b = lambda b, i: (b, 0, 0)
    const = lambda b, i: (0, 0)
    return pl.pallas_call(
        _swa_proj_kernel,
        grid=(bsz, L // tm),
        in_specs=[pl.BlockSpec((1, tm, d), row),
                  pl.BlockSpec((1, d), const),
                  pl.BlockSpec((1, 1, d), per_b),
                  pl.BlockSpec((1, 1, d), per_b),
                  pl.BlockSpec((d, hd), const),
                  pl.BlockSpec((d, kvw), const)],
        out_specs=[pl.BlockSpec((1, tm, hd), row), pl.BlockSpec((1, tm, kvw), row),
                   pl.BlockSpec((1, tm, kvw), row)],
        out_shape=[jax.ShapeDtypeStruct((bsz, L, hd), BF16),
                   jax.ShapeDtypeStruct((bsz, L, kvw), F32),
                   jax.ShapeDtypeStruct((bsz, L, kvw), BF16)],
        compiler_params=_params("parallel", "parallel"),
        name="swa_proj",
    )(x, g, sc, sh, w["d_wq"], w["d_wkv"])


def _swa_kernel(q_ref, kv_ref, nb_ref, sink_ref, o_ref, *, mask_front):
    band = WINDOW + CHUNK
    c = pl.program_id(1)
    start = pl.multiple_of(c * CHUNK, CHUNK)
    kvb = kv_ref[0, pl.ds(start, band), :]
    q = q_ref[0]
    if mask_front:
        ok = (start - WINDOW + lax.broadcasted_iota(I32, (1, 1, band), 2)) >= 0
    kw = D_KV_HEADS * HEAD_DIM
    for g in range(D_KV_HEADS):
        kg = kvb[:, g * HEAD_DIM:(g + 1) * HEAD_DIM]
        vg = kvb[:, kw + g * HEAD_DIM:kw + (g + 1) * HEAD_DIM]
        qg = jnp.concatenate([q[:, (g * D_REP + r) * HEAD_DIM:(g * D_REP + r + 1) * HEAD_DIM]
                              for r in range(D_REP)], axis=0)
        s = lax.dot_general(qg, kg, _NT, preferred_element_type=F32).reshape(D_REP, CHUNK, band)
        s = s + nb_ref[g * D_REP:(g + 1) * D_REP]
        if mask_front:
            s = jnp.where(ok, s, NEG_INF)
        sink = sink_ref[g * D_REP:(g + 1) * D_REP]
        m = jnp.maximum(jnp.max(s, axis=2, keepdims=True), sink)
        e = jnp.exp(s - m)
        p = e / (jnp.sum(e, axis=2, keepdims=True) + jnp.exp(sink - m))
        o = jnp.dot(p.reshape(D_REP * CHUNK, band).astype(BF16), vg, preferred_element_type=F32)
        for r in range(D_REP):
            hh = g * D_REP + r
            o_ref[0, :, hh * HEAD_DIM:(hh + 1) * HEAD_DIM] = o[r * CHUNK:(r + 1) * CHUNK].astype(BF16)


def _swa_attention(q, kv, nb, sinks, *, mask_front):
    bsz, L, hd = q.shape
    lkv, kvw = kv.shape[1], kv.shape[2]
    return pl.pallas_call(
        functools.partial(_swa_kernel, mask_front=mask_front),
        grid=(bsz, L // CHUNK),
        in_specs=[pl.BlockSpec((1, CHUNK, hd), lambda b, c: (b, c, 0)),
                  pl.BlockSpec((1, lkv, kvw), lambda b, c: (b, 0, 0)),
                  pl.BlockSpec((N_HEADS, CHUNK, WINDOW + CHUNK), lambda b, c: (0, 0, 0)),
                  pl.BlockSpec((N_HEADS, 1, 1), lambda b, c: (0, 0, 0))],
        out_specs=pl.BlockSpec((1, CHUNK, hd), lambda b, c: (b, c, 0)),
        out_shape=jax.ShapeDtypeStruct((bsz, L, hd), BF16),
        compiler_params=_params("parallel", "parallel"),
        name="swa_attention",
    )(q, kv, nb, sinks)


def _route_rows(s, sb):
    n, m = N_GROUPS, EXPERTS_PER_GROUP
    grp = []
    for g in range(n):
        x = sb[g * m:(g + 1) * m]
        best = None
        for a in range(m):
            for b in range(a + 1, m):
                pair = x[a] + x[b]
                best = pair if best is None else jnp.maximum(best, pair)
        grp.append(best)
    chosen_g = []
    taken = None
    for g in range(n):
        is_g = None
        for o in range(g + 1, n):
            c = grp[g] >= grp[o]
            is_g = c if is_g is None else (is_g & c)
        if is_g is None:
            is_g = ~taken
        elif taken is not None:
            is_g = is_g & (~taken)
        taken = is_g if taken is None else (taken | is_g)
        chosen_g.append(is_g)
    picked = []
    for e in range(N_EXPERTS):
        g, a = divmod(e, m)
        beaten = jnp.zeros(sb[e].shape, F32)
        for b in range(m):
            if b == a:
                continue
            o = g * m + b
            wins = (sb[o] > sb[e]) | ((sb[o] == sb[e]) & (b < a))
            beaten = beaten + wins.astype(F32)
        picked.append(chosen_g[g] & (beaten < 2.0))
    tops = [jnp.where(picked[e], s[e], 0.0) for e in range(N_EXPERTS)]
    denom = tops[0]
    for e in range(1, N_EXPERTS):
        denom = denom + tops[e]
    return [t / denom for t in tops]


def _post_mix_kernel(o_ref, wo_ref, x_ref, gt_ref, g_ref, sc_ref, sh_ref, wr_ref, br_ref,
                     x1_ref, h2_ref, gates_ref, *, o_transposed):
    mixed = lax.dot_general(o_ref[0], wo_ref[...], _TN if o_transposed else _NN, preferred_element_type=F32)
    x1 = x_ref[0] + gt_ref[0] * mixed
    x1_ref[0] = x1
    h2 = _modnorm(x1, g_ref[...], sc_ref[0], sh_ref[0])
    h2_ref[0] = h2.astype(BF16)
    logits = _dot3(wr_ref[...], h2, _NT)
    s = _sigmoid(logits)
    sb = s + br_ref[...]
    rows = _route_rows([s[e:e + 1] for e in range(N_EXPERTS)], [sb[e:e + 1] for e in range(N_EXPERTS)])
    gates_ref[0] = jnp.concatenate(rows, axis=0)


def _post_mix(o, wo, x, gt, g, sc, sh, wr_t, br, tm, o_transposed=False):
    bsz, L, d = x.shape
    hd = wo.shape[0]
    o_spec = (pl.BlockSpec((1, hd, tm), lambda b, i: (b, 0, i)) if o_transposed
              else pl.BlockSpec((1, tm, hd), lambda b, i: (b, i, 0)))
    row = lambda b, i: (b, i, 0)
    per_b = lambda b, i: (b, 0, 0)
    const = lambda b, i: (0, 0)
    return pl.pallas_call(
        functools.partial(_post_mix_kernel, o_transposed=o_transposed),
        grid=(bsz, L // tm),
        in_specs=[o_spec,
                  pl.BlockSpec((hd, d), const),
                  pl.BlockSpec((1, tm, d), row),
                  pl.BlockSpec((1, 1, d), per_b),
                  pl.BlockSpec((1, d), const),
                  pl.BlockSpec((1, 1, d), per_b),
                  pl.BlockSpec((1, 1, d), per_b),
                  pl.BlockSpec((N_EXPERTS, d), const),
                  pl.BlockSpec((N_EXPERTS, 1), const)],
        out_specs=[pl.BlockSpec((1, tm, d), row), pl.BlockSpec((1, tm, d), row),
                   pl.BlockSpec((1, N_EXPERTS, tm), lambda b, i: (b, 0, i))],
        out_shape=[jax.ShapeDtypeStruct((bsz, L, d), F32),
                   jax.ShapeDtypeStruct((bsz, L, d), BF16),
                   jax.ShapeDtypeStruct((bsz, N_EXPERTS, L), F32)],
        compiler_params=_params("parallel", "parallel"),
        name="post_mix_route",
    )(o, wo, x, gt, g, sc, sh, wr_t, br)


def _moe_kernel(x_ref, h_ref, gates_ref, gt_ref, wgu_ref, wd_ref, o_ref, acc_sc):
    e = pl.program_id(2)

    @pl.when(e == 0)
    def _():
        acc_sc[...] = jnp.zeros(acc_sc.shape, F32)

    gu = jnp.dot(h_ref[0], wgu_ref[0], preferred_element_type=F32)
    gate = gu[:, :D_EXPERT]
    act = (gate * _sigmoid(gate)) * gu[:, D_EXPERT:]
    y = jnp.dot(act.astype(BF16), wd_ref[0], preferred_element_type=F32)
    gates = gates_ref[0]
    lane = lax.broadcasted_iota(I32, gates.shape, 1)
    ge = jnp.sum(jnp.where(lane == e, gates, 0.0), axis=1, keepdims=True)
    acc_sc[...] += ge * y

    @pl.when(e == N_EXPERTS - 1)
    def _():
        o_ref[0] = x_ref[0] + gt_ref[0] * acc_sc[...]


def _moe(x, h, gates, gt, wgu, wd, tm):
    bsz, L, d = x.shape
    row = lambda b, i, e: (b, i, 0)
    return pl.pallas_call(
        _moe_kernel,
        grid=(bsz, L // tm, N_EXPERTS),
        in_specs=[pl.BlockSpec((1, tm, d), row),
                  pl.BlockSpec((1, tm, d), row),
                  pl.BlockSpec((1, tm, N_EXPERTS), row),
                  pl.BlockSpec((1, 1, d), lambda b, i, e: (b, 0, 0)),
                  pl.BlockSpec((1, d, 2 * D_EXPERT), lambda b, i, e: (e, 0, 0)),
                  pl.BlockSpec((1, D_EXPERT, d), lambda b, i, e: (e, 0, 0))],
        out_specs=pl.BlockSpec((1, tm, d), row),
        out_shape=jax.ShapeDtypeStruct((bsz, L, d), F32),
        scratch_shapes=[pltpu.VMEM((tm, d), F32)],
        compiler_params=_params("parallel", "parallel", "arbitrary"),
        name="moe_experts",
    )(x, h, gates, gt, wgu, wd)


def _final_norm_kernel(x_ref, g_ref, o_ref):
    o_ref[0] = _rms(x_ref[0]) * g_ref[...]


def _final_norm(x, g, tm):
    bsz, L, d = x.shape
    row = lambda b, i: (b, i, 0)
    return pl.pallas_call(
        _final_norm_kernel,
        grid=(bsz, L // tm),
        in_specs=[pl.BlockSpec((1, tm, d), row), pl.BlockSpec((1, d), lambda b, i: (0, 0))],
        out_specs=pl.BlockSpec((1, tm, d), row),
        out_shape=jax.ShapeDtypeStruct((bsz, L, d), F32),
        compiler_params=_params("parallel", "parallel"),
        name="final_norm",
    )(x, g)


def _rot_cols(w):
    half = w.shape[-1] // 2
    return jnp.concatenate([-w[..., half:], w[..., :half]], axis=-1)


def _head_pad(w2d, width):
    w3 = w2d.reshape(w2d.shape[0], N_HEADS, HEAD_DIM)
    return jnp.pad(w3, ((0, 0), (0, 0), (0, width - HEAD_DIM))).reshape(w2d.shape[0], N_HEADS * width)


def _prep_weights(a_w_in, a_g_q, a_w_uq, a_g_kv, a_w_ukv, a_w_o, b_w_in, b_b_f, b_w_o,
                  c_w_in, c_w_o, d_w_in, d_sinks, d_w_o, moe_w_router, moe_b_router,
                  moe_w_gate, moe_w_up, moe_w_down):
    w = {}
    hd = N_HEADS * HEAD_DIM
    kr = a_w_in[:, A_Q_LORA + A_KV_LORA:]
    w["a_in"] = jnp.concatenate([a_w_in, _rot_cols(kr)], axis=1).astype(BF16)
    w["a_gq"] = a_g_q.reshape(1, -1)
    w["a_gkv"] = a_g_kv.reshape(1, -1)
    uq = a_w_uq.reshape(A_Q_LORA, N_HEADS, A_NOPE + A_ROPE)
    pad = LANES - A_NOPE - A_ROPE
    zq = lambda n: jnp.zeros((A_Q_LORA, N_HEADS, n), F32)
    w["a_wq"] = jnp.concatenate([uq, zq(pad)], axis=-1).reshape(A_Q_LORA, N_HEADS * LANES).astype(BF16)
    w["a_wqs"] = jnp.concatenate([zq(A_NOPE), _rot_cols(uq[..., A_NOPE:]), zq(pad)],
                                 axis=-1).reshape(A_Q_LORA, N_HEADS * LANES).astype(BF16)
    ukv = a_w_ukv.reshape(A_KV_LORA, N_HEADS, A_NOPE + A_V)
    w["a_wk"] = jnp.concatenate([ukv[..., :A_NOPE], jnp.zeros((A_KV_LORA, N_HEADS, LANES - A_NOPE), F32)],
                                axis=-1).reshape(A_KV_LORA, N_HEADS * LANES).astype(BF16)
    w["a_wv"] = ukv[..., A_NOPE:].reshape(A_KV_LORA, N_HEADS * A_V).astype(BF16)
    w["a_wvt"] = _head_pad(ukv[..., A_NOPE:].reshape(A_KV_LORA, N_HEADS * A_V), V_ROWS).T.astype(BF16)
    sel = np.zeros((A_ROPE, N_HEADS, LANES), np.float32)
    for r in range(A_ROPE):
        sel[r, :, A_NOPE + r] = 1.0
    w["a_sel"] = jnp.asarray(sel.reshape(A_ROPE, N_HEADS * LANES), BF16)
    w["a_wo"] = a_w_o.astype(BF16)
    ones_rows = np.zeros((N_HEADS, V_ROWS, 1), np.float32)
    ones_rows[:, HEAD_DIM, 0] = 1.0
    w["ones_rows"] = jnp.asarray(ones_rows.reshape(N_HEADS * V_ROWS, 1))
    w["b_wq"] = b_w_in[:, :hd].astype(BF16)
    w["b_wk"] = b_w_in[:, hd:2 * hd].astype(BF16)
    w["b_wv"] = b_w_in[:, 2 * hd:3 * hd].astype(BF16)
    w["b_wf"] = jnp.pad(b_w_in[:, 3 * hd:], ((0, 0), (0, LANES - N_HEADS))).astype(BF16)
    w["b_bf"] = b_b_f.reshape(1, N_HEADS)
    w["b_wo"] = b_w_o.astype(BF16)
    w["b_wqt"] = _head_pad(b_w_in[:, :hd], LANES).T.astype(BF16)
    w["b_wvt"] = _head_pad(b_w_in[:, 2 * hd:3 * hd], V_ROWS).T.astype(BF16)
    hi = C_IDX_HEADS * C_IDX_DIM
    w["c_wqt"] = _head_pad(c_w_in[:, :hd], LANES).T.astype(BF16)
    kv = c_w_in[:, hd:hd + 2 * HEAD_DIM]
    qi = c_w_in[:, hd + 2 * HEAD_DIM:hd + 2 * HEAD_DIM + hi]
    rest = c_w_in[:, hd + 2 * HEAD_DIM + hi:]
    w["c_wqit"] = qi.T.astype(BF16)
    sm = jnp.concatenate([kv, rest], axis=1)
    w["c_wsm"] = jnp.pad(sm, ((0, 0), (0, 2 * LANES - sm.shape[1]))).astype(BF16)
    w["c_wsmt"] = w["c_wsm"].T
    w["c_wo"] = c_w_o.astype(BF16)
    w["d_wq"] = d_w_in[:, :hd].astype(BF16)
    w["d_wkv"] = d_w_in[:, hd:].astype(BF16)
    w["d_sinks"] = d_sinks.astype(F32).reshape(N_HEADS, 1, 1)
    w["d_wo"] = d_w_o.astype(BF16)
    w["wr_t"] = moe_w_router.T
    w["br"] = moe_b_router.reshape(N_EXPERTS, 1)
    w["wgu"] = jnp.concatenate([moe_w_gate, moe_w_up], axis=-1).astype(BF16)
    w["wd"] = moe_w_down.astype(BF16)
    return w


def _rope_tables(pos):
    half = A_ROPE // 2
    inv = ROPE_THETA ** (-jnp.arange(half, dtype=F32) / half)
    ang = pos.astype(F32)[:, None] * inv[None, :]
    cos, sin = jnp.cos(ang), jnp.sin(ang)
    n = pos.shape[0]
    pad = LANES - A_NOPE - A_ROPE
    cosp = jnp.concatenate([jnp.ones((n, A_NOPE), F32), cos, cos, jnp.zeros((n, pad), F32)], axis=1)
    sinp = jnp.concatenate([jnp.zeros((n, A_NOPE), F32), sin, sin, jnp.zeros((n, pad), F32)], axis=1)
    return cosp, sinp


def _pad_rows(a, n):
    return jnp.pad(a, ((0, 0), (0, n - a.shape[1])) + ((0, 0),) * (a.ndim - 2))


def _round_up(n, m):
    return (n + m - 1) // m * m


def kernel(x_prompt, x_sample, c_prompt, c_sample, cache_a_latent, cache_a_krope, cache_b_k, cache_b_v, cache_b_logf, cache_c_k, cache_c_v, cache_c_kidx, cache_d_k, cache_d_v, w_ada, b_ada, g_mix, g_ffn, g_final, rel_bias, a_w_in, a_g_q, a_w_uq, a_g_kv, a_w_ukv, a_w_o, b_w_in, b_b_f, b_w_o, c_w_in, c_w_o, d_w_in, d_sinks, d_w_o, moe_w_router, moe_b_router, moe_w_gate, moe_w_up, moe_w_down):
    bp, S, d = x_prompt.shape
    bs, Ls, _ = x_sample.shape
    P = cache_a_latent.shape[1]
    depth = w_ada.shape[0]
    hd = N_HEADS * HEAD_DIM
    assert S % PROJ_ROWS == 0 and Ls == CHUNK and P % DSA_TQ == 0
    assert DSA_SEARCH_ROWS % DSA_TK == 0 and S % DSA_SEARCH_ROWS == 0
    assert S % FLASH_T_TILE == 0 and FLASH_T_TILE % CHUNK == 0 and S % FOX_PROJ_ROWS == 0
    assert DSA_SEARCH_ROWS % SAMPLE_TK == 0
    tm_p, tm_s = PROJ_ROWS, Ls
    lk_s = _round_up(P + Ls, DSA_SEARCH_ROWS)

    w = _prep_weights(a_w_in, a_g_q, a_w_uq, a_g_kv, a_w_ukv, a_w_o, b_w_in, b_b_f, b_w_o,
                      c_w_in, c_w_o, d_w_in, d_sinks, d_w_o, moe_w_router, moe_b_router,
                      moe_w_gate, moe_w_up, moe_w_down)
    mod = _adaln(jnp.concatenate([c_prompt, c_sample], axis=0), w_ada, b_ada)
    farq, nb_dsa, nb_swa = _bias_tiles(rel_bias)

    def mods(i, lo, hi):
        return [mod[i, lo:hi, k * d:(k + 1) * d][:, None, :] for k in range(6)]

    xp, xs = x_prompt, x_sample
    outs = {}
    for i in range(depth):
        sh1_p, sc1_p, gt1_p, sh2_p, sc2_p, gt2_p = mods(i, 0, bp)
        sh1_s, sc1_s, gt1_s, sh2_s, sc2_s, gt2_s = mods(i, bp, bp + bs)
        g1 = g_mix[i].reshape(1, d)
        mixer = i % 4
        if mixer == 0:
            cos_p, sin_p = _rope_tables(jnp.arange(S))
            cos_s, sin_s = _rope_tables(P + jnp.arange(Ls))
            qt_p, lat_p, kr_p = _mla_proj_t(xp, g1, sc1_p, sh1_p, w, cos_p, sin_p, tm_p)
            k_p, vt_p = _mla_expand_t(lat_p, kr_p, w, tm_p)
            op = _flash_t(qt_p, k_p, vt_p, per_frame=False)
            q_s, lat_s, kr_s = _mla_proj(xs, g1, sc1_s, sh1_s, w, cos_s, sin_s, tm_s)
            lat_all = _pad_rows(jnp.concatenate([cache_a_latent, lat_s], axis=1), lk_s)
            kr_all = _pad_rows(jnp.concatenate([cache_a_krope, kr_s], axis=1), lk_s)
            k_s, v_s = _mla_expand(lat_all, kr_all, w, DSA_TK)
            os_ = _flash(q_s, k_s, v_s, tq=Ls, tk=SAMPLE_TK, qpos0=P, per_frame=False, wide=True)
            outs["a"] = (lat_p, kr_p, lat_s, kr_s)
            wo = w["a_wo"]
        elif mixer == 1:
            qt_p, k_p, v_p, vt_p, lf_p, lft_p = _fox_proj_t(xp, g1, sc1_p, sh1_p, w, FOX_PROJ_ROWS)
            cumt_p, kaug_p = _fox_cumaug(lft_p, lf_p, k_p, CUMSUM_TILE)
            op = _flash_t(qt_p, kaug_p, vt_p, per_frame=True, qd=cumt_p)
            q_s, k_s, v_s, kb_s, vb_s, lf_s = _fox_proj(xs, g1, sc1_s, sh1_s, w, tm_s)
            lf_all = jnp.concatenate([cache_b_logf.astype(F32), lf_s], axis=1)
            cum_s = _cumsum_rows(_pad_rows(lf_all, lk_s).swapaxes(1, 2), CUMSUM_TILE)
            kb_all = _pad_rows(jnp.concatenate([cache_b_k.reshape(bs, P, hd).astype(BF16), kb_s], axis=1), lk_s)
            vb_all = _pad_rows(jnp.concatenate([cache_b_v.reshape(bs, P, hd).astype(BF16), vb_s], axis=1), lk_s)
            os_ = _flash(q_s, kb_all, vb_all, tq=Ls, tk=SAMPLE_TK, qpos0=P, per_frame=True, wide=False,
                         qd=cum_s[:, :, P:P + Ls, None], kd=cum_s[:, :, None, :])
            shp = lambda a: a.reshape(a.shape[0], a.shape[1], N_HEADS, HEAD_DIM)
            outs["b"] = (shp(k_p), shp(v_p), lf_p, shp(k_s), shp(v_s), lf_s)
            wo = w["b_wo"]
        elif mixer == 2:
            qt_p, qit_p, k_p, v_p, ki_p, wit_p, kaug_p, vt_p, kib_p = _dsa_proj(xp, g1, sc1_p, sh1_p, w, tm_p)
            op = _dsa_attention(qt_p, qit_p, wit_p, kaug_p, vt_p, kib_p, farq, nb_dsa, qpos0=0,
                                n_sel=min(C_TOPK_MAX, S // 4))
            qt_s, qit_s, k_s, v_s, ki_s, wit_s, kaug_s, vt_s, kib_s = _dsa_proj(xs, g1, sc1_s, sh1_s, w, tm_s)
            pad_q = lambda a: jnp.pad(a, ((0, 0), (0, 0), (0, DSA_TQ - Ls)))
            ones_cols = jnp.zeros((bs, P, LANES - HEAD_DIM), BF16).at[:, :, :2].set(1.0)
            kaug_c = jnp.concatenate([cache_c_k.astype(BF16), ones_cols], axis=2)
            kaug_all = _pad_rows(jnp.concatenate([kaug_c, kaug_s], axis=1), lk_s)
            ones_row = jnp.zeros((bs, V_ROWS - HEAD_DIM, P), BF16).at[:, 0, :].set(1.0)
            vt_c = jnp.concatenate([jnp.swapaxes(cache_c_v, 1, 2).astype(BF16), ones_row], axis=1)
            vt_all = jnp.pad(jnp.concatenate([vt_c, vt_s], axis=2), ((0, 0), (0, 0), (0, lk_s - P - Ls)))
            ki_all = _pad_rows(jnp.concatenate([cache_c_kidx.astype(BF16), kib_s], axis=1), lk_s)
            os_ = _dsa_attention(pad_q(qt_s), pad_q(qit_s), pad_q(wit_s), kaug_all, vt_all, ki_all,
                                 farq, nb_dsa, qpos0=P, n_sel=min(C_TOPK_MAX, (P + Ls) // 4))[:, :Ls]
            outs["c"] = (k_p, v_p, ki_p, k_s, v_s, ki_s)
            wo = w["c_wo"]
        else:
            kvw = D_KV_HEADS * HEAD_DIM
            q_p, kv_p, kvb_p = _swa_proj(xp, g1, sc1_p, sh1_p, w, tm_p)
            op = _swa_attention(q_p, jnp.pad(kvb_p, ((0, 0), (WINDOW, 0), (0, 0))), nb_swa, w["d_sinks"],
                                mask_front=True)
            q_s, kv_s, kvb_s = _swa_proj(xs, g1, sc1_s, sh1_s, w, tm_s)
            wc = cache_d_k.shape[1]
            ck = cache_d_k.reshape(bs, wc, kvw)
            cv = cache_d_v.reshape(bs, wc, kvw)
            kv_cache = jnp.concatenate([ck, cv], axis=-1)
            kv_all = jnp.concatenate([kv_cache.astype(BF16), kvb_s], axis=1)
            kv_all = jnp.pad(kv_all, ((0, 0), (WINDOW - wc, 0), (0, 0)))
            os_ = _swa_attention(q_s, kv_all, nb_swa, w["d_sinks"], mask_front=(wc < WINDOW))
            keep = min(WINDOW, S)
            shp = lambda a: a.reshape(a.shape[0], a.shape[1], D_KV_HEADS, HEAD_DIM)
            k_roll = jnp.concatenate([ck, kv_s[..., :kvw]], axis=1)[:, Ls:]
            v_roll = jnp.concatenate([cv, kv_s[..., kvw:]], axis=1)[:, Ls:]
            outs["d"] = (shp(kv_p[:, S - keep:, :kvw]), shp(kv_p[:, S - keep:, kvw:]), shp(k_roll), shp(v_roll))
            wo = w["d_wo"]

        g2 = g_ffn[i].reshape(1, d)
        x1_p, h2_p, gates_p = _post_mix(op, wo, xp, gt1_p, g2, sc2_p, sh2_p, w["wr_t"], w["br"], tm_p,
                                        o_transposed=(mixer < 2))
        xp = _moe(x1_p, h2_p, jnp.swapaxes(gates_p, 1, 2), gt2_p, w["wgu"][i], w["wd"][i], tm_p)
        x1_s, h2_s, gates_s = _post_mix(os_, wo, xs, gt1_s, g2, sc2_s, sh2_s, w["wr_t"], w["br"], tm_s)
        xs = _moe(x1_s, h2_s, jnp.swapaxes(gates_s, 1, 2), gt2_s, w["wgu"][i], w["wd"][i], tm_s)

    gf = g_final.reshape(1, d)
    y_p = _final_norm(xp, gf, tm_p)
    y_s = _final_norm(xs, gf, tm_s)
    a_lat_p, a_kr_p, a_lat_s, a_kr_s = outs["a"]
    b_k_p, b_v_p, b_lf_p, b_k_s, b_v_s, b_lf_s = outs["b"]
    c_k_p, c_v_p, c_ki_p, c_k_s, c_v_s, c_ki_s = outs["c"]
    d_k_p, d_v_p, d_k_s, d_v_s = outs["d"]
    return (y_p, y_s,
            a_lat_p, a_kr_p, b_k_p, b_v_p, b_lf_p, c_k_p, c_v_p, c_ki_p, d_k_p, d_v_p,
            a_lat_s, a_kr_s, b_k_s, b_v_s, b_lf_s, c_k_s, c_v_s, c_ki_s, d_k_s, d_v_s)
```

```python
import functools
import math

import jax
import jax.numpy as jnp
import numpy as np
from jax import lax
from jax.experimental import pallas as pl
from jax.experimental.pallas import tpu as pltpu

F32 = jnp.float32
BF16 = jnp.bfloat16
I32 = jnp.int32

CHUNK = 64
CHUNK_SHIFT = CHUNK.bit_length() - 1
NORM_EPS = 1e-6
NEG_INF = -1e30
LOG2E = math.log2(math.e)
N_HEADS = 16
HEAD_DIM = 64
HEAD_SCALE = HEAD_DIM ** -0.5
N_BUCKETS = 32
MAX_DISTANCE = 128
A_Q_LORA = 512
A_KV_LORA = 256
A_NOPE = 64
A_ROPE = 32
A_V = 64
A_SCALE = (A_NOPE + A_ROPE) ** -0.5
ROPE_THETA = 10000.0
C_IDX_HEADS = 8
C_IDX_DIM = 64
C_TOPK_MAX = 256
D_KV_HEADS = 2
D_REP = N_HEADS // D_KV_HEADS
WINDOW = 128
N_WIN_CHUNKS = WINDOW // CHUNK
N_EXPERTS = 16
N_GROUPS = 4
EXPERTS_PER_GROUP = N_EXPERTS // N_GROUPS
D_EXPERT = 256

LANES = 128
VMEM_LIMIT_BYTES = 56 * 1024 * 1024

PROJ_ROWS = 512
FLASH_TQ = 256
FLASH_TK = 512
DSA_TQ = 128
DSA_TK = 256
DSA_SEARCH_ROWS = 512
V_ROWS = HEAD_DIM + 16
CUMSUM_TILE = 256
FLASH_T_TILE = 512
FOX_PROJ_ROWS = 256
SAMPLE_TK = 512
MOE_EXPERTS_PER_STEP = 8

_NT = (((1,), (1,)), ((), ()))
_NN = (((1,), (0,)), ((), ()))
_TN = (((0,), (0,)), ((), ()))


def _params(*sem):
    return pltpu.CompilerParams(dimension_semantics=sem, vmem_limit_bytes=VMEM_LIMIT_BYTES)


def _split2(a):
    hi = a.astype(BF16)
    lo = (a - hi.astype(F32)).astype(BF16)
    return hi, lo


def _split3(c):
    hi = c.astype(BF16).astype(F32)
    r1 = c - hi
    mid = r1.astype(BF16).astype(F32)
    lo = (r1 - mid).astype(BF16).astype(F32)
    return hi, mid, lo


def _dot3(a, b, dims):
    ah, al = _split2(a)
    bh, bl = _split2(b)
    d = lambda x, y: lax.dot_general(x, y, dims, preferred_element_type=F32)
    return d(ah, bh) + (d(ah, bl) + d(al, bh))


def _rms(x):
    return x * lax.rsqrt(jnp.mean(x * x, axis=-1, keepdims=True) + NORM_EPS)


def _modnorm(x, g, sc, sh):
    return _rms(x) * g * (1.0 + sc) + sh


def _sigmoid(z):
    return 1.0 / (1.0 + jnp.exp(-z))


def _log_sigmoid(z):
    return jnp.minimum(z, 0.0) - jnp.log1p(jnp.exp(-jnp.abs(z)))


def _adaln_kernel(c_ref, w_ref, b_ref, o_ref):
    c = c_ref[...]
    s = c * _sigmoid(c)
    o_ref[0] = _dot3(s, w_ref[0], _NN) + b_ref[0]


def _adaln(c_all, w_ada, b_ada):
    depth, d, d6 = w_ada.shape
    bc = c_all.shape[0]
    return pl.pallas_call(
        _adaln_kernel,
        grid=(depth, d6 // d),
        in_specs=[pl.BlockSpec((bc, d), lambda i, j: (0, 0)),
                  pl.BlockSpec((1, d, d), lambda i, j: (i, 0, j)),
                  pl.BlockSpec((1, 1, d), lambda i, j: (i, 0, j))],
        out_specs=pl.BlockSpec((1, bc, d), lambda i, j: (i, 0, j)),
        out_shape=jax.ShapeDtypeStruct((depth, bc, d6), F32),
        compiler_params=_params("parallel", "parallel"),
        name="adaln",
    )(c_all, w_ada, b_ada.reshape(depth, 1, d6))


_FAR_THRESHOLDS = (12, 16, 23, 32, 46, 64, 91)
_FAR_BUCKET = N_BUCKETS // 2 - 1
FAR_HI_ROW = HEAD_DIM
FAR_LO_ROW = HEAD_DIM + 1


def _rel_bucket(rel):
    n = jnp.abs(rel)
    nb = N_BUCKETS // 2
    max_exact = nb // 2
    far = jnp.full(rel.shape, max_exact, I32)
    for t in _FAR_THRESHOLDS:
        far = far + (n >= t).astype(I32)
    return jnp.where(rel > 0, nb, 0) + jnp.where(n < max_exact, n, far)


def _bias_kernel(tab_ref, farq_ref, dsa_ref, swa_ref):
    def lookup(bucket, h):
        acc = jnp.zeros(bucket.shape, F32)
        for b in range(N_BUCKETS):
            acc = jnp.where(bucket == b, tab_ref[b, h], acc)
        return acc

    kr = lax.broadcasted_iota(I32, (DSA_TQ, DSA_TQ), 0)
    qc = lax.broadcasted_iota(I32, (DSA_TQ, DSA_TQ), 1)
    for t, off in enumerate((-2 * DSA_TQ, -DSA_TQ, 0)):
        bucket = _rel_bucket(kr - qc + off)
        for h in range(N_HEADS):
            far = tab_ref[_FAR_BUCKET, h]
            dsa_ref[t, :, h * DSA_TQ:(h + 1) * DSA_TQ] = (lookup(bucket, h) - far) * LOG2E
    row = lax.broadcasted_iota(I32, (LANES, DSA_TQ), 0)
    for h in range(N_HEADS):
        c = jnp.full((LANES, DSA_TQ), tab_ref[_FAR_BUCKET, h] * LOG2E, F32)
        hi = c.astype(BF16).astype(F32)
        lo = (c - hi).astype(BF16).astype(F32)
        blk = jnp.where(row == FAR_HI_ROW, hi, jnp.where(row == FAR_LO_ROW, lo, 0.0))
        farq_ref[:, h * DSA_TQ:(h + 1) * DSA_TQ] = blk.astype(BF16)
    qr = lax.broadcasted_iota(I32, (CHUNK, WINDOW + CHUNK), 0)
    kc = lax.broadcasted_iota(I32, (CHUNK, WINDOW + CHUNK), 1)
    bucket = _rel_bucket(kc - qr - WINDOW)
    for h in range(N_HEADS):
        swa_ref[h] = lookup(bucket, h)


def _bias_tiles(rel_bias):
    vm = pl.BlockSpec(memory_space=pltpu.VMEM)
    return pl.pallas_call(
        _bias_kernel,
        in_specs=[pl.BlockSpec(memory_space=pltpu.SMEM)],
        out_specs=[vm, vm, vm],
        out_shape=[jax.ShapeDtypeStruct((LANES, N_HEADS * DSA_TQ), BF16),
                   jax.ShapeDtypeStruct((3, DSA_TQ, N_HEADS * DSA_TQ), F32),
                   jax.ShapeDtypeStruct((N_HEADS, CHUNK, WINDOW + CHUNK), F32)],
        compiler_params=pltpu.CompilerParams(vmem_limit_bytes=VMEM_LIMIT_BYTES),
        name="bias_tiles",
    )(rel_bias)


def _mla_in(x_ref, g_ref, sc_ref, sh_ref, win_ref, gq_ref, gkv_ref, cos_ref, sin_ref, lat_ref, kr_ref):
    h = _modnorm(x_ref[0], g_ref[...], sc_ref[0], sh_ref[0]).astype(BF16)
    hw = jnp.dot(h, win_ref[...], preferred_element_type=F32)
    o1 = A_Q_LORA
    o2 = o1 + A_KV_LORA
    o3 = o2 + A_ROPE
    lat_ref[0] = _rms(hw[:, o1:o2]) * gkv_ref[...]
    cosr = cos_ref[...][:, A_NOPE:A_NOPE + A_ROPE]
    sinr = sin_ref[...][:, A_NOPE:A_NOPE + A_ROPE]
    kr_ref[0] = hw[:, o2:o3] * cosr + hw[:, o3:o3 + A_ROPE] * sinr
    return (_rms(hw[:, :o1]) * gq_ref[...]).astype(BF16)


def _mla_proj_kernel(x_ref, g_ref, sc_ref, sh_ref, win_ref, gq_ref, gkv_ref, wq_ref, wqs_ref,
                     cos_ref, sin_ref, q_ref, lat_ref, kr_ref):
    cqn = _mla_in(x_ref, g_ref, sc_ref, sh_ref, win_ref, gq_ref, gkv_ref, cos_ref, sin_ref, lat_ref, kr_ref)
    cosp = cos_ref[...]
    sinp = sin_ref[...]
    a = jnp.dot(cqn, wq_ref[...], preferred_element_type=F32)
    b = jnp.dot(cqn, wqs_ref[...], preferred_element_type=F32)
    for hd in range(N_HEADS):
        sl = slice(hd * LANES, (hd + 1) * LANES)
        q_ref[0, :, sl] = ((a[:, sl] * cosp + b[:, sl] * sinp) * A_SCALE).astype(BF16)


def _mla_proj(x, g, sc, sh, w, cosp, sinp, tm):
    bsz, L, d = x.shape
    row = lambda b, i: (b, i, 0)
    per_b = lambda b, i: (b, 0, 0)
    const = lambda b, i: (0, 0)
    nin = w["a_in"].shape[1]
    return pl.pallas_call(
        _mla_proj_kernel,
        grid=(bsz, L // tm),
        in_specs=[pl.BlockSpec((1, tm, d), row),
                  pl.BlockSpec((1, d), const),
                  pl.BlockSpec((1, 1, d), per_b),
                  pl.BlockSpec((1, 1, d), per_b),
                  pl.BlockSpec((d, nin), const),
                  pl.BlockSpec((1, A_Q_LORA), const),
                  pl.BlockSpec((1, A_KV_LORA), const),
                  pl.BlockSpec((A_Q_LORA, N_HEADS * LANES), const),
                  pl.BlockSpec((A_Q_LORA, N_HEADS * LANES), const),
                  pl.BlockSpec((tm, LANES), lambda b, i: (i, 0)),
                  pl.BlockSpec((tm, LANES), lambda b, i: (i, 0))],
        out_specs=[pl.BlockSpec((1, tm, N_HEADS * LANES), row),
                   pl.BlockSpec((1, tm, A_KV_LORA), row),
                   pl.BlockSpec((1, tm, A_ROPE), row)],
        out_shape=[jax.ShapeDtypeStruct((bsz, L, N_HEADS * LANES), BF16),
                   jax.ShapeDtypeStruct((bsz, L, A_KV_LORA), F32),
                   jax.ShapeDtypeStruct((bsz, L, A_ROPE), F32)],
        compiler_params=_params("parallel", "parallel"),
        name="mla_proj",
    )(x, g, sc, sh, w["a_in"], w["a_gq"], w["a_gkv"], w["a_wq"], w["a_wqs"], cosp, sinp)


def _mla_expand_kernel(lat_ref, kr_ref, wk_ref, sel_ref, wv_ref, k_ref, v_ref):
    lat = lat_ref[0].astype(BF16)
    kr = kr_ref[0].astype(BF16)
    k = (jnp.dot(lat, wk_ref[...], preferred_element_type=F32)
         + jnp.dot(kr, sel_ref[...], preferred_element_type=F32))
    k_ref[0] = k.astype(BF16)
    v_ref[0] = jnp.dot(lat, wv_ref[...], preferred_element_type=F32).astype(BF16)


def _mla_expand(lat, kr, w, tm):
    bsz, L, _ = lat.shape
    row = lambda b, i: (b, i, 0)
    const = lambda b, i: (0, 0)
    return pl.pallas_call(
        _mla_expand_kernel,
        grid=(bsz, L // tm),
        in_specs=[pl.BlockSpec((1, tm, A_KV_LORA), row),
                  pl.BlockSpec((1, tm, A_ROPE), row),
                  pl.BlockSpec((A_KV_LORA, N_HEADS * LANES), const),
                  pl.BlockSpec((A_ROPE, N_HEADS * LANES), const),
                  pl.BlockSpec((A_KV_LORA, N_HEADS * A_V), const)],
        out_specs=[pl.BlockSpec((1, tm, N_HEADS * LANES), row),
                   pl.BlockSpec((1, tm, N_HEADS * A_V), row)],
        out_shape=[jax.ShapeDtypeStruct((bsz, L, N_HEADS * LANES), BF16),
                   jax.ShapeDtypeStruct((bsz, L, N_HEADS * A_V), BF16)],
        compiler_params=_params("parallel", "parallel"),
        name="mla_expand",
    )(lat, kr, w["a_wk"], w["a_sel"], w["a_wv"])


def _mla_proj_t_kernel(x_ref, g_ref, sc_ref, sh_ref, win_ref, gq_ref, gkv_ref, wqt_ref, wqst_ref,
                       cos_ref, sin_ref, cost_ref, sint_ref, qt_ref, lat_ref, kr_ref):
    cqn = _mla_in(x_ref, g_ref, sc_ref, sh_ref, win_ref, gq_ref, gkv_ref, cos_ref, sin_ref, lat_ref, kr_ref)
    a = lax.dot_general(wqt_ref[...], cqn, _NT, preferred_element_type=F32)
    b = lax.dot_general(wqst_ref[...], cqn, _NT, preferred_element_type=F32)
    cost = cost_ref[...]
    sint = sint_ref[...]
    for hd in range(N_HEADS):
        sl = slice(hd * LANES, (hd + 1) * LANES)
        qt_ref[0, sl, :] = ((a[sl] * cost + b[sl] * sint) * (A_SCALE * LOG2E)).astype(BF16)


def _mla_proj_t(x, g, sc, sh, w, cosp, sinp, tm):
    bsz, L, d = x.shape
    row = lambda b, i: (b, i, 0)
    per_b = lambda b, i: (b, 0, 0)
    const = lambda b, i: (0, 0)
    nin = w["a_in"].shape[1]
    hq = N_HEADS * LANES
    return pl.pallas_call(
        _mla_proj_t_kernel,
        grid=(bsz, L // tm),
        in_specs=[pl.BlockSpec((1, tm, d), row),
                  pl.BlockSpec((1, d), const),
                  pl.BlockSpec((1, 1, d), per_b),
                  pl.BlockSpec((1, 1, d), per_b),
                  pl.BlockSpec((d, nin), const),
                  pl.BlockSpec((1, A_Q_LORA), const),
                  pl.BlockSpec((1, A_KV_LORA), const),
                  pl.BlockSpec((hq, A_Q_LORA), const),
                  pl.BlockSpec((hq, A_Q_LORA), const),
                  pl.BlockSpec((tm, LANES), lambda b, i: (i, 0)),
                  pl.BlockSpec((tm, LANES), lambda b, i: (i, 0)),
                  pl.BlockSpec((LANES, tm), lambda b, i: (0, i)),
                  pl.BlockSpec((LANES, tm), lambda b, i: (0, i))],
        out_specs=[pl.BlockSpec((1, hq, tm), lambda b, i: (b, 0, i)),
                   pl.BlockSpec((1, tm, A_KV_LORA), row),
                   pl.BlockSpec((1, tm, A_ROPE), row)],
        out_shape=[jax.ShapeDtypeStruct((bsz, hq, L), BF16),
                   jax.ShapeDtypeStruct((bsz, L, A_KV_LORA), F32),
                   jax.ShapeDtypeStruct((bsz, L, A_ROPE), F32)],
        compiler_params=_params("parallel", "parallel"),
        name="mla_proj_t",
    )(x, g, sc, sh, w["a_in"], w["a_gq"], w["a_gkv"], w["a_wq"].T, w["a_wqs"].T, cosp, sinp, cosp.T, sinp.T)


def _mla_expand_t_kernel(lat_ref, kr_ref, wk_ref, sel_ref, wvt_ref, ones_ref, k_ref, vt_ref):
    lat = lat_ref[0].astype(BF16)
    kr = kr_ref[0].astype(BF16)
    k = (jnp.dot(lat, wk_ref[...], preferred_element_type=F32)
         + jnp.dot(kr, sel_ref[...], preferred_element_type=F32))
    k_ref[0] = k.astype(BF16)
    vt = lax.dot_general(wvt_ref[...], lat, _NT, preferred_element_type=F32) + ones_ref[...]
    vt_ref[0] = vt.astype(BF16)


def _mla_expand_t(lat, kr, w, tm):
    bsz, L, _ = lat.shape
    row = lambda b, i: (b, i, 0)
    const = lambda b, i: (0, 0)
    hv = N_HEADS * V_ROWS
    return pl.pallas_call(
        _mla_expand_t_kernel,
        grid=(bsz, L // tm),
        in_specs=[pl.BlockSpec((1, tm, A_KV_LORA), row),
                  pl.BlockSpec((1, tm, A_ROPE), row),
                  pl.BlockSpec((A_KV_LORA, N_HEADS * LANES), const),
                  pl.BlockSpec((A_ROPE, N_HEADS * LANES), const),
                  pl.BlockSpec((hv, A_KV_LORA), const),
                  pl.BlockSpec((hv, 1), const)],
        out_specs=[pl.BlockSpec((1, tm, N_HEADS * LANES), row),
                   pl.BlockSpec((1, hv, tm), lambda b, i: (b, 0, i))],
        out_shape=[jax.ShapeDtypeStruct((bsz, L, N_HEADS * LANES), BF16),
                   jax.ShapeDtypeStruct((bsz, hv, L), BF16)],
        compiler_params=_params("parallel", "parallel"),
        name="mla_expand_t",
    )(lat, kr, w["a_wk"], w["a_sel"], w["a_wvt"], w["ones_rows"])


def _flash_kernel(*refs, tq, tk, qpos0, per_frame, wide, decay, nk_tiles):
    if decay:
        q_ref, k_ref, v_ref, qd_ref, kd_ref, o_ref, m_sc, l_sc, acc_sc = refs
    else:
        q_ref, k_ref, v_ref, o_ref, m_sc, l_sc, acc_sc = refs
    i = pl.program_id(2)
    q0 = qpos0 + i * tq
    if per_frame:
        vis_all = q0 + 1
        vis_any = q0 + tq
    else:
        vis_all = ((q0 >> CHUNK_SHIFT) + 1) * CHUNK
        vis_any = (((q0 + tq - 1) >> CHUNK_SHIFT) + 1) * CHUNK
    n_full = jnp.minimum(vis_all // tk, nk_tiles)
    n_tot = jnp.minimum((vis_any + tk - 1) // tk, nk_tiles)

    q = q_ref[0]
    lane = lax.broadcasted_iota(I32, (1, LANES), 1)
    first = lane < HEAD_DIM
    if wide:
        qs = (q[:, :LANES], q[:, LANES:])
    else:
        zero = jnp.zeros_like(q)
        qs = (jnp.where(first, q, zero), jnp.where(first, zero, q))

    m_sc[...] = jnp.full(m_sc.shape, NEG_INF, F32)
    l_sc[...] = jnp.zeros(l_sc.shape, F32)
    acc_sc[...] = jnp.zeros(acc_sc.shape, F32)
    qp = q0 + lax.broadcasted_iota(I32, (tq, 1), 0)

    def tile(j, masked):
        ks = pl.multiple_of(j * tk, tk)
        kt = k_ref[0, pl.ds(ks, tk), :]
        vt = v_ref[0, pl.ds(ks, tk), :]
        if masked:
            kp = ks + lax.broadcasted_iota(I32, (1, tk), 1)
            if per_frame:
                mask = kp <= qp
            else:
                mask = (kp >> CHUNK_SHIFT) <= (qp >> CHUNK_SHIFT)
        for hh in range(2):
            kk = kt[:, hh * LANES:(hh + 1) * LANES] if wide else kt
            s = lax.dot_general(qs[hh], kk, _NT, preferred_element_type=F32)
            if decay:
                s = s + (qd_ref[0, hh] - kd_ref[0, hh, :, pl.ds(ks, tk)])
            if masked:
                s = jnp.where(mask, s, NEG_INF)
            m_old = m_sc[hh]
            m_new = jnp.maximum(m_old, jnp.max(s, axis=1, keepdims=True))
            p = jnp.exp(s - m_new)
            alpha = jnp.exp(m_old - m_new)
            l_sc[hh] = alpha * l_sc[hh] + jnp.sum(p, axis=1, keepdims=True)
            acc_sc[hh] = alpha * acc_sc[hh] + jnp.dot(p.astype(BF16), vt, preferred_element_type=F32)
            m_sc[hh] = m_new

    def full_body(j, c):
        tile(j, False)
        return c

    def diag_body(j, c):
        tile(j, True)
        return c

    lax.fori_loop(0, n_full, full_body, 0)
    lax.fori_loop(n_full, n_tot, diag_body, 0)
    o0 = acc_sc[0] / l_sc[0]
    o1 = acc_sc[1] / l_sc[1]
    o_ref[0] = jnp.where(first, o0, o1).astype(BF16)


def _flash(q, k, v, *, tq, tk, qpos0, per_frame, wide, qd=None, kd=None):
    bsz, lq, _ = q.shape
    lk = k.shape[1]
    qw = 2 * LANES if wide else LANES
    decay = qd is not None
    in_specs = [pl.BlockSpec((1, tq, qw), lambda b, hp, i: (b, i, hp)),
                pl.BlockSpec((1, lk, qw), lambda b, hp, i: (b, 0, hp)),
                pl.BlockSpec((1, lk, LANES), lambda b, hp, i: (b, 0, hp))]
    args = [q, k, v]
    if decay:
        in_specs += [pl.BlockSpec((1, 2, tq, 1), lambda b, hp, i: (b, hp, i, 0)),
                     pl.BlockSpec((1, 2, 1, lk), lambda b, hp, i: (b, hp, 0, 0))]
        args += [qd, kd]
    kern = functools.partial(_flash_kernel, tq=tq, tk=tk, qpos0=qpos0, per_frame=per_frame,
                             wide=wide, decay=decay, nk_tiles=lk // tk)
    return pl.pallas_call(
        kern,
        grid=(bsz, N_HEADS // 2, lq // tq),
        in_specs=in_specs,
        out_specs=pl.BlockSpec((1, tq, LANES), lambda b, hp, i: (b, i, hp)),
        out_shape=jax.ShapeDtypeStruct((bsz, lq, N_HEADS * HEAD_DIM), BF16),
        scratch_shapes=[pltpu.VMEM((2, tq, 1), F32), pltpu.VMEM((2, tq, 1), F32),
                        pltpu.VMEM((2, tq, LANES), F32)],
        compiler_params=_params("parallel", "parallel", "arbitrary"),
        name="flash_attention",
    )(*args)


DECAY_K_ROW = HEAD_DIM
DECAY_Q_ROW = HEAD_DIM + 3


def _flash_t_kernel(*refs, tile, per_frame, decay):
    if decay:
        qt_ref, k_ref, vt_ref, qd_ref, o_ref, m_sc, acc_sc, p_sc, alpha_sc = refs
    else:
        qt_ref, k_ref, vt_ref, o_ref, m_sc, acc_sc, p_sc, alpha_sc = refs
    i = pl.program_id(2)
    q0 = i * tile
    qt = qt_ref[0]
    qs = []
    for hh in range(2):
        blk = qt[hh * LANES:(hh + 1) * LANES]
        if decay:
            hi, mid, lo = _split3(qd_ref[0, 0, hh:hh + 1, :])
            row = lax.broadcasted_iota(I32, (LANES, tile), 0)
            aug = jnp.where(row == DECAY_Q_ROW, hi, jnp.where(row == DECAY_Q_ROW + 1, mid,
                            jnp.where(row == DECAY_Q_ROW + 2, lo, 0.0)))
            aug = jnp.where((row >= DECAY_K_ROW) & (row < DECAY_Q_ROW), 1.0, aug)
            blk = (blk.astype(F32) + aug).astype(BF16)
        qs.append(blk)
    m_sc[...] = jnp.full(m_sc.shape, NEG_INF, F32)
    acc_sc[...] = jnp.zeros(acc_sc.shape, F32)
    qp = q0 + lax.broadcasted_iota(I32, (1, tile), 1)

    def scores(ks, masked):
        kt = k_ref[0, pl.ds(ks, tile), :]
        ss = [jnp.dot(kt[:, hh * LANES:(hh + 1) * LANES], qs[hh], preferred_element_type=F32) for hh in range(2)]
        if masked:
            kp = ks + lax.broadcasted_iota(I32, (tile, 1), 0)
            mask = (kp <= qp) if per_frame else ((kp >> CHUNK_SHIFT) <= (qp >> CHUNK_SHIFT))
            ss = [jnp.where(mask, s, NEG_INF) for s in ss]
        return ss

    def softmax(ss):
        m_old = m_sc[...]
        m_new = [jnp.maximum(m_old[hh:hh + 1], jnp.max(ss[hh], axis=0, keepdims=True)) for hh in range(2)]
        m_sc[...] = jnp.concatenate(m_new, axis=0)
        for hh in range(2):
            p_sc[hh] = jnp.exp2(ss[hh] - m_new[hh]).astype(BF16)
        alpha_sc[...] = jnp.concatenate([jnp.exp2(m_old[hh:hh + 1] - m_new[hh]) for hh in range(2)], axis=0)

    def accumulate(jt):
        vt = vt_ref[0, :, pl.ds(pl.multiple_of(jt * tile, tile), tile)]
        alpha = alpha_sc[...]
        for hh in range(2):
            pv = jnp.dot(vt[hh * V_ROWS:(hh + 1) * V_ROWS], p_sc[hh], preferred_element_type=F32)
            acc_sc[hh] = alpha[hh:hh + 1] * acc_sc[hh] + pv

    softmax(scores(pl.multiple_of(i * tile, tile), True))

    def body(j, j_prev):
        ss = scores(pl.multiple_of(j * tile, tile), False)
        accumulate(j_prev)
        softmax(ss)
        return j

    accumulate(lax.fori_loop(0, i, body, i))
    for hh in range(2):
        a = acc_sc[hh]
        o_ref[0, hh * HEAD_DIM:(hh + 1) * HEAD_DIM, :] = (a[:HEAD_DIM] / a[HEAD_DIM:HEAD_DIM + 1]).astype(BF16)


def _flash_t(qt, k, vt, *, per_frame, qd=None):
    bsz, _, L = qt.shape
    tile = FLASH_T_TILE
    decay = qd is not None
    in_specs = [pl.BlockSpec((1, 2 * LANES, tile), lambda b, hp, i: (b, hp, i)),
                pl.BlockSpec((1, L, 2 * LANES), lambda b, hp, i: (b, 0, hp)),
                pl.BlockSpec((1, 2 * V_ROWS, L), lambda b, hp, i: (b, hp, 0))]
    args = [qt, k, vt]
    if decay:
        in_specs.append(pl.BlockSpec((1, 1, 2, tile), lambda b, hp, i: (b, hp, 0, i)))
        args.append(qd.reshape(bsz, N_HEADS // 2, 2, L))
    return pl.pallas_call(
        functools.partial(_flash_t_kernel, tile=tile, per_frame=per_frame, decay=decay),
        grid=(bsz, N_HEADS // 2, L // tile),
        in_specs=in_specs,
        out_specs=pl.BlockSpec((1, 2 * HEAD_DIM, tile), lambda b, hp, i: (b, hp, i)),
        out_shape=jax.ShapeDtypeStruct((bsz, N_HEADS * HEAD_DIM, L), BF16),
        scratch_shapes=[pltpu.VMEM((2, tile), F32), pltpu.VMEM((2, V_ROWS, tile), F32),
                        pltpu.VMEM((2, tile, tile), BF16), pltpu.VMEM((2, tile), F32)],
        compiler_params=_params("parallel", "parallel", "arbitrary"),
        name="flash_attention_t",
    )(*args)


def _fox_proj_kernel(x_ref, g_ref, sc_ref, sh_ref, wq_ref, wk_ref, wv_ref, wf_ref, bf_ref,
                     q_ref, k_ref, v_ref, kb_ref, vb_ref, lf_ref):
    h = _modnorm(x_ref[0], g_ref[...], sc_ref[0], sh_ref[0]).astype(BF16)
    q_ref[0] = (jnp.dot(h, wq_ref[...], preferred_element_type=F32) * HEAD_SCALE).astype(BF16)
    k = jnp.dot(h, wk_ref[...], preferred_element_type=F32)
    k_ref[0] = k
    kb_ref[0] = k.astype(BF16)
    v = jnp.dot(h, wv_ref[...], preferred_element_type=F32)
    v_ref[0] = v
    vb_ref[0] = v.astype(BF16)
    f = jnp.dot(h, wf_ref[...], preferred_element_type=F32)[:, :N_HEADS] + bf_ref[...]
    lf_ref[0] = _log_sigmoid(f)


def _fox_proj(x, g, sc, sh, w, tm):
    bsz, L, d = x.shape
    hd = N_HEADS * HEAD_DIM
    row = lambda b, i: (b, i, 0)
    per_b = lambda b, i: (b, 0, 0)
    const = lambda b, i: (0, 0)
    return pl.pallas_call(
        _fox_proj_kernel,
        grid=(bsz, L // tm),
        in_specs=[pl.BlockSpec((1, tm, d), row),
                  pl.BlockSpec((1, d), const),
                  pl.BlockSpec((1, 1, d), per_b),
                  pl.BlockSpec((1, 1, d), per_b),
                  pl.BlockSpec((d, hd), const),
                  pl.BlockSpec((d, hd), const),
                  pl.BlockSpec((d, hd), const),
                  pl.BlockSpec((d, LANES), const),
                  pl.BlockSpec((1, N_HEADS), const)],
        out_specs=[pl.BlockSpec((1, tm, hd), row)] * 5 + [pl.BlockSpec((1, tm, N_HEADS), row)],
        out_shape=[jax.ShapeDtypeStruct((bsz, L, hd), BF16),
                   jax.ShapeDtypeStruct((bsz, L, hd), F32),
                   jax.ShapeDtypeStruct((bsz, L, hd), F32),
                   jax.ShapeDtypeStruct((bsz, L, hd), BF16),
                   jax.ShapeDtypeStruct((bsz, L, hd), BF16),
                   jax.ShapeDtypeStruct((bsz, L, N_HEADS), F32)],
        compiler_params=_params("parallel", "parallel"),
        name="fox_proj",
    )(x, g, sc, sh, w["b_wq"], w["b_wk"], w["b_wv"], w["b_wf"], w["b_bf"])


def _cumsum_kernel(x_ref, o_ref, carry_ref, *, tc):
    @pl.when(pl.program_id(1) == 0)
    def _():
        carry_ref[...] = jnp.zeros(carry_ref.shape, F32)

    r = lax.broadcasted_iota(I32, (tc, tc), 0)
    c = lax.broadcasted_iota(I32, (tc, tc), 1)
    upper = (r <= c).astype(BF16)
    h1, h2, h3 = [a.astype(BF16) for a in _split3(x_ref[0])]
    d = lambda a: jnp.dot(a, upper, preferred_element_type=F32)
    cum = ((d(h3) + d(h2)) + d(h1)) + carry_ref[...]
    o_ref[0] = cum
    carry_ref[...] = cum[:, tc - 1:tc]


def _cumsum_rows(x, tc):
    bsz, nh, L = x.shape
    return pl.pallas_call(
        functools.partial(_cumsum_kernel, tc=tc),
        grid=(bsz, L // tc),
        in_specs=[pl.BlockSpec((1, nh, tc), lambda b, j: (b, 0, j))],
        out_specs=pl.BlockSpec((1, nh, tc), lambda b, j: (b, 0, j)),
        out_shape=jax.ShapeDtypeStruct((bsz, nh, L), F32),
        scratch_shapes=[pltpu.VMEM((nh, 1), F32)],
        compiler_params=_params("parallel", "arbitrary"),
        name="cumsum",
    )(x)


def _fox_proj_t_kernel(x_ref, g_ref, sc_ref, sh_ref, wqt_ref, wk_ref, wv_ref, wvt_ref, ones_ref,
                       wf_ref, bf_ref, wft_ref, bft_ref, qt_ref, k_ref, v_ref, vt_ref, lf_ref, lft_ref):
    h = _modnorm(x_ref[0], g_ref[...], sc_ref[0], sh_ref[0]).astype(BF16)
    qt = lax.dot_general(wqt_ref[...], h, _NT, preferred_element_type=F32)
    qt_ref[0] = (qt * (HEAD_SCALE * LOG2E)).astype(BF16)
    k_ref[0] = jnp.dot(h, wk_ref[...], preferred_element_type=F32)
    v_ref[0] = jnp.dot(h, wv_ref[...], preferred_element_type=F32)
    vt = lax.dot_general(wvt_ref[...], h, _NT, preferred_element_type=F32) + ones_ref[...]
    vt_ref[0] = vt.astype(BF16)
    f = jnp.dot(h, wf_ref[...], preferred_element_type=F32)[:, :N_HEADS] + bf_ref[...]
    lf_ref[0] = _log_sigmoid(f)
    ft = lax.dot_general(wft_ref[...], h, _NT, preferred_element_type=F32)[:N_HEADS] + bft_ref[...]
    lft_ref[0] = _log_sigmoid(ft)


def _fox_proj_t(x, g, sc, sh, w, tm):
    bsz, L, d = x.shape
    hd = N_HEADS * HEAD_DIM
    hq = N_HEADS * LANES
    hv = N_HEADS * V_ROWS
    row = lambda b, i: (b, i, 0)
    col = lambda b, i: (b, 0, i)
    per_b = lambda b, i: (b, 0, 0)
    const = lambda b, i: (0, 0)
    return pl.pallas_call(
        _fox_proj_t_kernel,
        grid=(bsz, L // tm),
        in_specs=[pl.BlockSpec((1, tm, d), row),
                  pl.BlockSpec((1, d), const),
                  pl.BlockSpec((1, 1, d), per_b),
                  pl.BlockSpec((1, 1, d), per_b),
                  pl.BlockSpec((hq, d), const),
                  pl.BlockSpec((d, hd), const),
                  pl.BlockSpec((d, hd), const),
                  pl.BlockSpec((hv, d), const),
                  pl.BlockSpec((hv, 1), const),
                  pl.BlockSpec((d, LANES), const),
                  pl.BlockSpec((1, N_HEADS), const),
                  pl.BlockSpec((LANES, d), const),
                  pl.BlockSpec((N_HEADS, 1), const)],
        out_specs=[pl.BlockSpec((1, hq, tm), col), pl.BlockSpec((1, tm, hd), row), pl.BlockSpec((1, tm, hd), row),
                   pl.BlockSpec((1, hv, tm), col), pl.BlockSpec((1, tm, N_HEADS), row),
                   pl.BlockSpec((1, N_HEADS, tm), col)],
        out_shape=[jax.ShapeDtypeStruct((bsz, hq, L), BF16),
                   jax.ShapeDtypeStruct((bsz, L, hd), F32),
                   jax.ShapeDtypeStruct((bsz, L, hd), F32),
                   jax.ShapeDtypeStruct((bsz, hv, L), BF16),
                   jax.ShapeDtypeStruct((bsz, L, N_HEADS), F32),
                   jax.ShapeDtypeStruct((bsz, N_HEADS, L), F32)],
        compiler_params=_params("parallel", "parallel"),
        name="fox_proj_t",
    )(x, g, sc, sh, w["b_wqt"], w["b_wk"], w["b_wv"], w["b_wvt"], w["ones_rows"],
      w["b_wf"], w["b_bf"], w["b_wf"].T, w["b_bf"].T)


def _fox_cumaug_kernel(lft_ref, lf_ref, k_ref, cumt_ref, kaug_ref, crow_sc, ccol_sc, *, tc):
    @pl.when(pl.program_id(1) == 0)
    def _():
        crow_sc[...] = jnp.zeros(crow_sc.shape, F32)
        ccol_sc[...] = jnp.zeros(ccol_sc.shape, F32)

    r = lax.broadcasted_iota(I32, (tc, tc), 0)
    c = lax.broadcasted_iota(I32, (tc, tc), 1)
    upper = (r <= c).astype(BF16)
    lower = (c <= r).astype(BF16)
    xh, xm, xl = [a.astype(BF16) for a in _split3(lft_ref[0])]
    dr = lambda a: jnp.dot(a, upper, preferred_element_type=F32)
    cumt = ((dr(xl) + dr(xm)) + dr(xh)) + crow_sc[...]
    crow_sc[...] = cumt[:, tc - 1:tc]
    cumt_ref[0] = cumt * LOG2E
    yh, ym, yl = [a.astype(BF16) for a in _split3(lf_ref[0])]
    dc = lambda a: jnp.dot(lower, a, preferred_element_type=F32)
    cum = ((dc(yl) + dc(ym)) + dc(yh)) + ccol_sc[...]
    ccol_sc[...] = cum[tc - 1:tc, :]
    neg = cum * (-LOG2E)
    k = k_ref[0]
    lane = lax.broadcasted_iota(I32, (tc, LANES - HEAD_DIM), 1)
    for h in range(N_HEADS):
        hi, mid, lo = _split3(neg[:, h:h + 1])
        aug = jnp.where(lane == 0, hi, jnp.where(lane == 1, mid, jnp.where(lane == 2, lo, 0.0)))
        aug = jnp.where((lane >= DECAY_Q_ROW - HEAD_DIM) & (lane < DECAY_Q_ROW - HEAD_DIM + 3), 1.0, aug)
        kaug_ref[0, :, h * LANES:(h + 1) * LANES] = jnp.concatenate(
            [k[:, h * HEAD_DIM:(h + 1) * HEAD_DIM], aug], axis=1).astype(BF16)


def _fox_cumaug(lft, lf, k, tc):
    bsz, nh, L = lft.shape
    hd = k.shape[2]
    return pl.pallas_call(
        functools.partial(_fox_cumaug_kernel, tc=tc),
        grid=(bsz, L // tc),
        in_specs=[pl.BlockSpec((1, nh, tc), lambda b, j: (b, 0, j)),
                  pl.BlockSpec((1, tc, nh), lambda b, j: (b, j, 0)),
                  pl.BlockSpec((1, tc, hd), lambda b, j: (b, j, 0))],
        out_specs=[pl.BlockSpec((1, nh, tc), lambda b, j: (b, 0, j)),
                   pl.BlockSpec((1, tc, nh * LANES), lambda b, j: (b, j, 0))],
        out_shape=[jax.ShapeDtypeStruct((bsz, nh, L), F32),
                   jax.ShapeDtypeStruct((bsz, L, nh * LANES), BF16)],
        scratch_shapes=[pltpu.VMEM((nh, 1), F32), pltpu.VMEM((1, nh), F32)],
        compiler_params=_params("parallel", "arbitrary"),
        name="fox_cumsum_aug",
    )(lft, lf, k)


def _dsa_proj_kernel(x_ref, g_ref, sc_ref, sh_ref, wqt_ref, wqit_ref, wsm_ref, wsmt_ref,
                     qt_ref, qit_ref, k_ref, v_ref, ki_ref, wit_ref, kaug_ref, vt_ref, kib_ref):
    h = _modnorm(x_ref[0], g_ref[...], sc_ref[0], sh_ref[0]).astype(BF16)
    tm = h.shape[0]
    qt = lax.dot_general(wqt_ref[...], h, _NT, preferred_element_type=F32)
    qt_ref[0] = (qt * (HEAD_SCALE * LOG2E)).astype(BF16)
    qit = lax.dot_general(wqit_ref[...], h, _NT, preferred_element_type=F32)
    qit_ref[0] = (qit * (C_IDX_DIM ** -0.5)).astype(BF16)
    sm = jnp.dot(h, wsm_ref[...], preferred_element_type=F32)
    smt = lax.dot_general(wsmt_ref[...], h, _NT, preferred_element_type=F32)
    k = sm[:, :HEAD_DIM]
    ki = sm[:, 2 * HEAD_DIM:2 * HEAD_DIM + C_IDX_DIM]
    k_ref[0] = k
    v_ref[0] = sm[:, HEAD_DIM:2 * HEAD_DIM]
    ki_ref[0] = ki
    kib_ref[0] = ki.astype(BF16)
    lane = lax.broadcasted_iota(I32, (tm, LANES - HEAD_DIM), 1)
    ones_cols = jnp.where(lane < 2, 1.0, 0.0)
    kaug_ref[0] = jnp.concatenate([k, ones_cols], axis=1).astype(BF16)
    row = lax.broadcasted_iota(I32, (V_ROWS - HEAD_DIM, tm), 0)
    ones_row = jnp.where(row == 0, 1.0, 0.0)
    vt_ref[0] = jnp.concatenate([smt[HEAD_DIM:2 * HEAD_DIM], ones_row], axis=0).astype(BF16)
    o = 2 * HEAD_DIM + C_IDX_DIM
    wit_ref[0] = smt[o:o + C_IDX_HEADS] * (C_IDX_HEADS ** -0.5)


def _dsa_proj(x, g, sc, sh, w, tm):
    bsz, L, d = x.shape
    hi = C_IDX_HEADS * C_IDX_DIM
    hq = N_HEADS * LANES
    row = lambda b, i: (b, i, 0)
    col = lambda b, i: (b, 0, i)
    per_b = lambda b, i: (b, 0, 0)
    const = lambda b, i: (0, 0)
    small = lambda n, dt: jax.ShapeDtypeStruct((bsz, L, n), dt)
    tall = lambda n, dt: jax.ShapeDtypeStruct((bsz, n, L), dt)
    return pl.pallas_call(
        _dsa_proj_kernel,
        grid=(bsz, L // tm),
        in_specs=[pl.BlockSpec((1, tm, d), row),
                  pl.BlockSpec((1, d), const),
                  pl.BlockSpec((1, 1, d), per_b),
                  pl.BlockSpec((1, 1, d), per_b),
                  pl.BlockSpec((hq, d), const),
                  pl.BlockSpec((hi, d), const),
                  pl.BlockSpec((d, 2 * LANES), const),
                  pl.BlockSpec((2 * LANES, d), const)],
        out_specs=[pl.BlockSpec((1, hq, tm), col), pl.BlockSpec((1, hi, tm), col),
                   pl.BlockSpec((1, tm, HEAD_DIM), row), pl.BlockSpec((1, tm, HEAD_DIM), row),
                   pl.BlockSpec((1, tm, C_IDX_DIM), row), pl.BlockSpec((1, C_IDX_HEADS, tm), col),
                   pl.BlockSpec((1, tm, LANES), row), pl.BlockSpec((1, V_ROWS, tm), col),
                   pl.BlockSpec((1, tm, C_IDX_DIM), row)],
        out_shape=[tall(hq, BF16), tall(hi, BF16), small(HEAD_DIM, F32), small(HEAD_DIM, F32),
                   small(C_IDX_DIM, F32), tall(C_IDX_HEADS, F32),
                   small(LANES, BF16), tall(V_ROWS, BF16), small(C_IDX_DIM, BF16)],
        compiler_params=_params("parallel", "parallel"),
        name="dsa_proj",
    )(x, g, sc, sh, w["c_wqt"], w["c_wqit"], w["c_wsm"], w["c_wsmt"])


_INT_MIN = -2 ** 31
_COUNT_ROWS = 64


def _dsa_kernel(qt_ref, qit_ref, wit_ref, k_ref, vt_ref, ki_ref, farq_ref, nb_ref, o_ref,
                sk_sc, m_sc, acc_sc, p_sc, alpha_sc, *, qpos0, n_sel, nk_tiles):
    tq, tk = DSA_TQ, DSA_TK
    i = pl.program_id(1)
    q0 = qpos0 + i * tq
    nt = jnp.minimum((q0 + tq + tk - 1) // tk, nk_tiles)
    qch = (q0 + lax.broadcasted_iota(I32, (1, tq), 1)) >> CHUNK_SHIFT

    def admissible(ks, w):
        kp = ks + lax.broadcasted_iota(I32, (w, 1), 0)
        return (kp >> CHUNK_SHIFT) <= qch

    qit = qit_ref[0]
    qis = jnp.concatenate([qit[h * C_IDX_DIM:(h + 1) * C_IDX_DIM] for h in range(C_IDX_HEADS)], axis=1)
    wit = wit_ref[0]

    def score_body(j, c):
        ks = pl.multiple_of(j * tk, tk)
        d = jnp.dot(ki_ref[0, pl.ds(ks, tk), :], qis, preferred_element_type=F32)
        sc = jnp.zeros((tk, tq), F32)
        for h in range(C_IDX_HEADS):
            sc = sc + jnp.maximum(d[:, h * tq:(h + 1) * tq], 0.0) * wit[h:h + 1]
        sc = jnp.where(sc == 0.0, 0.0, sc)
        sc = jnp.where(admissible(ks, tk), sc, NEG_INF)
        bits = pltpu.bitcast(sc, I32)
        sk_sc[pl.ds(ks, tk), :] = bits ^ ((bits >> 31) & 0x7FFFFFFF)
        return c

    lax.fori_loop(0, nt, score_body, 0)
    tiles_per_step = DSA_SEARCH_ROWS // tk
    n_steps = (nt + tiles_per_step - 1) // tiles_per_step

    @pl.when(nt < n_steps * tiles_per_step)
    def _():
        sk_sc[pl.ds(pl.multiple_of(nt * tk, tk), tk), :] = jnp.full((tk, tq), _INT_MIN, I32)

    def count(pred):
        def body(j, c):
            kt = sk_sc[pl.ds(pl.multiple_of(j * DSA_SEARCH_ROWS, DSA_SEARCH_ROWS), DSA_SEARCH_ROWS), :]
            g = jnp.where(pred(kt), 1.0, 0.0)
            parts = [g[r * _COUNT_ROWS:(r + 1) * _COUNT_ROWS] for r in range(DSA_SEARCH_ROWS // _COUNT_ROWS)]
            while len(parts) > 1:
                parts = [parts[a] + parts[a + 1] for a in range(0, len(parts), 2)]
            return c + parts[0]
        c = lax.fori_loop(0, n_steps, body, jnp.zeros((_COUNT_ROWS, tq), F32))
        return jnp.sum(c, axis=0, keepdims=True)

    nsel = float(n_sel)
    lo = jnp.where(count(lambda kt: kt >= 0) >= nsel, 0, _INT_MIN).astype(I32)

    def bit_body(t, lo):
        cand = lo | jnp.left_shift(jnp.int32(1), 30 - t)
        return jnp.where(count(lambda kt: kt >= cand) >= nsel, cand, lo)

    thr = lax.fori_loop(0, 31, bit_body, lo)
    need = nsel - count(lambda kt: kt > thr)

    qt = qt_ref[0]
    qs = jnp.concatenate([qt[h * LANES:(h + 1) * LANES] for h in range(N_HEADS)], axis=1) + farq_ref[...]
    m_sc[...] = jnp.full(m_sc.shape, NEG_INF, F32)
    acc_sc[...] = jnp.zeros(acc_sc.shape, F32)
    ra = lax.broadcasted_iota(I32, (tk, tk), 0)
    ca = lax.broadcasted_iota(I32, (tk, tk), 1)
    earlier = (ca < ra).astype(BF16)

    def select(ks, w, run):
        kt = sk_sc[pl.ds(ks, w), :]
        eq = kt == thr
        rank = run + jnp.dot(earlier[:w, :w], jnp.where(eq, 1.0, 0.0).astype(BF16), preferred_element_type=F32)
        sel = ((kt > thr) | (eq & (rank < need))) & admissible(ks, w)
        return sel, run + jnp.sum(jnp.where(eq, 1.0, 0.0), axis=0, keepdims=True)

    def scores(ks, w):
        return jnp.dot(k_ref[0, pl.ds(ks, w), :], qs, preferred_element_type=F32)

    def softmax(s, sel, w, kind):
        for h in range(N_HEADS):
            sl = slice(h * tq, (h + 1) * tq)
            sh = s[:, sl]
            if kind is not None:
                sh = sh + nb_ref[kind, :, sl]
            sh = jnp.where(sel, sh, NEG_INF)
            m_old = m_sc[:, sl]
            m_new = jnp.maximum(m_old, jnp.max(sh, axis=0, keepdims=True))
            m_sc[:, sl] = m_new
            alpha_sc[:, sl] = jnp.exp2(m_old - m_new)
            p_sc[0:w, sl] = jnp.exp2(sh - m_new).astype(BF16)

    def accumulate(ks, w):
        pv = jnp.dot(vt_ref[0, :, pl.ds(ks, w)], p_sc[0:w, :], preferred_element_type=F32)
        acc_sc[...] = alpha_sc[...] * acc_sc[...] + pv

    p_sc[...] = jnp.zeros(p_sc.shape, BF16)
    alpha_sc[...] = jnp.ones(alpha_sc.shape, F32)
    n_far = jnp.maximum(q0 - tq, 0) // tk

    def far_body(j, carry):
        j_prev, run = carry
        ks = pl.multiple_of(j * tk, tk)
        sel, run = select(ks, tk, run)
        s = scores(ks, tk)
        accumulate(pl.multiple_of(j_prev * tk, tk), tk)
        softmax(s, sel, tk, None)
        return j, run

    j_last, run = lax.fori_loop(0, n_far, far_body, (0, jnp.zeros((1, tq), F32)))
    accumulate(pl.multiple_of(j_last * tk, tk), tk)
    ks0 = n_far * tk
    n_tail = (q0 + tq - ks0) // tq

    def tail_body(t, run):
        ks = pl.multiple_of(ks0 + t * tq, tq)
        kind = jnp.clip((ks - q0) // tq + 2, 0, 2)
        sel, run = select(ks, tq, run)
        softmax(scores(ks, tq), sel, tq, kind)
        accumulate(ks, tq)
        return run

    lax.fori_loop(0, n_tail, tail_body, run)
    acc = acc_sc[...]
    ot = jnp.concatenate([acc[:HEAD_DIM, h * tq:(h + 1) * tq] / acc[HEAD_DIM:HEAD_DIM + 1, h * tq:(h + 1) * tq]
                          for h in range(N_HEADS)], axis=0)
    o_ref[0] = ot.T.astype(BF16)


def _dsa_attention(qt, qit, wit, k, vt, ki, farq, nb, *, qpos0, n_sel):
    bsz, hq, lq = qt.shape
    lk = k.shape[1]
    hi = qit.shape[1]
    hd = N_HEADS * HEAD_DIM
    col = lambda b, i: (b, 0, i)
    whole = lambda b, i: (b, 0, 0)
    kern = functools.partial(_dsa_kernel, qpos0=qpos0, n_sel=n_sel, nk_tiles=lk // DSA_TK)
    return pl.pallas_call(
        kern,
        grid=(bsz, lq // DSA_TQ),
        in_specs=[pl.BlockSpec((1, hq, DSA_TQ), col),
                  pl.BlockSpec((1, hi, DSA_TQ), col),
                  pl.BlockSpec((1, C_IDX_HEADS, DSA_TQ), col),
                  pl.BlockSpec((1, lk, LANES), whole),
                  pl.BlockSpec((1, V_ROWS, lk), whole),
                  pl.BlockSpec((1, lk, C_IDX_DIM), whole),
                  pl.BlockSpec((LANES, N_HEADS * DSA_TQ), lambda b, i: (0, 0)),
                  pl.BlockSpec((3, DSA_TQ, N_HEADS * DSA_TQ), lambda b, i: (0, 0, 0))],
        out_specs=pl.BlockSpec((1, DSA_TQ, hd), lambda b, i: (b, i, 0)),
        out_shape=jax.ShapeDtypeStruct((bsz, lq, hd), BF16),
        scratch_shapes=[pltpu.VMEM((lk, DSA_TQ), I32),
                        pltpu.VMEM((1, N_HEADS * DSA_TQ), F32),
                        pltpu.VMEM((V_ROWS, N_HEADS * DSA_TQ), F32),
                        pltpu.VMEM((DSA_TK, N_HEADS * DSA_TQ), BF16),
                        pltpu.VMEM((1, N_HEADS * DSA_TQ), F32)],
        compiler_params=_params("parallel", "arbitrary"),
        name="dsa_attention",
    )(qt, qit, wit, k, vt, ki, farq, nb)


def _swa_proj_kernel(x_ref, g_ref, sc_ref, sh_ref, wq_ref, wkv_ref, q_ref, kv_ref, kvb_ref):
    h = _modnorm(x_ref[0], g_ref[...], sc_ref[0], sh_ref[0]).astype(BF16)
    q_ref[0] = (jnp.dot(h, wq_ref[...], preferred_element_type=F32) * HEAD_SCALE).astype(BF16)
    kv = jnp.dot(h, wkv_ref[...], preferred_element_type=F32)
    kv_ref[0] = kv
    kvb_ref[0] = kv.astype(BF16)


def _swa_proj(x, g, sc, sh, w, tm):
    bsz, L, d = x.shape
    hd = N_HEADS * HEAD_DIM
    kvw = 2 * D_KV_HEADS * HEAD_DIM
    row = lambda b, i: (b, i, 0)
    per_b = lambda b, i: (b, 0, 0)
    const = lambda b, i: (0, 0)
    return pl.pallas_call(
        _swa_proj_kernel,
        grid=(bsz, L // tm),
        in_specs=[pl.BlockSpec((1, tm, d), row),
                  pl.BlockSpec((1, d), const),
                  pl.BlockSpec((1, 1, d), per_b),
                  pl.BlockSpec((1, 1, d), per_b),
                  pl.BlockSpec((d, hd), const),
                  pl.BlockSpec((d, kvw), const)],
        out_specs=[pl.BlockSpec((1, tm, hd), row), pl.BlockSpec((1, tm, kvw), row),
                   pl.BlockSpec((1, tm, kvw), row)],
        out_shape=[jax.ShapeDtypeStruct((bsz, L, hd), BF16),
                   jax.ShapeDtypeStruct((bsz, L, kvw), F32),
                   jax.ShapeDtypeStruct((bsz, L, kvw), BF16)],
        compiler_params=_params("parallel", "parallel"),
        name="swa_proj",
    )(x, g, sc, sh, w["d_wq"], w["d_wkv"])


def _swa_kernel(q_ref, kv_ref, nb_ref, sink_ref, o_ref, *, mask_front):
    band = WINDOW + CHUNK
    c = pl.program_id(1)
    start = pl.multiple_of(c * CHUNK, CHUNK)
    kvb = kv_ref[0, pl.ds(start, band), :]
    q = q_ref[0]
    if mask_front:
        ok = (start - WINDOW + lax.broadcasted_iota(I32, (1, 1, band), 2)) >= 0
    kw = D_KV_HEADS * HEAD_DIM
    for g in range(D_KV_HEADS):
        kg = kvb[:, g * HEAD_DIM:(g + 1) * HEAD_DIM]
        vg = kvb[:, kw + g * HEAD_DIM:kw + (g + 1) * HEAD_DIM]
        qg = jnp.concatenate([q[:, (g * D_REP + r) * HEAD_DIM:(g * D_REP + r + 1) * HEAD_DIM]
                              for r in range(D_REP)], axis=0)
        s = lax.dot_general(qg, kg, _NT, preferred_element_type=F32).reshape(D_REP, CHUNK, band)
        s = s + nb_ref[g * D_REP:(g + 1) * D_REP]
        if mask_front:
            s = jnp.where(ok, s, NEG_INF)
        sink = sink_ref[g * D_REP:(g + 1) * D_REP]
        m = jnp.maximum(jnp.max(s, axis=2, keepdims=True), sink)
        e = jnp.exp(s - m)
        p = e / (jnp.sum(e, axis=2, keepdims=True) + jnp.exp(sink - m))
        o = jnp.dot(p.reshape(D_REP * CHUNK, band).astype(BF16), vg, preferred_element_type=F32)
        for r in range(D_REP):
            hh = g * D_REP + r
            o_ref[0, :, hh * HEAD_DIM:(hh + 1) * HEAD_DIM] = o[r * CHUNK:(r + 1) * CHUNK].astype(BF16)


def _swa_attention(q, kv, nb, sinks, *, mask_front):
    bsz, L, hd = q.shape
    lkv, kvw = kv.shape[1], kv.shape[2]
    return pl.pallas_call(
        functools.partial(_swa_kernel, mask_front=mask_front),
        grid=(bsz, L // CHUNK),
        in_specs=[pl.BlockSpec((1, CHUNK, hd), lambda b, c: (b, c, 0)),
                  pl.BlockSpec((1, lkv, kvw), lambda b, c: (b, 0, 0)),
                  pl.BlockSpec((N_HEADS, CHUNK, WINDOW + CHUNK), lambda b, c: (0, 0, 0)),
                  pl.BlockSpec((N_HEADS, 1, 1), lambda b, c: (0, 0, 0))],
        out_specs=pl.BlockSpec((1, CHUNK, hd), lambda b, c: (b, c, 0)),
        out_shape=jax.ShapeDtypeStruct((bsz, L, hd), BF16),
        compiler_params=_params("parallel", "parallel"),
        name="swa_attention",
    )(q, kv, nb, sinks)


def _route_rows(s, sb):
    n, m = N_GROUPS, EXPERTS_PER_GROUP
    grp = []
    for g in range(n):
        x = sb[g * m:(g + 1) * m]
        best = None
        for a in range(m):
            for b in range(a + 1, m):
                pair = x[a] + x[b]
                best = pair if best is None else jnp.maximum(best, pair)
        grp.append(best)
    chosen_g = []
    taken = None
    for g in range(n):
        is_g = None
        for o in range(g + 1, n):
            c = grp[g] >= grp[o]
            is_g = c if is_g is None else (is_g & c)
        if is_g is None:
            is_g = ~taken
        elif taken is not None:
            is_g = is_g & (~taken)
        taken = is_g if taken is None else (taken | is_g)
        chosen_g.append(is_g)
    picked = []
    for e in range(N_EXPERTS):
        g, a = divmod(e, m)
        beaten = jnp.zeros(sb[e].shape, F32)
        for b in range(m):
            if b == a:
                continue
            o = g * m + b
            wins = (sb[o] > sb[e]) | ((sb[o] == sb[e]) & (b < a))
            beaten = beaten + wins.astype(F32)
        picked.append(chosen_g[g] & (beaten < 2.0))
    tops = [jnp.where(picked[e], s[e], 0.0) for e in range(N_EXPERTS)]
    denom = tops[0]
    for e in range(1, N_EXPERTS):
        denom = denom + tops[e]
    return [t / denom for t in tops]


def _post_mix_kernel(o_ref, wo_ref, x_ref, gt_ref, g_ref, sc_ref, sh_ref, wr_ref, br_ref,
                     x1_ref, h2_ref, gates_ref, *, o_transposed):
    mixed = lax.dot_general(o_ref[0], wo_ref[...], _TN if o_transposed else _NN, preferred_element_type=F32)
    x1 = x_ref[0] + gt_ref[0] * mixed
    x1_ref[0] = x1
    h2 = _modnorm(x1, g_ref[...], sc_ref[0], sh_ref[0])
    h2_ref[0] = h2.astype(BF16)
    logits = _dot3(wr_ref[...], h2, _NT)
    s = _sigmoid(logits)
    sb = s + br_ref[...]
    rows = _route_rows([s[e:e + 1] for e in range(N_EXPERTS)], [sb[e:e + 1] for e in range(N_EXPERTS)])
    gates_ref[0] = jnp.concatenate(rows, axis=0)


def _post_mix(o, wo, x, gt, g, sc, sh, wr_t, br, tm, o_transposed=False):
    bsz, L, d = x.shape
    hd = wo.shape[0]
    o_spec = (pl.BlockSpec((1, hd, tm), lambda b, i: (b, 0, i)) if o_transposed
              else pl.BlockSpec((1, tm, hd), lambda b, i: (b, i, 0)))
    row = lambda b, i: (b, i, 0)
    per_b = lambda b, i: (b, 0, 0)
    const = lambda b, i: (0, 0)
    return pl.pallas_call(
        functools.partial(_post_mix_kernel, o_transposed=o_transposed),
        grid=(bsz, L // tm),
        in_specs=[o_spec,
                  pl.BlockSpec((hd, d), const),
                  pl.BlockSpec((1, tm, d), row),
                  pl.BlockSpec((1, 1, d), per_b),
                  pl.BlockSpec((1, d), const),
                  pl.BlockSpec((1, 1, d), per_b),
                  pl.BlockSpec((1, 1, d), per_b),
                  pl.BlockSpec((N_EXPERTS, d), const),
                  pl.BlockSpec((N_EXPERTS, 1), const)],
        out_specs=[pl.BlockSpec((1, tm, d), row), pl.BlockSpec((1, tm, d), row),
                   pl.BlockSpec((1, N_EXPERTS, tm), lambda b, i: (b, 0, i))],
        out_shape=[jax.ShapeDtypeStruct((bsz, L, d), F32),
                   jax.ShapeDtypeStruct((bsz, L, d), BF16),
                   jax.ShapeDtypeStruct((bsz, N_EXPERTS, L), F32)],
        compiler_params=_params("parallel", "parallel"),
        name="post_mix_route",
    )(o, wo, x, gt, g, sc, sh, wr_t, br)


def _moe_kernel(x_ref, h_ref, gates_ref, gt_ref, wgu_ref, wd_ref, o_ref, acc_sc):
    e = pl.program_id(2)

    @pl.when(e == 0)
    def _():
        acc_sc[...] = jnp.zeros(acc_sc.shape, F32)

    h = h_ref[0]
    gates = gates_ref[0]
    lane = lax.broadcasted_iota(I32, gates.shape, 1)
    gus = [jnp.dot(h, wgu_ref[u], preferred_element_type=F32) for u in range(MOE_EXPERTS_PER_STEP)]
    acts = [((gu[:, :D_EXPERT] * _sigmoid(gu[:, :D_EXPERT])) * gu[:, D_EXPERT:]).astype(BF16) for gu in gus]
    ys = [jnp.dot(acts[u], wd_ref[u], preferred_element_type=F32) for u in range(MOE_EXPERTS_PER_STEP)]
    tot = None
    for u in range(MOE_EXPERTS_PER_STEP):
        ge = jnp.sum(jnp.where(lane == e * MOE_EXPERTS_PER_STEP + u, gates, 0.0), axis=1, keepdims=True)
        tot = ge * ys[u] if tot is None else tot + ge * ys[u]
    acc_sc[...] += tot

    @pl.when(e == N_EXPERTS // MOE_EXPERTS_PER_STEP - 1)
    def _():
        o_ref[0] = x_ref[0] + gt_ref[0] * acc_sc[...]


def _moe(x, h, gates, gt, wgu, wd, tm):
    bsz, L, d = x.shape
    row = lambda b, i, e: (b, i, 0)
    return pl.pallas_call(
        _moe_kernel,
        grid=(bsz, L // tm, N_EXPERTS // MOE_EXPERTS_PER_STEP),
        in_specs=[pl.BlockSpec((1, tm, d), row),
                  pl.BlockSpec((1, tm, d), row),
                  pl.BlockSpec((1, tm, N_EXPERTS), row),
                  pl.BlockSpec((1, 1, d), lambda b, i, e: (b, 0, 0)),
                  pl.BlockSpec((MOE_EXPERTS_PER_STEP, d, 2 * D_EXPERT), lambda b, i, e: (e, 0, 0)),
                  pl.BlockSpec((MOE_EXPERTS_PER_STEP, D_EXPERT, d), lambda b, i, e: (e, 0, 0))],
        out_specs=pl.BlockSpec((1, tm, d), row),
        out_shape=jax.ShapeDtypeStruct((bsz, L, d), F32),
        scratch_shapes=[pltpu.VMEM((tm, d), F32)],
        compiler_params=_params("parallel", "parallel", "arbitrary"),
        name="moe_experts",
    )(x, h, gates, gt, wgu, wd)


def _final_norm_kernel(x_ref, g_ref, o_ref):
    o_ref[0] = _rms(x_ref[0]) * g_ref[...]


def _final_norm(x, g, tm):
    bsz, L, d = x.shape
    row = lambda b, i: (b, i, 0)
    return pl.pallas_call(
        _final_norm_kernel,
        grid=(bsz, L // tm),
        in_specs=[pl.BlockSpec((1, tm, d), row), pl.BlockSpec((1, d), lambda b, i: (0, 0))],
        out_specs=pl.BlockSpec((1, tm, d), row),
        out_shape=jax.ShapeDtypeStruct((bsz, L, d), F32),
        compiler_params=_params("parallel", "parallel"),
        name="final_norm",
    )(x, g)


def _rot_cols(w):
    half = w.shape[-1] // 2
    return jnp.concatenate([-w[..., half:], w[..., :half]], axis=-1)


def _head_pad(w2d, width):
    w3 = w2d.reshape(w2d.shape[0], N_HEADS, HEAD_DIM)
    return jnp.pad(w3, ((0, 0), (0, 0), (0, width - HEAD_DIM))).reshape(w2d.shape[0], N_HEADS * width)


def _prep_weights(a_w_in, a_g_q, a_w_uq, a_g_kv, a_w_ukv, a_w_o, b_w_in, b_b_f, b_w_o,
                  c_w_in, c_w_o, d_w_in, d_sinks, d_w_o, moe_w_router, moe_b_router,
                  moe_w_gate, moe_w_up, moe_w_down):
    w = {}
    hd = N_HEADS * HEAD_DIM
    kr = a_w_in[:, A_Q_LORA + A_KV_LORA:]
    w["a_in"] = jnp.concatenate([a_w_in, _rot_cols(kr)], axis=1).astype(BF16)
    w["a_gq"] = a_g_q.reshape(1, -1)
    w["a_gkv"] = a_g_kv.reshape(1, -1)
    uq = a_w_uq.reshape(A_Q_LORA, N_HEADS, A_NOPE + A_ROPE)
    pad = LANES - A_NOPE - A_ROPE
    zq = lambda n: jnp.zeros((A_Q_LORA, N_HEADS, n), F32)
    w["a_wq"] = jnp.concatenate([uq, zq(pad)], axis=-1).reshape(A_Q_LORA, N_HEADS * LANES).astype(BF16)
    w["a_wqs"] = jnp.concatenate([zq(A_NOPE), _rot_cols(uq[..., A_NOPE:]), zq(pad)],
                                 axis=-1).reshape(A_Q_LORA, N_HEADS * LANES).astype(BF16)
    ukv = a_w_ukv.reshape(A_KV_LORA, N_HEADS, A_NOPE + A_V)
    w["a_wk"] = jnp.concatenate([ukv[..., :A_NOPE], jnp.zeros((A_KV_LORA, N_HEADS, LANES - A_NOPE), F32)],
                                axis=-1).reshape(A_KV_LORA, N_HEADS * LANES).astype(BF16)
    w["a_wv"] = ukv[..., A_NOPE:].reshape(A_KV_LORA, N_HEADS * A_V).astype(BF16)
    w["a_wvt"] = _head_pad(ukv[..., A_NOPE:].reshape(A_KV_LORA, N_HEADS * A_V), V_ROWS).T.astype(BF16)
    sel = np.zeros((A_ROPE, N_HEADS, LANES), np.float32)
    for r in range(A_ROPE):
        sel[r, :, A_NOPE + r] = 1.0
    w["a_sel"] = jnp.asarray(sel.reshape(A_ROPE, N_HEADS * LANES), BF16)
    w["a_wo"] = a_w_o.astype(BF16)
    ones_rows = np.zeros((N_HEADS, V_ROWS, 1), np.float32)
    ones_rows[:, HEAD_DIM, 0] = 1.0
    w["ones_rows"] = jnp.asarray(ones_rows.reshape(N_HEADS * V_ROWS, 1))
    w["b_wq"] = b_w_in[:, :hd].astype(BF16)
    w["b_wk"] = b_w_in[:, hd:2 * hd].astype(BF16)
    w["b_wv"] = b_w_in[:, 2 * hd:3 * hd].astype(BF16)
    w["b_wf"] = jnp.pad(b_w_in[:, 3 * hd:], ((0, 0), (0, LANES - N_HEADS))).astype(BF16)
    w["b_bf"] = b_b_f.reshape(1, N_HEADS)
    w["b_wo"] = b_w_o.astype(BF16)
    w["b_wqt"] = _head_pad(b_w_in[:, :hd], LANES).T.astype(BF16)
    w["b_wvt"] = _head_pad(b_w_in[:, 2 * hd:3 * hd], V_ROWS).T.astype(BF16)
    hi = C_IDX_HEADS * C_IDX_DIM
    w["c_wqt"] = _head_pad(c_w_in[:, :hd], LANES).T.astype(BF16)
    kv = c_w_in[:, hd:hd + 2 * HEAD_DIM]
    qi = c_w_in[:, hd + 2 * HEAD_DIM:hd + 2 * HEAD_DIM + hi]
    rest = c_w_in[:, hd + 2 * HEAD_DIM + hi:]
    w["c_wqit"] = qi.T.astype(BF16)
    sm = jnp.concatenate([kv, rest], axis=1)
    w["c_wsm"] = jnp.pad(sm, ((0, 0), (0, 2 * LANES - sm.shape[1]))).astype(BF16)
    w["c_wsmt"] = w["c_wsm"].T
    w["c_wo"] = c_w_o.astype(BF16)
    w["d_wq"] = d_w_in[:, :hd].astype(BF16)
    w["d_wkv"] = d_w_in[:, hd:].astype(BF16)
    w["d_sinks"] = d_sinks.astype(F32).reshape(N_HEADS, 1, 1)
    w["d_wo"] = d_w_o.astype(BF16)
    w["wr_t"] = moe_w_router.T
    w["br"] = moe_b_router.reshape(N_EXPERTS, 1)
    w["wgu"] = jnp.concatenate([moe_w_gate, moe_w_up], axis=-1).astype(BF16)
    w["wd"] = moe_w_down.astype(BF16)
    return w


def _rope_tables(pos):
    half = A_ROPE // 2
    inv = ROPE_THETA ** (-jnp.arange(half, dtype=F32) / half)
    ang = pos.astype(F32)[:, None] * inv[None, :]
    cos, sin = jnp.cos(ang), jnp.sin(ang)
    n = pos.shape[0]
    pad = LANES - A_NOPE - A_ROPE
    cosp = jnp.concatenate([jnp.ones((n, A_NOPE), F32), cos, cos, jnp.zeros((n, pad), F32)], axis=1)
    sinp = jnp.concatenate([jnp.zeros((n, A_NOPE), F32), sin, sin, jnp.zeros((n, pad), F32)], axis=1)
    return cosp, sinp


def _pad_rows(a, n):
    return jnp.pad(a, ((0, 0), (0, n - a.shape[1])) + ((0, 0),) * (a.ndim - 2))


def _round_up(n, m):
    return (n + m - 1) // m * m


def kernel(x_prompt, x_sample, c_prompt, c_sample, cache_a_latent, cache_a_krope, cache_b_k, cache_b_v, cache_b_logf, cache_c_k, cache_c_v, cache_c_kidx, cache_d_k, cache_d_v, w_ada, b_ada, g_mix, g_ffn, g_final, rel_bias, a_w_in, a_g_q, a_w_uq, a_g_kv, a_w_ukv, a_w_o, b_w_in, b_b_f, b_w_o, c_w_in, c_w_o, d_w_in, d_sinks, d_w_o, moe_w_router, moe_b_router, moe_w_gate, moe_w_up, moe_w_down):
    bp, S, d = x_prompt.shape
    bs, Ls, _ = x_sample.shape
    P = cache_a_latent.shape[1]
    depth = w_ada.shape[0]
    hd = N_HEADS * HEAD_DIM
    assert S % PROJ_ROWS == 0 and Ls == CHUNK and P % DSA_TQ == 0
    assert DSA_SEARCH_ROWS % DSA_TK == 0 and S % DSA_SEARCH_ROWS == 0
    assert S % FLASH_T_TILE == 0 and FLASH_T_TILE % CHUNK == 0 and S % FOX_PROJ_ROWS == 0
    assert DSA_SEARCH_ROWS % SAMPLE_TK == 0
    tm_p, tm_s = PROJ_ROWS, Ls
    lk_s = _round_up(P + Ls, DSA_SEARCH_ROWS)

    w = _prep_weights(a_w_in, a_g_q, a_w_uq, a_g_kv, a_w_ukv, a_w_o, b_w_in, b_b_f, b_w_o,
                      c_w_in, c_w_o, d_w_in, d_sinks, d_w_o, moe_w_router, moe_b_router,
                      moe_w_gate, moe_w_up, moe_w_down)
    mod = _adaln(jnp.concatenate([c_prompt, c_sample], axis=0), w_ada, b_ada)
    farq, nb_dsa, nb_swa = _bias_tiles(rel_bias)

    def mods(i, lo, hi):
        return [mod[i, lo:hi, k * d:(k + 1) * d][:, None, :] for k in range(6)]

    xp, xs = x_prompt, x_sample
    outs = {}
    for i in range(depth):
        sh1_p, sc1_p, gt1_p, sh2_p, sc2_p, gt2_p = mods(i, 0, bp)
        sh1_s, sc1_s, gt1_s, sh2_s, sc2_s, gt2_s = mods(i, bp, bp + bs)
        g1 = g_mix[i].reshape(1, d)
        mixer = i % 4
        if mixer == 0:
            cos_p, sin_p = _rope_tables(jnp.arange(S))
            cos_s, sin_s = _rope_tables(P + jnp.arange(Ls))
            qt_p, lat_p, kr_p = _mla_proj_t(xp, g1, sc1_p, sh1_p, w, cos_p, sin_p, tm_p)
            k_p, vt_p = _mla_expand_t(lat_p, kr_p, w, tm_p)
            op = _flash_t(qt_p, k_p, vt_p, per_frame=False)
            q_s, lat_s, kr_s = _mla_proj(xs, g1, sc1_s, sh1_s, w, cos_s, sin_s, tm_s)
            lat_all = _pad_rows(jnp.concatenate([cache_a_latent, lat_s], axis=1), lk_s)
            kr_all = _pad_rows(jnp.concatenate([cache_a_krope, kr_s], axis=1), lk_s)
            k_s, v_s = _mla_expand(lat_all, kr_all, w, DSA_TK)
            os_ = _flash(q_s, k_s, v_s, tq=Ls, tk=SAMPLE_TK, qpos0=P, per_frame=False, wide=True)
            outs["a"] = (lat_p, kr_p, lat_s, kr_s)
            wo = w["a_wo"]
        elif mixer == 1:
            qt_p, k_p, v_p, vt_p, lf_p, lft_p = _fox_proj_t(xp, g1, sc1_p, sh1_p, w, FOX_PROJ_ROWS)
            cumt_p, kaug_p = _fox_cumaug(lft_p, lf_p, k_p, CUMSUM_TILE)
            op = _flash_t(qt_p, kaug_p, vt_p, per_frame=True, qd=cumt_p)
            q_s, k_s, v_s, kb_s, vb_s, lf_s = _fox_proj(xs, g1, sc1_s, sh1_s, w, tm_s)
            lf_all = jnp.concatenate([cache_b_logf.astype(F32), lf_s], axis=1)
            cum_s = _cumsum_rows(_pad_rows(lf_all, lk_s).swapaxes(1, 2), CUMSUM_TILE)
            kb_all = _pad_rows(jnp.concatenate([cache_b_k.reshape(bs, P, hd).astype(BF16), kb_s], axis=1), lk_s)
            vb_all = _pad_rows(jnp.concatenate([cache_b_v.reshape(bs, P, hd).astype(BF16), vb_s], axis=1), lk_s)
            os_ = _flash(q_s, kb_all, vb_all, tq=Ls, tk=SAMPLE_TK, qpos0=P, per_frame=True, wide=False,
                         qd=cum_s[:, :, P:P + Ls, None], kd=cum_s[:, :, None, :])
            shp = lambda a: a.reshape(a.shape[0], a.shape[1], N_HEADS, HEAD_DIM)
            outs["b"] = (shp(k_p), shp(v_p), lf_p, shp(k_s), shp(v_s), lf_s)
            wo = w["b_wo"]
        elif mixer == 2:
            qt_p, qit_p, k_p, v_p, ki_p, wit_p, kaug_p, vt_p, kib_p = _dsa_proj(xp, g1, sc1_p, sh1_p, w, tm_p)
            op = _dsa_attention(qt_p, qit_p, wit_p, kaug_p, vt_p, kib_p, farq, nb_dsa, qpos0=0,
                                n_sel=min(C_TOPK_MAX, S // 4))
            qt_s, qit_s, k_s, v_s, ki_s, wit_s, kaug_s, vt_s, kib_s = _dsa_proj(xs, g1, sc1_s, sh1_s, w, tm_s)
            pad_q = lambda a: jnp.pad(a, ((0, 0), (0, 0), (0, DSA_TQ - Ls)))
            ones_cols = jnp.zeros((bs, P, LANES - HEAD_DIM), BF16).at[:, :, :2].set(1.0)
            kaug_c = jnp.concatenate([cache_c_k.astype(BF16), ones_cols], axis=2)
            kaug_all = _pad_rows(jnp.concatenate([kaug_c, kaug_s], axis=1), lk_s)
            ones_row = jnp.zeros((bs, V_ROWS - HEAD_DIM, P), BF16).at[:, 0, :].set(1.0)
            vt_c = jnp.concatenate([jnp.swapaxes(cache_c_v, 1, 2).astype(BF16), ones_row], axis=1)
            vt_all = jnp.pad(jnp.concatenate([vt_c, vt_s], axis=2), ((0, 0), (0, 0), (0, lk_s - P - Ls)))
            ki_all = _pad_rows(jnp.concatenate([cache_c_kidx.astype(BF16), kib_s], axis=1), lk_s)
            os_ = _dsa_attention(pad_q(qt_s), pad_q(qit_s), pad_q(wit_s), kaug_all, vt_all, ki_all,
                                 farq, nb_dsa, qpos0=P, n_sel=min(C_TOPK_MAX, (P + Ls) // 4))[:, :Ls]
            outs["c"] = (k_p, v_p, ki_p, k_s, v_s, ki_s)
            wo = w["c_wo"]
        else:
            kvw = D_KV_HEADS * HEAD_DIM
            q_p, kv_p, kvb_p = _swa_proj(xp, g1, sc1_p, sh1_p, w, tm_p)
            op = _swa_attention(q_p, jnp.pad(kvb_p, ((0, 0), (WINDOW, 0), (0, 0))), nb_swa, w["d_sinks"],
                                mask_front=True)
            q_s, kv_s, kvb_s = _swa_proj(xs, g1, sc1_s, sh1_s, w, tm_s)
            wc = cache_d_k.shape[1]
            ck = cache_d_k.reshape(bs, wc, kvw)
            cv = cache_d_v.reshape(bs, wc, kvw)
            kv_cache = jnp.concatenate([ck, cv], axis=-1)
            kv_all = jnp.concatenate([kv_cache.astype(BF16), kvb_s], axis=1)
            kv_all = jnp.pad(kv_all, ((0, 0), (WINDOW - wc, 0), (0, 0)))
            os_ = _swa_attention(q_s, kv_all, nb_swa, w["d_sinks"], mask_front=(wc < WINDOW))
            keep = min(WINDOW, S)
            shp = lambda a: a.reshape(a.shape[0], a.shape[1], D_KV_HEADS, HEAD_DIM)
            k_roll = jnp.concatenate([ck, kv_s[..., :kvw]], axis=1)[:, Ls:]
            v_roll = jnp.concatenate([cv, kv_s[..., kvw:]], axis=1)[:, Ls:]
            outs["d"] = (shp(kv_p[:, S - keep:, :kvw]), shp(kv_p[:, S - keep:, kvw:]), shp(k_roll), shp(v_roll))
            wo = w["d_wo"]

        g2 = g_ffn[i].reshape(1, d)
        x1_p, h2_p, gates_p = _post_mix(op, wo, xp, gt1_p, g2, sc2_p, sh2_p, w["wr_t"], w["br"], tm_p,
                                        o_transposed=(mixer < 2))
        xp = _moe(x1_p, h2_p, jnp.swapaxes(gates_p, 1, 2), gt2_p, w["wgu"][i], w["wd"][i], tm_p)
        x1_s, h2_s, gates_s = _post_mix(os_, wo, xs, gt1_s, g2, sc2_s, sh2_s, w["wr_t"], w["br"], tm_s)
        xs = _moe(x1_s, h2_s, jnp.swapaxes(gates_s, 1, 2), gt2_s, w["wgu"][i], w["wd"][i], tm_s)

    gf = g_final.reshape(1, d)
    y_p = _final_norm(xp, gf, tm_p)
    y_s = _final_norm(xs, gf, tm_s)
    a_lat_p, a_kr_p, a_lat_s, a_kr_s = outs["a"]
    b_k_p, b_v_p, b_lf_p, b_k_s, b_v_s, b_lf_s = outs["b"]
    c_k_p, c_v_p, c_ki_p, c_k_s, c_v_s, c_ki_s = outs["c"]
    d_k_p, d_v_p, d_k_s, d_v_s = outs["d"]
    return (y_p, y_s,
            a_lat_p, a_kr_p, b_k_p, b_v_p, b_lf_p, c_k_p, c_v_p, c_ki_p, d_k_p, d_v_p,
            a_lat_s, a_kr_s, b_k_s, b_v_s, b_lf_s, c_k_s, c_v_s, c_ki_s, d_k_s, d_v_s)
```

```python
import functools
import math

import jax
import jax.numpy as jnp
import numpy as np
from jax import lax
from jax.experimental import pallas as pl
from jax.experimental.pallas import tpu as pltpu

F32 = jnp.float32
BF16 = jnp.bfloat16
I32 = jnp.int32

CHUNK = 64
CHUNK_SHIFT = CHUNK.bit_length() - 1
NORM_EPS = 1e-6
NEG_INF = -1e30
LOG2E = math.log2(math.e)
N_HEADS = 16
HEAD_DIM = 64
HEAD_SCALE = HEAD_DIM ** -0.5
N_BUCKETS = 32
MAX_DISTANCE = 128
A_Q_LORA = 512
A_KV_LORA = 256
A_NOPE = 64
A_ROPE = 32
A_V = 64
A_SCALE = (A_NOPE + A_ROPE) ** -0.5
ROPE_THETA = 10000.0
C_IDX_HEADS = 8
C_IDX_DIM = 64
C_TOPK_MAX = 256
D_KV_HEADS = 2
D_REP = N_HEADS // D_KV_HEADS
WINDOW = 128
N_WIN_CHUNKS = WINDOW // CHUNK
N_EXPERTS = 16
N_GROUPS = 4
EXPERTS_PER_GROUP = N_EXPERTS // N_GROUPS
D_EXPERT = 256

LANES = 128
VMEM_LIMIT_BYTES = 56 * 1024 * 1024

PROJ_ROWS = 512
FLASH_TQ = 256
FLASH_TK = 512
DSA_TQ = 128
DSA_TK = 256
DSA_SEARCH_ROWS = 512
V_ROWS = HEAD_DIM + 16
CUMSUM_TILE = 256
FLASH_T_TILE = 512
FOX_PROJ_ROWS = 256
FLASH_T_HEADS = 4
FLASH_T_HEADS_DECAY = 2
MOE_EXPERTS_PER_STEP = 8

_NT = (((1,), (1,)), ((), ()))
_NN = (((1,), (0,)), ((), ()))
_TN = (((0,), (0,)), ((), ()))


def _params(*sem):
    return pltpu.CompilerParams(dimension_semantics=sem, vmem_limit_bytes=VMEM_LIMIT_BYTES)


def _split2(a):
    hi = a.astype(BF16)
    lo = (a - hi.astype(F32)).astype(BF16)
    return hi, lo


def _split3(c):
    hi = c.astype(BF16).astype(F32)
    r1 = c - hi
    mid = r1.astype(BF16).astype(F32)
    lo = (r1 - mid).astype(BF16).astype(F32)
    return hi, mid, lo


def _dot3(a, b, dims):
    ah, al = _split2(a)
    bh, bl = _split2(b)
    d = lambda x, y: lax.dot_general(x, y, dims, preferred_element_type=F32)
    return d(ah, bh) + (d(ah, bl) + d(al, bh))


def _rms(x):
    return x * lax.rsqrt(jnp.mean(x * x, axis=-1, keepdims=True) + NORM_EPS)


def _modnorm(x, g, sc, sh):
    return _rms(x) * g * (1.0 + sc) + sh


def _sigmoid(z):
    return 1.0 / (1.0 + jnp.exp(-z))


def _log_sigmoid(z):
    return jnp.minimum(z, 0.0) - jnp.log1p(jnp.exp(-jnp.abs(z)))


def _adaln_kernel(c_ref, w_ref, b_ref, o_ref):
    c = c_ref[...]
    s = c * _sigmoid(c)
    o_ref[0] = _dot3(s, w_ref[0], _NN) + b_ref[0]


def _adaln(c_all, w_ada, b_ada):
    depth, d, d6 = w_ada.shape
    bc = c_all.shape[0]
    return pl.pallas_call(
        _adaln_kernel,
        grid=(depth, d6 // d),
        in_specs=[pl.BlockSpec((bc, d), lambda i, j: (0, 0)),
                  pl.BlockSpec((1, d, d), lambda i, j: (i, 0, j)),
                  pl.BlockSpec((1, 1, d), lambda i, j: (i, 0, j))],
        out_specs=pl.BlockSpec((1, bc, d), lambda i, j: (i, 0, j)),
        out_shape=jax.ShapeDtypeStruct((depth, bc, d6), F32),
        compiler_params=_params("parallel", "parallel"),
        name="adaln",
    )(c_all, w_ada, b_ada.reshape(depth, 1, d6))


_FAR_THRESHOLDS = (12, 16, 23, 32, 46, 64, 91)
_FAR_BUCKET = N_BUCKETS // 2 - 1
FAR_HI_ROW = HEAD_DIM
FAR_LO_ROW = HEAD_DIM + 1


def _rel_bucket(rel):
    n = jnp.abs(rel)
    nb = N_BUCKETS // 2
    max_exact = nb // 2
    far = jnp.full(rel.shape, max_exact, I32)
    for t in _FAR_THRESHOLDS:
        far = far + (n >= t).astype(I32)
    return jnp.where(rel > 0, nb, 0) + jnp.where(n < max_exact, n, far)


def _bias_kernel(tab_ref, farq_ref, dsa_ref, swa_ref):
    def lookup(bucket, h):
        acc = jnp.zeros(bucket.shape, F32)
        for b in range(N_BUCKETS):
            acc = jnp.where(bucket == b, tab_ref[b, h], acc)
        return acc

    kr = lax.broadcasted_iota(I32, (DSA_TQ, DSA_TQ), 0)
    qc = lax.broadcasted_iota(I32, (DSA_TQ, DSA_TQ), 1)
    for t, off in enumerate((-2 * DSA_TQ, -DSA_TQ, 0)):
        bucket = _rel_bucket(kr - qc + off)
        for h in range(N_HEADS):
            far = tab_ref[_FAR_BUCKET, h]
            bias = lookup(bucket, h)
            dsa_ref[t, :, h * DSA_TQ:(h + 1) * DSA_TQ] = (bias - far) * LOG2E
            if t > 0:
                swa_ref[t - 1, :, h * DSA_TQ:(h + 1) * DSA_TQ] = bias * LOG2E
    row = lax.broadcasted_iota(I32, (LANES, DSA_TQ), 0)
    for h in range(N_HEADS):
        c = jnp.full((LANES, DSA_TQ), tab_ref[_FAR_BUCKET, h] * LOG2E, F32)
        hi = c.astype(BF16).astype(F32)
        lo = (c - hi).astype(BF16).astype(F32)
        blk = jnp.where(row == FAR_HI_ROW, hi, jnp.where(row == FAR_LO_ROW, lo, 0.0))
        farq_ref[:, h * DSA_TQ:(h + 1) * DSA_TQ] = blk.astype(BF16)


def _bias_tiles(rel_bias):
    vm = pl.BlockSpec(memory_space=pltpu.VMEM)
    return pl.pallas_call(
        _bias_kernel,
        in_specs=[pl.BlockSpec(memory_space=pltpu.SMEM)],
        out_specs=[vm, vm, vm],
        out_shape=[jax.ShapeDtypeStruct((LANES, N_HEADS * DSA_TQ), BF16),
                   jax.ShapeDtypeStruct((3, DSA_TQ, N_HEADS * DSA_TQ), F32),
                   jax.ShapeDtypeStruct((2, DSA_TQ, N_HEADS * DSA_TQ), F32)],
        compiler_params=pltpu.CompilerParams(vmem_limit_bytes=VMEM_LIMIT_BYTES),
        name="bias_tiles",
    )(rel_bias)


def _mla_in(x_ref, g_ref, sc_ref, sh_ref, win_ref, gq_ref, gkv_ref, cos_ref, sin_ref, lat_ref, kr_ref):
    h = _modnorm(x_ref[0], g_ref[...], sc_ref[0], sh_ref[0]).astype(BF16)
    hw = jnp.dot(h, win_ref[...], preferred_element_type=F32)
    o1 = A_Q_LORA
    o2 = o1 + A_KV_LORA
    o3 = o2 + A_ROPE
    lat_ref[0] = _rms(hw[:, o1:o2]) * gkv_ref[...]
    cosr = cos_ref[...][:, A_NOPE:A_NOPE + A_ROPE]
    sinr = sin_ref[...][:, A_NOPE:A_NOPE + A_ROPE]
    kr_ref[0] = hw[:, o2:o3] * cosr + hw[:, o3:o3 + A_ROPE] * sinr
    return (_rms(hw[:, :o1]) * gq_ref[...]).astype(BF16)


def _mla_proj_kernel(x_ref, g_ref, sc_ref, sh_ref, win_ref, gq_ref, gkv_ref, wq_ref, wqs_ref,
                     cos_ref, sin_ref, q_ref, lat_ref, kr_ref):
    cqn = _mla_in(x_ref, g_ref, sc_ref, sh_ref, win_ref, gq_ref, gkv_ref, cos_ref, sin_ref, lat_ref, kr_ref)
    cosp = cos_ref[...]
    sinp = sin_ref[...]
    a = jnp.dot(cqn, wq_ref[...], preferred_element_type=F32)
    b = jnp.dot(cqn, wqs_ref[...], preferred_element_type=F32)
    for hd in range(N_HEADS):
        sl = slice(hd * LANES, (hd + 1) * LANES)
        q_ref[0, :, sl] = ((a[:, sl] * cosp + b[:, sl] * sinp) * A_SCALE).astype(BF16)


def _mla_proj(x, g, sc, sh, w, cosp, sinp, tm):
    bsz, L, d = x.shape
    row = lambda b, i: (b, i, 0)
    per_b = lambda b, i: (b, 0, 0)
    const = lambda b, i: (0, 0)
    nin = w["a_in"].shape[1]
    return pl.pallas_call(
        _mla_proj_kernel,
        grid=(bsz, L // tm),
        in_specs=[pl.BlockSpec((1, tm, d), row),
                  pl.BlockSpec((1, d), const),
                  pl.BlockSpec((1, 1, d), per_b),
                  pl.BlockSpec((1, 1, d), per_b),
                  pl.BlockSpec((d, nin), const),
                  pl.BlockSpec((1, A_Q_LORA), const),
                  pl.BlockSpec((1, A_KV_LORA), const),
                  pl.BlockSpec((A_Q_LORA, N_HEADS * LANES), const),
                  pl.BlockSpec((A_Q_LORA, N_HEADS * LANES), const),
                  pl.BlockSpec((tm, LANES), lambda b, i: (i, 0)),
                  pl.BlockSpec((tm, LANES), lambda b, i: (i, 0))],
        out_specs=[pl.BlockSpec((1, tm, N_HEADS * LANES), row),
                   pl.BlockSpec((1, tm, A_KV_LORA), row),
                   pl.BlockSpec((1, tm, A_ROPE), row)],
        out_shape=[jax.ShapeDtypeStruct((bsz, L, N_HEADS * LANES), BF16),
                   jax.ShapeDtypeStruct((bsz, L, A_KV_LORA), F32),
                   jax.ShapeDtypeStruct((bsz, L, A_ROPE), F32)],
        compiler_params=_params("parallel", "parallel"),
        name="mla_proj",
    )(x, g, sc, sh, w["a_in"], w["a_gq"], w["a_gkv"], w["a_wq"], w["a_wqs"], cosp, sinp)


def _mla_expand_kernel(lat_ref, kr_ref, wk_ref, sel_ref, wv_ref, k_ref, v_ref):
    lat = lat_ref[0].astype(BF16)
    kr = kr_ref[0].astype(BF16)
    k = (jnp.dot(lat, wk_ref[...], preferred_element_type=F32)
         + jnp.dot(kr, sel_ref[...], preferred_element_type=F32))
    k_ref[0] = k.astype(BF16)
    v_ref[0] = jnp.dot(lat, wv_ref[...], preferred_element_type=F32).astype(BF16)


def _mla_expand(lat, kr, w, tm):
    bsz, L, _ = lat.shape
    row = lambda b, i: (b, i, 0)
    const = lambda b, i: (0, 0)
    return pl.pallas_call(
        _mla_expand_kernel,
        grid=(bsz, L // tm),
        in_specs=[pl.BlockSpec((1, tm, A_KV_LORA), row),
                  pl.BlockSpec((1, tm, A_ROPE), row),
                  pl.BlockSpec((A_KV_LORA, N_HEADS * LANES), const),
                  pl.BlockSpec((A_ROPE, N_HEADS * LANES), const),
                  pl.BlockSpec((A_KV_LORA, N_HEADS * A_V), const)],
        out_specs=[pl.BlockSpec((1, tm, N_HEADS * LANES), row),
                   pl.BlockSpec((1, tm, N_HEADS * A_V), row)],
        out_shape=[jax.ShapeDtypeStruct((bsz, L, N_HEADS * LANES), BF16),
                   jax.ShapeDtypeStruct((bsz, L, N_HEADS * A_V), BF16)],
        compiler_params=_params("parallel", "parallel"),
        name="mla_expand",
    )(lat, kr, w["a_wk"], w["a_sel"], w["a_wv"])


def _mla_proj_t_kernel(x_ref, g_ref, sc_ref, sh_ref, win_ref, gq_ref, gkv_ref, wqt_ref, wqst_ref,
                       cos_ref, sin_ref, cost_ref, sint_ref, qt_ref, lat_ref, kr_ref):
    cqn = _mla_in(x_ref, g_ref, sc_ref, sh_ref, win_ref, gq_ref, gkv_ref, cos_ref, sin_ref, lat_ref, kr_ref)
    a = lax.dot_general(wqt_ref[...], cqn, _NT, preferred_element_type=F32)
    b = lax.dot_general(wqst_ref[...], cqn, _NT, preferred_element_type=F32)
    cost = cost_ref[...]
    sint = sint_ref[...]
    for hd in range(N_HEADS):
        sl = slice(hd * LANES, (hd + 1) * LANES)
        qt_ref[0, sl, :] = ((a[sl] * cost + b[sl] * sint) * (A_SCALE * LOG2E)).astype(BF16)


def _mla_proj_t(x, g, sc, sh, w, cosp, sinp, tm):
    bsz, L, d = x.shape
    row = lambda b, i: (b, i, 0)
    per_b = lambda b, i: (b, 0, 0)
    const = lambda b, i: (0, 0)
    nin = w["a_in"].shape[1]
    hq = N_HEADS * LANES
    return pl.pallas_call(
        _mla_proj_t_kernel,
        grid=(bsz, L // tm),
        in_specs=[pl.BlockSpec((1, tm, d), row),
                  pl.BlockSpec((1, d), const),
                  pl.BlockSpec((1, 1, d), per_b),
                  pl.BlockSpec((1, 1, d), per_b),
                  pl.BlockSpec((d, nin), const),
                  pl.BlockSpec((1, A_Q_LORA), const),
                  pl.BlockSpec((1, A_KV_LORA), const),
                  pl.BlockSpec((hq, A_Q_LORA), const),
                  pl.BlockSpec((hq, A_Q_LORA), const),
                  pl.BlockSpec((tm, LANES), lambda b, i: (i, 0)),
                  pl.BlockSpec((tm, LANES), lambda b, i: (i, 0)),
                  pl.BlockSpec((LANES, tm), lambda b, i: (0, i)),
                  pl.BlockSpec((LANES, tm), lambda b, i: (0, i))],
        out_specs=[pl.BlockSpec((1, hq, tm), lambda b, i: (b, 0, i)),
                   pl.BlockSpec((1, tm, A_KV_LORA), row),
                   pl.BlockSpec((1, tm, A_ROPE), row)],
        out_shape=[jax.ShapeDtypeStruct((bsz, hq, L), BF16),
                   jax.ShapeDtypeStruct((bsz, L, A_KV_LORA), F32),
                   jax.ShapeDtypeStruct((bsz, L, A_ROPE), F32)],
        compiler_params=_params("parallel", "parallel"),
        name="mla_proj_t",
    )(x, g, sc, sh, w["a_in"], w["a_gq"], w["a_gkv"], w["a_wq"].T, w["a_wqs"].T, cosp, sinp, cosp.T, sinp.T)


def _mla_expand_t_kernel(lat_ref, kr_ref, wk_ref, sel_ref, wvt_ref, ones_ref, k_ref, vt_ref):
    lat = lat_ref[0].astype(BF16)
    kr = kr_ref[0].astype(BF16)
    k = (jnp.dot(lat, wk_ref[...], preferred_element_type=F32)
         + jnp.dot(kr, sel_ref[...], preferred_element_type=F32))
    k_ref[0] = k.astype(BF16)
    vt = lax.dot_general(wvt_ref[...], lat, _NT, preferred_element_type=F32) + ones_ref[...]
    vt_ref[0] = vt.astype(BF16)


def _mla_expand_t(lat, kr, w, tm):
    bsz, L, _ = lat.shape
    row = lambda b, i: (b, i, 0)
    const = lambda b, i: (0, 0)
    hv = N_HEADS * V_ROWS
    return pl.pallas_call(
        _mla_expand_t_kernel,
        grid=(bsz, L // tm),
        in_specs=[pl.BlockSpec((1, tm, A_KV_LORA), row),
                  pl.BlockSpec((1, tm, A_ROPE), row),
                  pl.BlockSpec((A_KV_LORA, N_HEADS * LANES), const),
                  pl.BlockSpec((A_ROPE, N_HEADS * LANES), const),
                  pl.BlockSpec((hv, A_KV_LORA), const),
                  pl.BlockSpec((hv, 1), const)],
        out_specs=[pl.BlockSpec((1, tm, N_HEADS * LANES), row),
                   pl.BlockSpec((1, hv, tm), lambda b, i: (b, 0, i))],
        out_shape=[jax.ShapeDtypeStruct((bsz, L, N_HEADS * LANES), BF16),
                   jax.ShapeDtypeStruct((bsz, hv, L), BF16)],
        compiler_params=_params("parallel", "parallel"),
        name="mla_expand_t",
    )(lat, kr, w["a_wk"], w["a_sel"], w["a_wvt"], w["ones_rows"])


def _flash_kernel(*refs, tq, tk, qpos0, per_frame, wide, decay, nk_tiles):
    if decay:
        q_ref, k_ref, v_ref, qd_ref, kd_ref, o_ref, m_sc, l_sc, acc_sc = refs
    else:
        q_ref, k_ref, v_ref, o_ref, m_sc, l_sc, acc_sc = refs
    i = pl.program_id(2)
    q0 = qpos0 + i * tq
    if per_frame:
        vis_all = q0 + 1
        vis_any = q0 + tq
    else:
        vis_all = ((q0 >> CHUNK_SHIFT) + 1) * CHUNK
        vis_any = (((q0 + tq - 1) >> CHUNK_SHIFT) + 1) * CHUNK
    n_full = jnp.minimum(vis_all // tk, nk_tiles)
    n_tot = jnp.minimum((vis_any + tk - 1) // tk, nk_tiles)

    q = q_ref[0]
    lane = lax.broadcasted_iota(I32, (1, LANES), 1)
    first = lane < HEAD_DIM
    if wide:
        qs = (q[:, :LANES], q[:, LANES:])
    else:
        zero = jnp.zeros_like(q)
        qs = (jnp.where(first, q, zero), jnp.where(first, zero, q))

    m_sc[...] = jnp.full(m_sc.shape, NEG_INF, F32)
    l_sc[...] = jnp.zeros(l_sc.shape, F32)
    acc_sc[...] = jnp.zeros(acc_sc.shape, F32)
    qp = q0 + lax.broadcasted_iota(I32, (tq, 1), 0)

    def tile(j, masked):
        ks = pl.multiple_of(j * tk, tk)
        kt = k_ref[0, pl.ds(ks, tk), :]
        vt = v_ref[0, pl.ds(ks, tk), :]
        if masked:
            kp = ks + lax.broadcasted_iota(I32, (1, tk), 1)
            if per_frame:
                mask = kp <= qp
            else:
                mask = (kp >> CHUNK_SHIFT) <= (qp >> CHUNK_SHIFT)
        for hh in range(2):
            kk = kt[:, hh * LANES:(hh + 1) * LANES] if wide else kt
            s = lax.dot_general(qs[hh], kk, _NT, preferred_element_type=F32)
            if decay:
                s = s + (qd_ref[0, hh] - kd_ref[0, hh, :, pl.ds(ks, tk)])
            if masked:
                s = jnp.where(mask, s, NEG_INF)
            m_old = m_sc[hh]
            m_new = jnp.maximum(m_old, jnp.max(s, axis=1, keepdims=True))
            p = jnp.exp(s - m_new)
            alpha = jnp.exp(m_old - m_new)
            l_sc[hh] = alpha * l_sc[hh] + jnp.sum(p, axis=1, keepdims=True)
            acc_sc[hh] = alpha * acc_sc[hh] + jnp.dot(p.astype(BF16), vt, preferred_element_type=F32)
            m_sc[hh] = m_new

    def full_body(j, c):
        tile(j, False)
        return c

    def diag_body(j, c):
        tile(j, True)
        return c

    lax.fori_loop(0, n_full, full_body, 0)
    lax.fori_loop(n_full, n_tot, diag_body, 0)
    o0 = acc_sc[0] / l_sc[0]
    o1 = acc_sc[1] / l_sc[1]
    o_ref[0] = jnp.where(first, o0, o1).astype(BF16)


def _flash(q, k, v, *, tq, tk, qpos0, per_frame, wide, qd=None, kd=None):
    bsz, lq, _ = q.shape
    lk = k.shape[1]
    qw = 2 * LANES if wide else LANES
    decay = qd is not None
    in_specs = [pl.BlockSpec((1, tq, qw), lambda b, hp, i: (b, i, hp)),
                pl.BlockSpec((1, lk, qw), lambda b, hp, i: (b, 0, hp)),
                pl.BlockSpec((1, lk, LANES), lambda b, hp, i: (b, 0, hp))]
    args = [q, k, v]
    if decay:
        in_specs += [pl.BlockSpec((1, 2, tq, 1), lambda b, hp, i: (b, hp, i, 0)),
                     pl.BlockSpec((1, 2, 1, lk), lambda b, hp, i: (b, hp, 0, 0))]
        args += [qd, kd]
    kern = functools.partial(_flash_kernel, tq=tq, tk=tk, qpos0=qpos0, per_frame=per_frame,
                             wide=wide, decay=decay, nk_tiles=lk // tk)
    return pl.pallas_call(
        kern,
        grid=(bsz, N_HEADS // 2, lq // tq),
        in_specs=in_specs,
        out_specs=pl.BlockSpec((1, tq, LANES), lambda b, hp, i: (b, i, hp)),
        out_shape=jax.ShapeDtypeStruct((bsz, lq, N_HEADS * HEAD_DIM), BF16),
        scratch_shapes=[pltpu.VMEM((2, tq, 1), F32), pltpu.VMEM((2, tq, 1), F32),
                        pltpu.VMEM((2, tq, LANES), F32)],
        compiler_params=_params("parallel", "parallel", "arbitrary"),
        name="flash_attention",
    )(*args)


DECAY_K_ROW = HEAD_DIM
DECAY_Q_ROW = HEAD_DIM + 3


def _flash_t_kernel(*refs, tile, per_frame, decay, nh):
    if decay:
        qt_ref, k_ref, vt_ref, qd_ref, o_ref, m_sc, acc_sc, p_sc, alpha_sc = refs
    else:
        qt_ref, k_ref, vt_ref, o_ref, m_sc, acc_sc, p_sc, alpha_sc = refs
    heads = range(nh)
    i = pl.program_id(2)
    q0 = i * tile
    qt = qt_ref[0]
    qs = []
    for hh in heads:
        blk = qt[hh * LANES:(hh + 1) * LANES]
        if decay:
            hi, mid, lo = _split3(qd_ref[0, 0, hh:hh + 1, :])
            row = lax.broadcasted_iota(I32, (LANES, tile), 0)
            aug = jnp.where(row == DECAY_Q_ROW, hi, jnp.where(row == DECAY_Q_ROW + 1, mid,
                            jnp.where(row == DECAY_Q_ROW + 2, lo, 0.0)))
            aug = jnp.where((row >= DECAY_K_ROW) & (row < DECAY_Q_ROW), 1.0, aug)
            blk = (blk.astype(F32) + aug).astype(BF16)
        qs.append(blk)
    m_sc[...] = jnp.full(m_sc.shape, NEG_INF, F32)
    acc_sc[...] = jnp.zeros(acc_sc.shape, F32)
    qp = q0 + lax.broadcasted_iota(I32, (1, tile), 1)

    def scores(ks, masked):
        kt = k_ref[0, pl.ds(ks, tile), :]
        ss = [jnp.dot(kt[:, hh * LANES:(hh + 1) * LANES], qs[hh], preferred_element_type=F32) for hh in heads]
        if masked:
            kp = ks + lax.broadcasted_iota(I32, (tile, 1), 0)
            mask = (kp <= qp) if per_frame else ((kp >> CHUNK_SHIFT) <= (qp >> CHUNK_SHIFT))
            ss = [jnp.where(mask, s, NEG_INF) for s in ss]
        return ss

    def softmax(ss):
        m_old = m_sc[...]
        m_new = [jnp.maximum(m_old[hh:hh + 1], jnp.max(ss[hh], axis=0, keepdims=True)) for hh in heads]
        m_sc[...] = jnp.concatenate(m_new, axis=0)
        for hh in heads:
            p_sc[hh] = jnp.exp2(ss[hh] - m_new[hh]).astype(BF16)
        alpha_sc[...] = jnp.concatenate([jnp.exp2(m_old[hh:hh + 1] - m_new[hh]) for hh in heads], axis=0)

    def accumulate(jt):
        vt = vt_ref[0, :, pl.ds(pl.multiple_of(jt * tile, tile), tile)]
        alpha = alpha_sc[...]
        for hh in heads:
            pv = jnp.dot(vt[hh * V_ROWS:(hh + 1) * V_ROWS], p_sc[hh], preferred_element_type=F32)
            acc_sc[hh] = alpha[hh:hh + 1] * acc_sc[hh] + pv

    softmax(scores(pl.multiple_of(i * tile, tile), True))

    def body(j, j_prev):
        ss = scores(pl.multiple_of(j * tile, tile), False)
        accumulate(j_prev)
        softmax(ss)
        return j

    accumulate(lax.fori_loop(0, i, body, i))
    for hh in heads:
        a = acc_sc[hh]
        o_ref[0, hh * HEAD_DIM:(hh + 1) * HEAD_DIM, :] = (a[:HEAD_DIM] / a[HEAD_DIM:HEAD_DIM + 1]).astype(BF16)


def _flash_t(qt, k, vt, *, per_frame, qd=None):
    bsz, _, L = qt.shape
    tile = FLASH_T_TILE
    decay = qd is not None
    nh = FLASH_T_HEADS_DECAY if decay else FLASH_T_HEADS
    in_specs = [pl.BlockSpec((1, nh * LANES, tile), lambda b, hg, i: (b, hg, i)),
                pl.BlockSpec((1, L, nh * LANES), lambda b, hg, i: (b, 0, hg)),
                pl.BlockSpec((1, nh * V_ROWS, L), lambda b, hg, i: (b, hg, 0))]
    args = [qt, k, vt]
    if decay:
        in_specs.append(pl.BlockSpec((1, 1, nh, tile), lambda b, hg, i: (b, hg, 0, i)))
        args.append(qd.reshape(bsz, N_HEADS // nh, nh, L))
    return pl.pallas_call(
        functools.partial(_flash_t_kernel, tile=tile, per_frame=per_frame, decay=decay, nh=nh),
        grid=(bsz, N_HEADS // nh, L // tile),
        in_specs=in_specs,
        out_specs=pl.BlockSpec((1, nh * HEAD_DIM, tile), lambda b, hg, i: (b, hg, i)),
        out_shape=jax.ShapeDtypeStruct((bsz, N_HEADS * HEAD_DIM, L), BF16),
        scratch_shapes=[pltpu.VMEM((nh, tile), F32), pltpu.VMEM((nh, V_ROWS, tile), F32),
                        pltpu.VMEM((nh, tile, tile), BF16), pltpu.VMEM((nh, tile), F32)],
        compiler_params=_params("parallel", "parallel", "arbitrary"),
        name="flash_attention_t",
    )(*args)


def _fox_proj_kernel(x_ref, g_ref, sc_ref, sh_ref, wq_ref, wk_ref, wv_ref, wf_ref, bf_ref,
                     q_ref, k_ref, v_ref, kb_ref, vb_ref, lf_ref):
    h = _modnorm(x_ref[0], g_ref[...], sc_ref[0], sh_ref[0]).astype(BF16)
    q_ref[0] = (jnp.dot(h, wq_ref[...], preferred_element_type=F32) * HEAD_SCALE).astype(BF16)
    k = jnp.dot(h, wk_ref[...], preferred_element_type=F32)
    k_ref[0] = k
    kb_ref[0] = k.astype(BF16)
    v = jnp.dot(h, wv_ref[...], preferred_element_type=F32)
    v_ref[0] = v
    vb_ref[0] = v.astype(BF16)
    f = jnp.dot(h, wf_ref[...], preferred_element_type=F32)[:, :N_HEADS] + bf_ref[...]
    lf_ref[0] = _log_sigmoid(f)


def _fox_proj(x, g, sc, sh, w, tm):
    bsz, L, d = x.shape
    hd = N_HEADS * HEAD_DIM
    row = lambda b, i: (b, i, 0)
    per_b = lambda b, i: (b, 0, 0)
    const = lambda b, i: (0, 0)
    return pl.pallas_call(
        _fox_proj_kernel,
        grid=(bsz, L // tm),
        in_specs=[pl.BlockSpec((1, tm, d), row),
                  pl.BlockSpec((1, d), const),
                  pl.BlockSpec((1, 1, d), per_b),
                  pl.BlockSpec((1, 1, d), per_b),
                  pl.BlockSpec((d, hd), const),
                  pl.BlockSpec((d, hd), const),
                  pl.BlockSpec((d, hd), const),
                  pl.BlockSpec((d, LANES), const),
                  pl.BlockSpec((1, N_HEADS), const)],
        out_specs=[pl.BlockSpec((1, tm, hd), row)] * 5 + [pl.BlockSpec((1, tm, N_HEADS), row)],
        out_shape=[jax.ShapeDtypeStruct((bsz, L, hd), BF16),
                   jax.ShapeDtypeStruct((bsz, L, hd), F32),
                   jax.ShapeDtypeStruct((bsz, L, hd), F32),
                   jax.ShapeDtypeStruct((bsz, L, hd), BF16),
                   jax.ShapeDtypeStruct((bsz, L, hd), BF16),
                   jax.ShapeDtypeStruct((bsz, L, N_HEADS), F32)],
        compiler_params=_params("parallel", "parallel"),
        name="fox_proj",
    )(x, g, sc, sh, w["b_wq"], w["b_wk"], w["b_wv"], w["b_wf"], w["b_bf"])


def _cumsum_kernel(x_ref, o_ref, carry_ref, *, tc):
    @pl.when(pl.program_id(1) == 0)
    def _():
        carry_ref[...] = jnp.zeros(carry_ref.shape, F32)

    r = lax.broadcasted_iota(I32, (tc, tc), 0)
    c = lax.broadcasted_iota(I32, (tc, tc), 1)
    upper = (r <= c).astype(BF16)
    h1, h2, h3 = [a.astype(BF16) for a in _split3(x_ref[0])]
    d = lambda a: jnp.dot(a, upper, preferred_element_type=F32)
    cum = ((d(h3) + d(h2)) + d(h1)) + carry_ref[...]
    o_ref[0] = cum
    carry_ref[...] = cum[:, tc - 1:tc]


def _cumsum_rows(x, tc):
    bsz, nh, L = x.shape
    return pl.pallas_call(
        functools.partial(_cumsum_kernel, tc=tc),
        grid=(bsz, L // tc),
        in_specs=[pl.BlockSpec((1, nh, tc), lambda b, j: (b, 0, j))],
        out_specs=pl.BlockSpec((1, nh, tc), lambda b, j: (b, 0, j)),
        out_shape=jax.ShapeDtypeStruct((bsz, nh, L), F32),
        scratch_shapes=[pltpu.VMEM((nh, 1), F32)],
        compiler_params=_params("parallel", "arbitrary"),
        name="cumsum",
    )(x)


def _fox_proj_t_kernel(x_ref, g_ref, sc_ref, sh_ref, wqt_ref, wk_ref, wv_ref, wvt_ref, ones_ref,
                       wf_ref, bf_ref, wft_ref, bft_ref, qt_ref, k_ref, v_ref, vt_ref, lf_ref, lft_ref):
    h = _modnorm(x_ref[0], g_ref[...], sc_ref[0], sh_ref[0]).astype(BF16)
    qt = lax.dot_general(wqt_ref[...], h, _NT, preferred_element_type=F32)
    qt_ref[0] = (qt * (HEAD_SCALE * LOG2E)).astype(BF16)
    k_ref[0] = jnp.dot(h, wk_ref[...], preferred_element_type=F32)
    v_ref[0] = jnp.dot(h, wv_ref[...], preferred_element_type=F32)
    vt = lax.dot_general(wvt_ref[...], h, _NT, preferred_element_type=F32) + ones_ref[...]
    vt_ref[0] = vt.astype(BF16)
    f = jnp.dot(h, wf_ref[...], preferred_element_type=F32)[:, :N_HEADS] + bf_ref[...]
    lf_ref[0] = _log_sigmoid(f)
    ft = lax.dot_general(wft_ref[...], h, _NT, preferred_element_type=F32)[:N_HEADS] + bft_ref[...]
    lft_ref[0] = _log_sigmoid(ft)


def _fox_proj_t(x, g, sc, sh, w, tm):
    bsz, L, d = x.shape
    hd = N_HEADS * HEAD_DIM
    hq = N_HEADS * LANES
    hv = N_HEADS * V_ROWS
    row = lambda b, i: (b, i, 0)
    col = lambda b, i: (b, 0, i)
    per_b = lambda b, i: (b, 0, 0)
    const = lambda b, i: (0, 0)
    return pl.pallas_call(
        _fox_proj_t_kernel,
        grid=(bsz, L // tm),
        in_specs=[pl.BlockSpec((1, tm, d), row),
                  pl.BlockSpec((1, d), const),
                  pl.BlockSpec((1, 1, d), per_b),
                  pl.BlockSpec((1, 1, d), per_b),
                  pl.BlockSpec((hq, d), const),
                  pl.BlockSpec((d, hd), const),
                  pl.BlockSpec((d, hd), const),
                  pl.BlockSpec((hv, d), const),
                  pl.BlockSpec((hv, 1), const),
                  pl.BlockSpec((d, LANES), const),
                  pl.BlockSpec((1, N_HEADS), const),
                  pl.BlockSpec((LANES, d), const),
                  pl.BlockSpec((N_HEADS, 1), const)],
        out_specs=[pl.BlockSpec((1, hq, tm), col), pl.BlockSpec((1, tm, hd), row), pl.BlockSpec((1, tm, hd), row),
                   pl.BlockSpec((1, hv, tm), col), pl.BlockSpec((1, tm, N_HEADS), row),
                   pl.BlockSpec((1, N_HEADS, tm), col)],
        out_shape=[jax.ShapeDtypeStruct((bsz, hq, L), BF16),
                   jax.ShapeDtypeStruct((bsz, L, hd), F32),
                   jax.ShapeDtypeStruct((bsz, L, hd), F32),
                   jax.ShapeDtypeStruct((bsz, hv, L), BF16),
                   jax.ShapeDtypeStruct((bsz, L, N_HEADS), F32),
                   jax.ShapeDtypeStruct((bsz, N_HEADS, L), F32)],
        compiler_params=_params("parallel", "parallel"),
        name="fox_proj_t",
    )(x, g, sc, sh, w["b_wqt"], w["b_wk"], w["b_wv"], w["b_wvt"], w["ones_rows"],
      w["b_wf"], w["b_bf"], w["b_wf"].T, w["b_bf"].T)


def _fox_cumaug_kernel(lft_ref, lf_ref, k_ref, cumt_ref, kaug_ref, crow_sc, ccol_sc, *, tc):
    @pl.when(pl.program_id(1) == 0)
    def _():
        crow_sc[...] = jnp.zeros(crow_sc.shape, F32)
        ccol_sc[...] = jnp.zeros(ccol_sc.shape, F32)

    r = lax.broadcasted_iota(I32, (tc, tc), 0)
    c = lax.broadcasted_iota(I32, (tc, tc), 1)
    upper = (r <= c).astype(BF16)
    lower = (c <= r).astype(BF16)
    xh, xm, xl = [a.astype(BF16) for a in _split3(lft_ref[0])]
    dr = lambda a: jnp.dot(a, upper, preferred_element_type=F32)
    cumt = ((dr(xl) + dr(xm)) + dr(xh)) + crow_sc[...]
    crow_sc[...] = cumt[:, tc - 1:tc]
    cumt_ref[0] = cumt * LOG2E
    yh, ym, yl = [a.astype(BF16) for a in _split3(lf_ref[0])]
    dc = lambda a: jnp.dot(lower, a, preferred_element_type=F32)
    cum = ((dc(yl) + dc(ym)) + dc(yh)) + ccol_sc[...]
    ccol_sc[...] = cum[tc - 1:tc, :]
    neg = cum * (-LOG2E)
    k = k_ref[0]
    lane = lax.broadcasted_iota(I32, (tc, LANES - HEAD_DIM), 1)
    for h in range(N_HEADS):
        hi, mid, lo = _split3(neg[:, h:h + 1])
        aug = jnp.where(lane == 0, hi, jnp.where(lane == 1, mid, jnp.where(lane == 2, lo, 0.0)))
        aug = jnp.where((lane >= DECAY_Q_ROW - HEAD_DIM) & (lane < DECAY_Q_ROW - HEAD_DIM + 3), 1.0, aug)
        kaug_ref[0, :, h * LANES:(h + 1) * LANES] = jnp.concatenate(
            [k[:, h * HEAD_DIM:(h + 1) * HEAD_DIM], aug], axis=1).astype(BF16)


def _fox_cumaug(lft, lf, k, tc):
    bsz, nh, L = lft.shape
    hd = k.shape[2]
    return pl.pallas_call(
        functools.partial(_fox_cumaug_kernel, tc=tc),
        grid=(bsz, L // tc),
        in_specs=[pl.BlockSpec((1, nh, tc), lambda b, j: (b, 0, j)),
                  pl.BlockSpec((1, tc, nh), lambda b, j: (b, j, 0)),
                  pl.BlockSpec((1, tc, hd), lambda b, j: (b, j, 0))],
        out_specs=[pl.BlockSpec((1, nh, tc), lambda b, j: (b, 0, j)),
                   pl.BlockSpec((1, tc, nh * LANES), lambda b, j: (b, j, 0))],
        out_shape=[jax.ShapeDtypeStruct((bsz, nh, L), F32),
                   jax.ShapeDtypeStruct((bsz, L, nh * LANES), BF16)],
        scratch_shapes=[pltpu.VMEM((nh, 1), F32), pltpu.VMEM((1, nh), F32)],
        compiler_params=_params("parallel", "arbitrary"),
        name="fox_cumsum_aug",
    )(lft, lf, k)


def _dsa_proj_kernel(x_ref, g_ref, sc_ref, sh_ref, wqt_ref, wqit_ref, wsm_ref, wsmt_ref,
                     qt_ref, qit_ref, k_ref, v_ref, ki_ref, wit_ref, kaug_ref, vt_ref, kib_ref):
    h = _modnorm(x_ref[0], g_ref[...], sc_ref[0], sh_ref[0]).astype(BF16)
    tm = h.shape[0]
    qt = lax.dot_general(wqt_ref[...], h, _NT, preferred_element_type=F32)
    qt_ref[0] = (qt * (HEAD_SCALE * LOG2E)).astype(BF16)
    qit = lax.dot_general(wqit_ref[...], h, _NT, preferred_element_type=F32)
    qit_ref[0] = (qit * (C_IDX_DIM ** -0.5)).astype(BF16)
    sm = jnp.dot(h, wsm_ref[...], preferred_element_type=F32)
    smt = lax.dot_general(wsmt_ref[...], h, _NT, preferred_element_type=F32)
    k = sm[:, :HEAD_DIM]
    ki = sm[:, 2 * HEAD_DIM:2 * HEAD_DIM + C_IDX_DIM]
    k_ref[0] = k
    v_ref[0] = sm[:, HEAD_DIM:2 * HEAD_DIM]
    ki_ref[0] = ki
    kib_ref[0] = ki.astype(BF16)
    lane = lax.broadcasted_iota(I32, (tm, LANES - HEAD_DIM), 1)
    ones_cols = jnp.where(lane < 2, 1.0, 0.0)
    kaug_ref[0] = jnp.concatenate([k, ones_cols], axis=1).astype(BF16)
    row = lax.broadcasted_iota(I32, (V_ROWS - HEAD_DIM, tm), 0)
    ones_row = jnp.where(row == 0, 1.0, 0.0)
    vt_ref[0] = jnp.concatenate([smt[HEAD_DIM:2 * HEAD_DIM], ones_row], axis=0).astype(BF16)
    o = 2 * HEAD_DIM + C_IDX_DIM
    wit_ref[0] = smt[o:o + C_IDX_HEADS] * (C_IDX_HEADS ** -0.5)


def _dsa_proj(x, g, sc, sh, w, tm):
    bsz, L, d = x.shape
    hi = C_IDX_HEADS * C_IDX_DIM
    hq = N_HEADS * LANES
    row = lambda b, i: (b, i, 0)
    col = lambda b, i: (b, 0, i)
    per_b = lambda b, i: (b, 0, 0)
    const = lambda b, i: (0, 0)
    small = lambda n, dt: jax.ShapeDtypeStruct((bsz, L, n), dt)
    tall = lambda n, dt: jax.ShapeDtypeStruct((bsz, n, L), dt)
    return pl.pallas_call(
        _dsa_proj_kernel,
        grid=(bsz, L // tm),
        in_specs=[pl.BlockSpec((1, tm, d), row),
                  pl.BlockSpec((1, d), const),
                  pl.BlockSpec((1, 1, d), per_b),
                  pl.BlockSpec((1, 1, d), per_b),
                  pl.BlockSpec((hq, d), const),
                  pl.BlockSpec((hi, d), const),
                  pl.BlockSpec((d, 2 * LANES), const),
                  pl.BlockSpec((2 * LANES, d), const)],
        out_specs=[pl.BlockSpec((1, hq, tm), col), pl.BlockSpec((1, hi, tm), col),
                   pl.BlockSpec((1, tm, HEAD_DIM), row), pl.BlockSpec((1, tm, HEAD_DIM), row),
                   pl.BlockSpec((1, tm, C_IDX_DIM), row), pl.BlockSpec((1, C_IDX_HEADS, tm), col),
                   pl.BlockSpec((1, tm, LANES), row), pl.BlockSpec((1, V_ROWS, tm), col),
                   pl.BlockSpec((1, tm, C_IDX_DIM), row)],
        out_shape=[tall(hq, BF16), tall(hi, BF16), small(HEAD_DIM, F32), small(HEAD_DIM, F32),
                   small(C_IDX_DIM, F32), tall(C_IDX_HEADS, F32),
                   small(LANES, BF16), tall(V_ROWS, BF16), small(C_IDX_DIM, BF16)],
        compiler_params=_params("parallel", "parallel"),
        name="dsa_proj",
    )(x, g, sc, sh, w["c_wqt"], w["c_wqit"], w["c_wsm"], w["c_wsmt"])


_INT_MIN = -2 ** 31
_COUNT_ROWS = 64


def _dsa_kernel(qt_ref, qit_ref, wit_ref, k_ref, vt_ref, ki_ref, farq_ref, nb_ref, o_ref,
                sk_sc, m_sc, acc_sc, p_sc, alpha_sc, *, qpos0, n_sel, nk_tiles):
    tq, tk = DSA_TQ, DSA_TK
    i = pl.program_id(1)
    q0 = qpos0 + i * tq
    nt = jnp.minimum((q0 + tq + tk - 1) // tk, nk_tiles)
    qch = (q0 + lax.broadcasted_iota(I32, (1, tq), 1)) >> CHUNK_SHIFT

    def admissible(ks, w):
        kp = ks + lax.broadcasted_iota(I32, (w, 1), 0)
        return (kp >> CHUNK_SHIFT) <= qch

    qit = qit_ref[0]
    qis = jnp.concatenate([qit[h * C_IDX_DIM:(h + 1) * C_IDX_DIM] for h in range(C_IDX_HEADS)], axis=1)
    wit = wit_ref[0]

    def score_body(j, c):
        ks = pl.multiple_of(j * tk, tk)
        d = jnp.dot(ki_ref[0, pl.ds(ks, tk), :], qis, preferred_element_type=F32)
        sc = jnp.zeros((tk, tq), F32)
        for h in range(C_IDX_HEADS):
            sc = sc + jnp.maximum(d[:, h * tq:(h + 1) * tq], 0.0) * wit[h:h + 1]
        sc = jnp.where(sc == 0.0, 0.0, sc)
        sc = jnp.where(admissible(ks, tk), sc, NEG_INF)
        bits = pltpu.bitcast(sc, I32)
        sk_sc[pl.ds(ks, tk), :] = bits ^ ((bits >> 31) & 0x7FFFFFFF)
        return c

    lax.fori_loop(0, nt, score_body, 0)
    tiles_per_step = DSA_SEARCH_ROWS // tk
    n_steps = (nt + tiles_per_step - 1) // tiles_per_step

    @pl.when(nt < n_steps * tiles_per_step)
    def _():
        sk_sc[pl.ds(pl.multiple_of(nt * tk, tk), tk), :] = jnp.full((tk, tq), _INT_MIN, I32)

    def count(pred):
        def body(j, c):
            kt = sk_sc[pl.ds(pl.multiple_of(j * DSA_SEARCH_ROWS, DSA_SEARCH_ROWS), DSA_SEARCH_ROWS), :]
            g = jnp.where(pred(kt), 1.0, 0.0)
            parts = [g[r * _COUNT_ROWS:(r + 1) * _COUNT_ROWS] for r in range(DSA_SEARCH_ROWS // _COUNT_ROWS)]
            while len(parts) > 1:
                parts = [parts[a] + parts[a + 1] for a in range(0, len(parts), 2)]
            return c + parts[0]
        c = lax.fori_loop(0, n_steps, body, jnp.zeros((_COUNT_ROWS, tq), F32))
        return jnp.sum(c, axis=0, keepdims=True)

    nsel = float(n_sel)
    lo = jnp.where(count(lambda kt: kt >= 0) >= nsel, 0, _INT_MIN).astype(I32)

    def bit_body(t, lo):
        cand = lo | jnp.left_shift(jnp.int32(1), 30 - t)
        return jnp.where(count(lambda kt: kt >= cand) >= nsel, cand, lo)

    thr = lax.fori_loop(0, 31, bit_body, lo)
    need = nsel - count(lambda kt: kt > thr)

    qt = qt_ref[0]
    qs = jnp.concatenate([qt[h * LANES:(h + 1) * LANES] for h in range(N_HEADS)], axis=1) + farq_ref[...]
    m_sc[...] = jnp.full(m_sc.shape, NEG_INF, F32)
    acc_sc[...] = jnp.zeros(acc_sc.shape, F32)
    ra = lax.broadcasted_iota(I32, (tk, tk), 0)
    ca = lax.broadcasted_iota(I32, (tk, tk), 1)
    earlier = (ca < ra).astype(BF16)

    def select(ks, w, run):
        kt = sk_sc[pl.ds(ks, w), :]
        eq = kt == thr
        rank = run + jnp.dot(earlier[:w, :w], jnp.where(eq, 1.0, 0.0).astype(BF16), preferred_element_type=F32)
        sel = ((kt > thr) | (eq & (rank < need))) & admissible(ks, w)
        return sel, run + jnp.sum(jnp.where(eq, 1.0, 0.0), axis=0, keepdims=True)

    def scores(ks, w):
        return jnp.dot(k_ref[0, pl.ds(ks, w), :], qs, preferred_element_type=F32)

    def softmax(s, sel, w, kind):
        for h in range(N_HEADS):
            sl = slice(h * tq, (h + 1) * tq)
            sh = s[:, sl]
            if kind is not None:
                sh = sh + nb_ref[kind, :, sl]
            sh = jnp.where(sel, sh, NEG_INF)
            m_old = m_sc[:, sl]
            m_new = jnp.maximum(m_old, jnp.max(sh, axis=0, keepdims=True))
            m_sc[:, sl] = m_new
            alpha_sc[:, sl] = jnp.exp2(m_old - m_new)
            p_sc[0:w, sl] = jnp.exp2(sh - m_new).astype(BF16)

    def accumulate(ks, w):
        pv = jnp.dot(vt_ref[0, :, pl.ds(ks, w)], p_sc[0:w, :], preferred_element_type=F32)
        acc_sc[...] = alpha_sc[...] * acc_sc[...] + pv

    p_sc[...] = jnp.zeros(p_sc.shape, BF16)
    alpha_sc[...] = jnp.ones(alpha_sc.shape, F32)
    n_far = jnp.maximum(q0 - tq, 0) // tk

    def far_body(j, carry):
        j_prev, run = carry
        ks = pl.multiple_of(j * tk, tk)
        sel, run = select(ks, tk, run)
        s = scores(ks, tk)
        accumulate(pl.multiple_of(j_prev * tk, tk), tk)
        softmax(s, sel, tk, None)
        return j, run

    j_last, run = lax.fori_loop(0, n_far, far_body, (0, jnp.zeros((1, tq), F32)))
    accumulate(pl.multiple_of(j_last * tk, tk), tk)
    ks0 = n_far * tk
    n_tail = (q0 + tq - ks0) // tq

    def tail_body(t, run):
        ks = pl.multiple_of(ks0 + t * tq, tq)
        kind = jnp.clip((ks - q0) // tq + 2, 0, 2)
        sel, run = select(ks, tq, run)
        softmax(scores(ks, tq), sel, tq, kind)
        accumulate(ks, tq)
        return run

    lax.fori_loop(0, n_tail, tail_body, run)
    acc = acc_sc[...]
    ot = jnp.concatenate([acc[:HEAD_DIM, h * tq:(h + 1) * tq] / acc[HEAD_DIM:HEAD_DIM + 1, h * tq:(h + 1) * tq]
                          for h in range(N_HEADS)], axis=0)
    o_ref[0] = ot.T.astype(BF16)


def _dsa_attention(qt, qit, wit, k, vt, ki, farq, nb, *, qpos0, n_sel):
    bsz, hq, lq = qt.shape
    lk = k.shape[1]
    hi = qit.shape[1]
    hd = N_HEADS * HEAD_DIM
    col = lambda b, i: (b, 0, i)
    whole = lambda b, i: (b, 0, 0)
    kern = functools.partial(_dsa_kernel, qpos0=qpos0, n_sel=n_sel, nk_tiles=lk // DSA_TK)
    return pl.pallas_call(
        kern,
        grid=(bsz, lq // DSA_TQ),
        in_specs=[pl.BlockSpec((1, hq, DSA_TQ), col),
                  pl.BlockSpec((1, hi, DSA_TQ), col),
                  pl.BlockSpec((1, C_IDX_HEADS, DSA_TQ), col),
                  pl.BlockSpec((1, lk, LANES), whole),
                  pl.BlockSpec((1, V_ROWS, lk), whole),
                  pl.BlockSpec((1, lk, C_IDX_DIM), whole),
                  pl.BlockSpec((LANES, N_HEADS * DSA_TQ), lambda b, i: (0, 0)),
                  pl.BlockSpec((3, DSA_TQ, N_HEADS * DSA_TQ), lambda b, i: (0, 0, 0))],
        out_specs=pl.BlockSpec((1, DSA_TQ, hd), lambda b, i: (b, i, 0)),
        out_shape=jax.ShapeDtypeStruct((bsz, lq, hd), BF16),
        scratch_shapes=[pltpu.VMEM((lk, DSA_TQ), I32),
                        pltpu.VMEM((1, N_HEADS * DSA_TQ), F32),
                        pltpu.VMEM((V_ROWS, N_HEADS * DSA_TQ), F32),
                        pltpu.VMEM((DSA_TK, N_HEADS * DSA_TQ), BF16),
                        pltpu.VMEM((1, N_HEADS * DSA_TQ), F32)],
        compiler_params=_params("parallel", "arbitrary"),
        name="dsa_attention",
    )(qt, qit, wit, k, vt, ki, farq, nb)


SWA_TQ = DSA_TQ


def _swa_proj_kernel(x_ref, g_ref, sc_ref, sh_ref, wqt_ref, wkv_ref, wvt_ref, ones_ref,
                     qt_ref, kv_ref, kb_ref, vt_ref):
    h = _modnorm(x_ref[0], g_ref[...], sc_ref[0], sh_ref[0]).astype(BF16)
    qt = lax.dot_general(wqt_ref[...], h, _NT, preferred_element_type=F32)
    qt_ref[0] = (qt * (HEAD_SCALE * LOG2E)).astype(BF16)
    kv = jnp.dot(h, wkv_ref[...], preferred_element_type=F32)
    kv_ref[0] = kv
    kb_ref[0] = kv[:, :D_KV_HEADS * HEAD_DIM].astype(BF16)
    vt = lax.dot_general(wvt_ref[...], h, _NT, preferred_element_type=F32) + ones_ref[...]
    vt_ref[0] = vt.astype(BF16)


def _swa_proj(x, g, sc, sh, w, tm):
    bsz, L, d = x.shape
    hd = N_HEADS * HEAD_DIM
    kw = D_KV_HEADS * HEAD_DIM
    vr = D_KV_HEADS * V_ROWS
    row = lambda b, i: (b, i, 0)
    col = lambda b, i: (b, 0, i)
    per_b = lambda b, i: (b, 0, 0)
    const = lambda b, i: (0, 0)
    return pl.pallas_call(
        _swa_proj_kernel,
        grid=(bsz, L // tm),
        in_specs=[pl.BlockSpec((1, tm, d), row),
                  pl.BlockSpec((1, d), const),
                  pl.BlockSpec((1, 1, d), per_b),
                  pl.BlockSpec((1, 1, d), per_b),
                  pl.BlockSpec((hd, d), const),
                  pl.BlockSpec((d, 2 * kw), const),
                  pl.BlockSpec((vr, d), const),
                  pl.BlockSpec((vr, 1), const)],
        out_specs=[pl.BlockSpec((1, hd, tm), col), pl.BlockSpec((1, tm, 2 * kw), row),
                   pl.BlockSpec((1, tm, kw), row), pl.BlockSpec((1, vr, tm), col)],
        out_shape=[jax.ShapeDtypeStruct((bsz, hd, L), BF16),
                   jax.ShapeDtypeStruct((bsz, L, 2 * kw), F32),
                   jax.ShapeDtypeStruct((bsz, L, kw), BF16),
                   jax.ShapeDtypeStruct((bsz, vr, L), BF16)],
        compiler_params=_params("parallel", "parallel"),
        name="swa_proj",
    )(x, g, sc, sh, w["d_wqt"], w["d_wkv"], w["d_wvt"], w["ones_rows"][:vr])


def _swa_kernel(qt_ref, kp_ref, ko_ref, vp_ref, vo_ref, nb_ref, sink_ref, o_ref, *, tile_off):
    tq = SWA_TQ
    it = pl.program_id(1) + tile_off
    r = lax.broadcasted_iota(I32, (tq, tq), 0)
    c = lax.broadcasted_iota(I32, (tq, tq), 1)
    qch = c >> CHUNK_SHIFT
    kch = r >> CHUNK_SHIFT
    nch = tq // CHUNK
    ok_prev = ((kch - nch) >= (qch - N_WIN_CHUNKS)) & ((it - 1) * tq + r >= 0)
    ok_own = kch <= qch
    qt = qt_ref[0]
    for g in range(D_KV_HEADS):
        qs = jnp.concatenate([qt[(g * D_REP + u) * HEAD_DIM:(g * D_REP + u + 1) * HEAD_DIM] for u in range(D_REP)],
                             axis=1)
        ksl = slice(g * HEAD_DIM, (g + 1) * HEAD_DIM)
        sp = jnp.dot(kp_ref[0][:, ksl], qs, preferred_element_type=F32)
        so = jnp.dot(ko_ref[0][:, ksl], qs, preferred_element_type=F32)
        pps, pos, ms = [], [], []
        for u in range(D_REP):
            hh = g * D_REP + u
            sl = slice(u * tq, (u + 1) * tq)
            hsl = slice(hh * tq, (hh + 1) * tq)
            a = jnp.where(ok_prev, sp[:, sl] + nb_ref[0, :, hsl], NEG_INF)
            b = jnp.where(ok_own, so[:, sl] + nb_ref[1, :, hsl], NEG_INF)
            m = jnp.maximum(jnp.maximum(jnp.max(a, axis=0, keepdims=True), jnp.max(b, axis=0, keepdims=True)),
                            sink_ref[:, hsl])
            pps.append(jnp.exp2(a - m).astype(BF16))
            pos.append(jnp.exp2(b - m).astype(BF16))
            ms.append(m)
        vsl = slice(g * V_ROWS, (g + 1) * V_ROWS)
        acc = (jnp.dot(vp_ref[0][vsl], jnp.concatenate(pps, axis=1), preferred_element_type=F32)
               + jnp.dot(vo_ref[0][vsl], jnp.concatenate(pos, axis=1), preferred_element_type=F32))
        for u in range(D_REP):
            hh = g * D_REP + u
            sl = slice(u * tq, (u + 1) * tq)
            den = acc[HEAD_DIM:HEAD_DIM + 1, sl] + jnp.exp2(sink_ref[:, hh * tq:(hh + 1) * tq] - ms[u])
            o_ref[0, hh * HEAD_DIM:(hh + 1) * HEAD_DIM, :] = (acc[:HEAD_DIM, sl] / den).astype(BF16)


def _swa_attention(qt, kb, vt, nb, sinks, *, tile_off):
    bsz, hd, lq = qt.shape
    kw = kb.shape[2]
    vr = vt.shape[1]
    tq = SWA_TQ
    prev = lambda i: jnp.maximum(i + tile_off - 1, 0)
    return pl.pallas_call(
        functools.partial(_swa_kernel, tile_off=tile_off),
        grid=(bsz, lq // tq),
        in_specs=[pl.BlockSpec((1, hd, tq), lambda b, i: (b, 0, i)),
                  pl.BlockSpec((1, tq, kw), lambda b, i: (b, prev(i), 0)),
                  pl.BlockSpec((1, tq, kw), lambda b, i: (b, i + tile_off, 0)),
                  pl.BlockSpec((1, vr, tq), lambda b, i: (b, 0, prev(i))),
                  pl.BlockSpec((1, vr, tq), lambda b, i: (b, 0, i + tile_off)),
                  pl.BlockSpec((2, tq, N_HEADS * tq), lambda b, i: (0, 0, 0)),
                  pl.BlockSpec((1, N_HEADS * tq), lambda b, i: (0, 0))],
        out_specs=pl.BlockSpec((1, hd, tq), lambda b, i: (b, 0, i)),
        out_shape=jax.ShapeDtypeStruct((bsz, hd, lq), BF16),
        compiler_params=_params("parallel", "parallel"),
        name="swa_attention",
    )(qt, kb, kb, vt, vt, nb, sinks)


def _route_rows(s, sb):
    n, m = N_GROUPS, EXPERTS_PER_GROUP
    grp = []
    for g in range(n):
        x = sb[g * m:(g + 1) * m]
        best = None
        for a in range(m):
            for b in range(a + 1, m):
                pair = x[a] + x[b]
                best = pair if best is None else jnp.maximum(best, pair)
        grp.append(best)
    chosen_g = []
    taken = None
    for g in range(n):
        is_g = None
        for o in range(g + 1, n):
            c = grp[g] >= grp[o]
            is_g = c if is_g is None else (is_g & c)
        if is_g is None:
            is_g = ~taken
        elif taken is not None:
            is_g = is_g & (~taken)
        taken = is_g if taken is None else (taken | is_g)
        chosen_g.append(is_g)
    picked = []
    for e in range(N_EXPERTS):
        g, a = divmod(e, m)
        beaten = jnp.zeros(sb[e].shape, F32)
        for b in range(m):
            if b == a:
                continue
            o = g * m + b
            wins = (sb[o] > sb[e]) | ((sb[o] == sb[e]) & (b < a))
            beaten = beaten + wins.astype(F32)
        picked.append(chosen_g[g] & (beaten < 2.0))
    tops = [jnp.where(picked[e], s[e], 0.0) for e in range(N_EXPERTS)]
    denom = tops[0]
    for e in range(1, N_EXPERTS):
        denom = denom + tops[e]
    return [t / denom for t in tops]


def _post_mix_kernel(o_ref, wo_ref, x_ref, gt_ref, g_ref, sc_ref, sh_ref, wr_ref, br_ref,
                     x1_ref, h2_ref, gates_ref, *, o_transposed):
    mixed = lax.dot_general(o_ref[0], wo_ref[...], _TN if o_transposed else _NN, preferred_element_type=F32)
    x1 = x_ref[0] + gt_ref[0] * mixed
    x1_ref[0] = x1
    h2 = _modnorm(x1, g_ref[...], sc_ref[0], sh_ref[0])
    h2_ref[0] = h2.astype(BF16)
    logits = _dot3(wr_ref[...], h2, _NT)
    s = _sigmoid(logits)
    sb = s + br_ref[...]
    rows = _route_rows([s[e:e + 1] for e in range(N_EXPERTS)], [sb[e:e + 1] for e in range(N_EXPERTS)])
    gates_ref[0] = jnp.concatenate(rows, axis=0)


def _post_mix(o, wo, x, gt, g, sc, sh, wr_t, br, tm, o_transposed=False):
    bsz, L, d = x.shape
    hd = wo.shape[0]
    o_spec = (pl.BlockSpec((1, hd, tm), lambda b, i: (b, 0, i)) if o_transposed
              else pl.BlockSpec((1, tm, hd), lambda b, i: (b, i, 0)))
    row = lambda b, i: (b, i, 0)
    per_b = lambda b, i: (b, 0, 0)
    const = lambda b, i: (0, 0)
    return pl.pallas_call(
        functools.partial(_post_mix_kernel, o_transposed=o_transposed),
        grid=(bsz, L // tm),
        in_specs=[o_spec,
                  pl.BlockSpec((hd, d), const),
                  pl.BlockSpec((1, tm, d), row),
                  pl.BlockSpec((1, 1, d), per_b),
                  pl.BlockSpec((1, d), const),
                  pl.BlockSpec((1, 1, d), per_b),
                  pl.BlockSpec((1, 1, d), per_b),
                  pl.BlockSpec((N_EXPERTS, d), const),
                  pl.BlockSpec((N_EXPERTS, 1), const)],
        out_specs=[pl.BlockSpec((1, tm, d), row), pl.BlockSpec((1, tm, d), row),
                   pl.BlockSpec((1, N_EXPERTS, tm), lambda b, i: (b, 0, i))],
        out_shape=[jax.ShapeDtypeStruct((bsz, L, d), F32),
                   jax.ShapeDtypeStruct((bsz, L, d), BF16),
                   jax.ShapeDtypeStruct((bsz, N_EXPERTS, L), F32)],
        compiler_params=_params("parallel", "parallel"),
        name="post_mix_route",
    )(o, wo, x, gt, g, sc, sh, wr_t, br)


def _moe_kernel(x_ref, h_ref, gates_ref, gt_ref, wgu_ref, wd_ref, o_ref, acc_sc):
    e = pl.program_id(2)

    @pl.when(e == 0)
    def _():
        acc_sc[...] = jnp.zeros(acc_sc.shape, F32)

    h = h_ref[0]
    gates = gates_ref[0]
    lane = lax.broadcasted_iota(I32, gates.shape, 1)
    gus = [jnp.dot(h, wgu_ref[u], preferred_element_type=F32) for u in range(MOE_EXPERTS_PER_STEP)]
    acts = [((gu[:, :D_EXPERT] * _sigmoid(gu[:, :D_EXPERT])) * gu[:, D_EXPERT:]).astype(BF16) for gu in gus]
    ys = [jnp.dot(acts[u], wd_ref[u], preferred_element_type=F32) for u in range(MOE_EXPERTS_PER_STEP)]
    tot = None
    for u in range(MOE_EXPERTS_PER_STEP):
        ge = jnp.sum(jnp.where(lane == e * MOE_EXPERTS_PER_STEP + u, gates, 0.0), axis=1, keepdims=True)
        tot = ge * ys[u] if tot is None else tot + ge * ys[u]
    acc_sc[...] += tot

    @pl.when(e == N_EXPERTS // MOE_EXPERTS_PER_STEP - 1)
    def _():
        o_ref[0] = x_ref[0] + gt_ref[0] * acc_sc[...]


def _moe(x, h, gates, gt, wgu, wd, tm):
    bsz, L, d = x.shape
    row = lambda b, i, e: (b, i, 0)
    return pl.pallas_call(
        _moe_kernel,
        grid=(bsz, L // tm, N_EXPERTS // MOE_EXPERTS_PER_STEP),
        in_specs=[pl.BlockSpec((1, tm, d), row),
                  pl.BlockSpec((1, tm, d), row),
                  pl.BlockSpec((1, tm, N_EXPERTS), row),
                  pl.BlockSpec((1, 1, d), lambda b, i, e: (b, 0, 0)),
                  pl.BlockSpec((MOE_EXPERTS_PER_STEP, d, 2 * D_EXPERT), lambda b, i, e: (e, 0, 0)),
                  pl.BlockSpec((MOE_EXPERTS_PER_STEP, D_EXPERT, d), lambda b, i, e: (e, 0, 0))],
        out_specs=pl.BlockSpec((1, tm, d), row),
        out_shape=jax.ShapeDtypeStruct((bsz, L, d), F32),
        scratch_shapes=[pltpu.VMEM((tm, d), F32)],
        compiler_params=_params("parallel", "parallel", "arbitrary"),
        name="moe_experts",
    )(x, h, gates, gt, wgu, wd)


def _final_norm_kernel(x_ref, g_ref, o_ref):
    o_ref[0] = _rms(x_ref[0]) * g_ref[...]


def _final_norm(x, g, tm):
    bsz, L, d = x.shape
    row = lambda b, i: (b, i, 0)
    return pl.pallas_call(
        _final_norm_kernel,
        grid=(bsz, L // tm),
        in_specs=[pl.BlockSpec((1, tm, d), row), pl.BlockSpec((1, d), lambda b, i: (0, 0))],
        out_specs=pl.BlockSpec((1, tm, d), row),
        out_shape=jax.ShapeDtypeStruct((bsz, L, d), F32),
        compiler_params=_params("parallel", "parallel"),
        name="final_norm",
    )(x, g)


def _rot_cols(w):
    half = w.shape[-1] // 2
    return jnp.concatenate([-w[..., half:], w[..., :half]], axis=-1)


def _head_pad(w2d, width):
    w3 = w2d.reshape(w2d.shape[0], N_HEADS, HEAD_DIM)
    return jnp.pad(w3, ((0, 0), (0, 0), (0, width - HEAD_DIM))).reshape(w2d.shape[0], N_HEADS * width)


def _prep_weights(a_w_in, a_g_q, a_w_uq, a_g_kv, a_w_ukv, a_w_o, b_w_in, b_b_f, b_w_o,
                  c_w_in, c_w_o, d_w_in, d_sinks, d_w_o, moe_w_router, moe_b_router,
                  moe_w_gate, moe_w_up, moe_w_down):
    w = {}
    hd = N_HEADS * HEAD_DIM
    kr = a_w_in[:, A_Q_LORA + A_KV_LORA:]
    w["a_in"] = jnp.concatenate([a_w_in, _rot_cols(kr)], axis=1).astype(BF16)
    w["a_gq"] = a_g_q.reshape(1, -1)
    w["a_gkv"] = a_g_kv.reshape(1, -1)
    uq = a_w_uq.reshape(A_Q_LORA, N_HEADS, A_NOPE + A_ROPE)
    pad = LANES - A_NOPE - A_ROPE
    zq = lambda n: jnp.zeros((A_Q_LORA, N_HEADS, n), F32)
    w["a_wq"] = jnp.concatenate([uq, zq(pad)], axis=-1).reshape(A_Q_LORA, N_HEADS * LANES).astype(BF16)
    w["a_wqs"] = jnp.concatenate([zq(A_NOPE), _rot_cols(uq[..., A_NOPE:]), zq(pad)],
                                 axis=-1).reshape(A_Q_LORA, N_HEADS * LANES).astype(BF16)
    ukv = a_w_ukv.reshape(A_KV_LORA, N_HEADS, A_NOPE + A_V)
    w["a_wk"] = jnp.concatenate([ukv[..., :A_NOPE], jnp.zeros((A_KV_LORA, N_HEADS, LANES - A_NOPE), F32)],
                                axis=-1).reshape(A_KV_LORA, N_HEADS * LANES).astype(BF16)
    w["a_wv"] = ukv[..., A_NOPE:].reshape(A_KV_LORA, N_HEADS * A_V).astype(BF16)
    w["a_wvt"] = _head_pad(ukv[..., A_NOPE:].reshape(A_KV_LORA, N_HEADS * A_V), V_ROWS).T.astype(BF16)
    sel = np.zeros((A_ROPE, N_HEADS, LANES), np.float32)
    for r in range(A_ROPE):
        sel[r, :, A_NOPE + r] = 1.0
    w["a_sel"] = jnp.asarray(sel.reshape(A_ROPE, N_HEADS * LANES), BF16)
    w["a_wo"] = a_w_o.astype(BF16)
    ones_rows = np.zeros((N_HEADS, V_ROWS, 1), np.float32)
    ones_rows[:, HEAD_DIM, 0] = 1.0
    w["ones_rows"] = jnp.asarray(ones_rows.reshape(N_HEADS * V_ROWS, 1))
    w["b_wq"] = b_w_in[:, :hd].astype(BF16)
    w["b_wk"] = b_w_in[:, hd:2 * hd].astype(BF16)
    w["b_wv"] = b_w_in[:, 2 * hd:3 * hd].astype(BF16)
    w["b_wf"] = jnp.pad(b_w_in[:, 3 * hd:], ((0, 0), (0, LANES - N_HEADS))).astype(BF16)
    w["b_bf"] = b_b_f.reshape(1, N_HEADS)
    w["b_wo"] = b_w_o.astype(BF16)
    w["b_wqt"] = _head_pad(b_w_in[:, :hd], LANES).T.astype(BF16)
    w["b_wvt"] = _head_pad(b_w_in[:, 2 * hd:3 * hd], V_ROWS).T.astype(BF16)
    hi = C_IDX_HEADS * C_IDX_DIM
    w["c_wqt"] = _head_pad(c_w_in[:, :hd], LANES).T.astype(BF16)
    kv = c_w_in[:, hd:hd + 2 * HEAD_DIM]
    qi = c_w_in[:, hd + 2 * HEAD_DIM:hd + 2 * HEAD_DIM + hi]
    rest = c_w_in[:, hd + 2 * HEAD_DIM + hi:]
    w["c_wqit"] = qi.T.astype(BF16)
    sm = jnp.concatenate([kv, rest], axis=1)
    w["c_wsm"] = jnp.pad(sm, ((0, 0), (0, 2 * LANES - sm.shape[1]))).astype(BF16)
    w["c_wsmt"] = w["c_wsm"].T
    w["c_wo"] = c_w_o.astype(BF16)
    kw = D_KV_HEADS * HEAD_DIM
    w["d_wqt"] = d_w_in[:, :hd].T.astype(BF16)
    w["d_wkv"] = d_w_in[:, hd:].astype(BF16)
    dv = d_w_in[:, hd + kw:].reshape(-1, D_KV_HEADS, HEAD_DIM)
    dv = jnp.pad(dv, ((0, 0), (0, 0), (0, V_ROWS - HEAD_DIM))).reshape(-1, D_KV_HEADS * V_ROWS)
    w["d_wvt"] = dv.T.astype(BF16)
    w["d_sinks"] = jnp.repeat(d_sinks.astype(F32) * LOG2E, SWA_TQ).reshape(1, N_HEADS * SWA_TQ)
    w["d_wo"] = d_w_o.astype(BF16)
    w["wr_t"] = moe_w_router.T
    w["br"] = moe_b_router.reshape(N_EXPERTS, 1)
    w["wgu"] = jnp.concatenate([moe_w_gate, moe_w_up], axis=-1).astype(BF16)
    w["wd"] = moe_w_down.astype(BF16)
    return w


def _rope_tables(pos):
    half = A_ROPE // 2
    inv = ROPE_THETA ** (-jnp.arange(half, dtype=F32) / half)
    ang = pos.astype(F32)[:, None] * inv[None, :]
    cos, sin = jnp.cos(ang), jnp.sin(ang)
    n = pos.shape[0]
    pad = LANES - A_NOPE - A_ROPE
    cosp = jnp.concatenate([jnp.ones((n, A_NOPE), F32), cos, cos, jnp.zeros((n, pad), F32)], axis=1)
    sinp = jnp.concatenate([jnp.zeros((n, A_NOPE), F32), sin, sin, jnp.zeros((n, pad), F32)], axis=1)
    return cosp, sinp


def _pad_rows(a, n):
    return jnp.pad(a, ((0, 0), (0, n - a.shape[1])) + ((0, 0),) * (a.ndim - 2))


def _round_up(n, m):
    return (n + m - 1) // m * m


def kernel(x_prompt, x_sample, c_prompt, c_sample, cache_a_latent, cache_a_krope, cache_b_k, cache_b_v, cache_b_logf, cache_c_k, cache_c_v, cache_c_kidx, cache_d_k, cache_d_v, w_ada, b_ada, g_mix, g_ffn, g_final, rel_bias, a_w_in, a_g_q, a_w_uq, a_g_kv, a_w_ukv, a_w_o, b_w_in, b_b_f, b_w_o, c_w_in, c_w_o, d_w_in, d_sinks, d_w_o, moe_w_router, moe_b_router, moe_w_gate, moe_w_up, moe_w_down):
    bp, S, d = x_prompt.shape
    bs, Ls, _ = x_sample.shape
    P = cache_a_latent.shape[1]
    depth = w_ada.shape[0]
    hd = N_HEADS * HEAD_DIM
    assert S % PROJ_ROWS == 0 and Ls == CHUNK and P % DSA_TQ == 0
    assert DSA_SEARCH_ROWS % DSA_TK == 0 and S % DSA_SEARCH_ROWS == 0
    assert S % FLASH_T_TILE == 0 and FLASH_T_TILE % CHUNK == 0 and S % FOX_PROJ_ROWS == 0
    tm_p, tm_s = PROJ_ROWS, Ls
    lk_s = _round_up(P + Ls, DSA_SEARCH_ROWS)

    w = _prep_weights(a_w_in, a_g_q, a_w_uq, a_g_kv, a_w_ukv, a_w_o, b_w_in, b_b_f, b_w_o,
                      c_w_in, c_w_o, d_w_in, d_sinks, d_w_o, moe_w_router, moe_b_router,
                      moe_w_gate, moe_w_up, moe_w_down)
    mod = _adaln(jnp.concatenate([c_prompt, c_sample], axis=0), w_ada, b_ada)
    farq, nb_dsa, nb_swa = _bias_tiles(rel_bias)

    def mods(i, lo, hi):
        return [mod[i, lo:hi, k * d:(k + 1) * d][:, None, :] for k in range(6)]

    xp, xs = x_prompt, x_sample
    outs = {}
    for i in range(depth):
        sh1_p, sc1_p, gt1_p, sh2_p, sc2_p, gt2_p = mods(i, 0, bp)
        sh1_s, sc1_s, gt1_s, sh2_s, sc2_s, gt2_s = mods(i, bp, bp + bs)
        g1 = g_mix[i].reshape(1, d)
        mixer = i % 4
        if mixer == 0:
            cos_p, sin_p = _rope_tables(jnp.arange(S))
            cos_s, sin_s = _rope_tables(P + jnp.arange(Ls))
            qt_p, lat_p, kr_p = _mla_proj_t(xp, g1, sc1_p, sh1_p, w, cos_p, sin_p, tm_p)
            k_p, vt_p = _mla_expand_t(lat_p, kr_p, w, tm_p)
            op = _flash_t(qt_p, k_p, vt_p, per_frame=False)
            q_s, lat_s, kr_s = _mla_proj(xs, g1, sc1_s, sh1_s, w, cos_s, sin_s, tm_s)
            lat_all = _pad_rows(jnp.concatenate([cache_a_latent, lat_s], axis=1), lk_s)
            kr_all = _pad_rows(jnp.concatenate([cache_a_krope, kr_s], axis=1), lk_s)
            k_s, v_s = _mla_expand(lat_all, kr_all, w, DSA_TK)
            os_ = _flash(q_s, k_s, v_s, tq=Ls, tk=lk_s, qpos0=P, per_frame=False, wide=True)
            outs["a"] = (lat_p, kr_p, lat_s, kr_s)
            wo = w["a_wo"]
        elif mixer == 1:
            qt_p, k_p, v_p, vt_p, lf_p, lft_p = _fox_proj_t(xp, g1, sc1_p, sh1_p, w, FOX_PROJ_ROWS)
            cumt_p, kaug_p = _fox_cumaug(lft_p, lf_p, k_p, CUMSUM_TILE)
            op = _flash_t(qt_p, kaug_p, vt_p, per_frame=True, qd=cumt_p)
            q_s, k_s, v_s, kb_s, vb_s, lf_s = _fox_proj(xs, g1, sc1_s, sh1_s, w, tm_s)
            lf_all = jnp.concatenate([cache_b_logf.astype(F32), lf_s], axis=1)
            cum_s = _cumsum_rows(_pad_rows(lf_all, lk_s).swapaxes(1, 2), CUMSUM_TILE)
            kb_all = _pad_rows(jnp.concatenate([cache_b_k.reshape(bs, P, hd).astype(BF16), kb_s], axis=1), lk_s)
            vb_all = _pad_rows(jnp.concatenate([cache_b_v.reshape(bs, P, hd).astype(BF16), vb_s], axis=1), lk_s)
            os_ = _flash(q_s, kb_all, vb_all, tq=Ls, tk=lk_s, qpos0=P, per_frame=True, wide=False,
                         qd=cum_s[:, :, P:P + Ls, None], kd=cum_s[:, :, None, :])
            shp = lambda a: a.reshape(a.shape[0], a.shape[1], N_HEADS, HEAD_DIM)
            outs["b"] = (shp(k_p), shp(v_p), lf_p, shp(k_s), shp(v_s), lf_s)
            wo = w["b_wo"]
        elif mixer == 2:
            qt_p, qit_p, k_p, v_p, ki_p, wit_p, kaug_p, vt_p, kib_p = _dsa_proj(xp, g1, sc1_p, sh1_p, w, tm_p)
            op = _dsa_attention(qt_p, qit_p, wit_p, kaug_p, vt_p, kib_p, farq, nb_dsa, qpos0=0,
                                n_sel=min(C_TOPK_MAX, S // 4))
            qt_s, qit_s, k_s, v_s, ki_s, wit_s, kaug_s, vt_s, kib_s = _dsa_proj(xs, g1, sc1_s, sh1_s, w, tm_s)
            pad_q = lambda a: jnp.pad(a, ((0, 0), (0, 0), (0, DSA_TQ - Ls)))
            ones_cols = jnp.zeros((bs, P, LANES - HEAD_DIM), BF16).at[:, :, :2].set(1.0)
            kaug_c = jnp.concatenate([cache_c_k.astype(BF16), ones_cols], axis=2)
            kaug_all = _pad_rows(jnp.concatenate([kaug_c, kaug_s], axis=1), lk_s)
            ones_row = jnp.zeros((bs, V_ROWS - HEAD_DIM, P), BF16).at[:, 0, :].set(1.0)
            vt_c = jnp.concatenate([jnp.swapaxes(cache_c_v, 1, 2).astype(BF16), ones_row], axis=1)
            vt_all = jnp.pad(jnp.concatenate([vt_c, vt_s], axis=2), ((0, 0), (0, 0), (0, lk_s - P - Ls)))
            ki_all = _pad_rows(jnp.concatenate([cache_c_kidx.astype(BF16), kib_s], axis=1), lk_s)
            os_ = _dsa_attention(pad_q(qt_s), pad_q(qit_s), pad_q(wit_s), kaug_all, vt_all, ki_all,
                                 farq, nb_dsa, qpos0=P, n_sel=min(C_TOPK_MAX, (P + Ls) // 4))[:, :Ls]
            outs["c"] = (k_p, v_p, ki_p, k_s, v_s, ki_s)
            wo = w["c_wo"]
        else:
            kvw = D_KV_HEADS * HEAD_DIM
            qt_p, kv_p, kb_p, vt_p = _swa_proj(xp, g1, sc1_p, sh1_p, w, tm_p)
            op = _swa_attention(qt_p, kb_p, vt_p, nb_swa, w["d_sinks"], tile_off=0)
            qt_s, kv_s, kb_s, vt_s = _swa_proj(xs, g1, sc1_s, sh1_s, w, tm_s)
            wc = cache_d_k.shape[1]
            assert wc == WINDOW == SWA_TQ
            ck = cache_d_k.reshape(bs, wc, kvw)
            cv = cache_d_v.reshape(bs, wc, kvw)
            kb_all = _pad_rows(jnp.concatenate([ck.astype(BF16), kb_s], axis=1), 2 * SWA_TQ)
            cvt = jnp.swapaxes(cache_d_v, 1, 3).swapaxes(1, 2)
            cvt = jnp.pad(cvt, ((0, 0), (0, 0), (0, V_ROWS - HEAD_DIM), (0, 0))).at[:, :, HEAD_DIM, :].set(1.0)
            vt_all = jnp.concatenate([cvt.reshape(bs, D_KV_HEADS * V_ROWS, wc).astype(BF16), vt_s], axis=2)
            vt_all = jnp.pad(vt_all, ((0, 0), (0, 0), (0, 2 * SWA_TQ - wc - Ls)))
            qt_s = jnp.pad(qt_s, ((0, 0), (0, 0), (0, SWA_TQ - Ls)))
            os_ = _swa_attention(qt_s, kb_all, vt_all, nb_swa, w["d_sinks"], tile_off=1)[:, :, :Ls]
            keep = min(WINDOW, S)
            shp = lambda a: a.reshape(a.shape[0], a.shape[1], D_KV_HEADS, HEAD_DIM)
            k_roll = jnp.concatenate([ck, kv_s[..., :kvw]], axis=1)[:, Ls:]
            v_roll = jnp.concatenate([cv, kv_s[..., kvw:]], axis=1)[:, Ls:]
            outs["d"] = (shp(kv_p[:, S - keep:, :kvw]), shp(kv_p[:, S - keep:, kvw:]), shp(k_roll), shp(v_roll))
            wo = w["d_wo"]

        g2 = g_ffn[i].reshape(1, d)
        x1_p, h2_p, gates_p = _post_mix(op, wo, xp, gt1_p, g2, sc2_p, sh2_p, w["wr_t"], w["br"], tm_p,
                                        o_transposed=(mixer != 2))
        xp = _moe(x1_p, h2_p, jnp.swapaxes(gates_p, 1, 2), gt2_p, w["wgu"][i], w["wd"][i], tm_p)
        x1_s, h2_s, gates_s = _post_mix(os_, wo, xs, gt1_s, g2, sc2_s, sh2_s, w["wr_t"], w["br"], tm_s,
                                        o_transposed=(mixer == 3))
        xs = _moe(x1_s, h2_s, jnp.swapaxes(gates_s, 1, 2), gt2_s, w["wgu"][i], w["wd"][i], tm_s)

    gf = g_final.reshape(1, d)
    y_p = _final_norm(xp, gf, tm_p)
    y_s = _final_norm(xs, gf, tm_s)
    a_lat_p, a_kr_p, a_lat_s, a_kr_s = outs["a"]
    b_k_p, b_v_p, b_lf_p, b_k_s, b_v_s, b_lf_s = outs["b"]
    c_k_p, c_v_p, c_ki_p, c_k_s, c_v_s, c_ki_s = outs["c"]
    d_k_p, d_v_p, d_k_s, d_v_s = outs["d"]
    return (y_p, y_s,
            a_lat_p, a_kr_p, b_k_p, b_v_p, b_lf_p, c_k_p, c_v_p, c_ki_p, d_k_p, d_v_p,
            a_lat_s, a_kr_s, b_k_s, b_v_s, b_lf_s, c_k_s, c_v_s, c_ki_s, d_k_s, d_v_s)
```

```python
import functools
import math

import jax
import jax.numpy as jnp
import numpy as np
from jax import lax
from jax.experimental import pallas as pl
from jax.experimental.pallas import tpu as pltpu

F32 = jnp.float32
BF16 = jnp.bfloat16
I32 = jnp.int32

CHUNK = 64
CHUNK_SHIFT = CHUNK.bit_length() - 1
NORM_EPS = 1e-6
NEG_INF = -1e30
LOG2E = math.log2(math.e)
N_HEADS = 16
HEAD_DIM = 64
HEAD_SCALE = HEAD_DIM ** -0.5
N_BUCKETS = 32
MAX_DISTANCE = 128
A_Q_LORA = 512
A_KV_LORA = 256
A_NOPE = 64
A_ROPE = 32
A_V = 64
A_SCALE = (A_NOPE + A_ROPE) ** -0.5
ROPE_THETA = 10000.0
C_IDX_HEADS = 8
C_IDX_DIM = 64
C_TOPK_MAX = 256
D_KV_HEADS = 2
D_REP = N_HEADS // D_KV_HEADS
WINDOW = 128
N_WIN_CHUNKS = WINDOW // CHUNK
N_EXPERTS = 16
N_GROUPS = 4
EXPERTS_PER_GROUP = N_EXPERTS // N_GROUPS
D_EXPERT = 256

LANES = 128
VMEM_LIMIT_BYTES = 56 * 1024 * 1024

PROJ_ROWS = 512
FLASH_TQ = 256
FLASH_TK = 512
DSA_TQ = 128
DSA_TK = 256
DSA_SEARCH_ROWS = 512
V_ROWS = HEAD_DIM + 16
CUMSUM_TILE = 256
FLASH_T_TILE = 512
FOX_PROJ_ROWS = 256
FLASH_T_HEADS = 4
FLASH_T_HEADS_DECAY = 2
MOE_EXPERTS_PER_STEP = 8

_NT = (((1,), (1,)), ((), ()))
_NN = (((1,), (0,)), ((), ()))
_TN = (((0,), (0,)), ((), ()))


def _params(*sem):
    return pltpu.CompilerParams(dimension_semantics=sem, vmem_limit_bytes=VMEM_LIMIT_BYTES)


def _split2(a):
    hi = a.astype(BF16)
    lo = (a - hi.astype(F32)).astype(BF16)
    return hi, lo


def _split3(c):
    hi = c.astype(BF16).astype(F32)
    r1 = c - hi
    mid = r1.astype(BF16).astype(F32)
    lo = (r1 - mid).astype(BF16).astype(F32)
    return hi, mid, lo


def _dot3(a, b, dims):
    ah, al = _split2(a)
    bh, bl = _split2(b)
    d = lambda x, y: lax.dot_general(x, y, dims, preferred_element_type=F32)
    return d(ah, bh) + (d(ah, bl) + d(al, bh))


def _rms(x):
    return x * lax.rsqrt(jnp.mean(x * x, axis=-1, keepdims=True) + NORM_EPS)


def _modnorm(x, g, sc, sh):
    return _rms(x) * g * (1.0 + sc) + sh


def _sigmoid(z):
    return 1.0 / (1.0 + jnp.exp(-z))


def _log_sigmoid(z):
    return jnp.minimum(z, 0.0) - jnp.log1p(jnp.exp(-jnp.abs(z)))


def _adaln_kernel(c_ref, w_ref, b_ref, o_ref):
    c = c_ref[...]
    s = c * _sigmoid(c)
    o_ref[0] = _dot3(s, w_ref[0], _NN) + b_ref[0]


def _adaln(c_all, w_ada, b_ada):
    depth, d, d6 = w_ada.shape
    bc = c_all.shape[0]
    return pl.pallas_call(
        _adaln_kernel,
        grid=(depth, d6 // d),
        in_specs=[pl.BlockSpec((bc, d), lambda i, j: (0, 0)),
                  pl.BlockSpec((1, d, d), lambda i, j: (i, 0, j)),
                  pl.BlockSpec((1, 1, d), lambda i, j: (i, 0, j))],
        out_specs=pl.BlockSpec((1, bc, d), lambda i, j: (i, 0, j)),
        out_shape=jax.ShapeDtypeStruct((depth, bc, d6), F32),
        compiler_params=_params("parallel", "parallel"),
        name="adaln",
    )(c_all, w_ada, b_ada.reshape(depth, 1, d6))


_FAR_THRESHOLDS = (12, 16, 23, 32, 46, 64, 91)
_FAR_BUCKET = N_BUCKETS // 2 - 1
FAR_HI_ROW = HEAD_DIM
FAR_LO_ROW = HEAD_DIM + 1


def _rel_bucket(rel):
    n = jnp.abs(rel)
    nb = N_BUCKETS // 2
    max_exact = nb // 2
    far = jnp.full(rel.shape, max_exact, I32)
    for t in _FAR_THRESHOLDS:
        far = far + (n >= t).astype(I32)
    return jnp.where(rel > 0, nb, 0) + jnp.where(n < max_exact, n, far)


def _bias_kernel(tab_ref, farq_ref, dsa_ref, swa_ref):
    def lookup(bucket, h):
        acc = jnp.zeros(bucket.shape, F32)
        for b in range(N_BUCKETS):
            acc = jnp.where(bucket == b, tab_ref[b, h], acc)
        return acc

    kr = lax.broadcasted_iota(I32, (DSA_TQ, DSA_TQ), 0)
    qc = lax.broadcasted_iota(I32, (DSA_TQ, DSA_TQ), 1)
    for t, off in enumerate((-2 * DSA_TQ, -DSA_TQ, 0)):
        bucket = _rel_bucket(kr - qc + off)
        for h in range(N_HEADS):
            far = tab_ref[_FAR_BUCKET, h]
            bias = lookup(bucket, h)
            dsa_ref[t, :, h * DSA_TQ:(h + 1) * DSA_TQ] = (bias - far) * LOG2E
            if t > 0:
                swa_ref[t - 1, :, h * DSA_TQ:(h + 1) * DSA_TQ] = bias * LOG2E
    row = lax.broadcasted_iota(I32, (LANES, DSA_TQ), 0)
    for h in range(N_HEADS):
        c = jnp.full((LANES, DSA_TQ), tab_ref[_FAR_BUCKET, h] * LOG2E, F32)
        hi = c.astype(BF16).astype(F32)
        lo = (c - hi).astype(BF16).astype(F32)
        blk = jnp.where(row == FAR_HI_ROW, hi, jnp.where(row == FAR_LO_ROW, lo, 0.0))
        farq_ref[:, h * DSA_TQ:(h + 1) * DSA_TQ] = blk.astype(BF16)


def _bias_tiles(rel_bias):
    vm = pl.BlockSpec(memory_space=pltpu.VMEM)
    return pl.pallas_call(
        _bias_kernel,
        in_specs=[pl.BlockSpec(memory_space=pltpu.SMEM)],
        out_specs=[vm, vm, vm],
        out_shape=[jax.ShapeDtypeStruct((LANES, N_HEADS * DSA_TQ), BF16),
                   jax.ShapeDtypeStruct((3, DSA_TQ, N_HEADS * DSA_TQ), F32),
                   jax.ShapeDtypeStruct((2, DSA_TQ, N_HEADS * DSA_TQ), F32)],
        compiler_params=pltpu.CompilerParams(vmem_limit_bytes=VMEM_LIMIT_BYTES),
        name="bias_tiles",
    )(rel_bias)


def _mla_in(x_ref, g_ref, sc_ref, sh_ref, win_ref, gq_ref, gkv_ref, cos_ref, sin_ref, lat_ref, kr_ref):
    h = _modnorm(x_ref[0], g_ref[...], sc_ref[0], sh_ref[0]).astype(BF16)
    hw = jnp.dot(h, win_ref[...], preferred_element_type=F32)
    o1 = A_Q_LORA
    o2 = o1 + A_KV_LORA
    o3 = o2 + A_ROPE
    lat_ref[0] = _rms(hw[:, o1:o2]) * gkv_ref[...]
    cosr = cos_ref[...][:, A_NOPE:A_NOPE + A_ROPE]
    sinr = sin_ref[...][:, A_NOPE:A_NOPE + A_ROPE]
    kr_ref[0] = hw[:, o2:o3] * cosr + hw[:, o3:o3 + A_ROPE] * sinr
    return (_rms(hw[:, :o1]) * gq_ref[...]).astype(BF16)


def _mla_proj_kernel(x_ref, g_ref, sc_ref, sh_ref, win_ref, gq_ref, gkv_ref, wq_ref, wqs_ref,
                     cos_ref, sin_ref, q_ref, lat_ref, kr_ref):
    cqn = _mla_in(x_ref, g_ref, sc_ref, sh_ref, win_ref, gq_ref, gkv_ref, cos_ref, sin_ref, lat_ref, kr_ref)
    cosp = cos_ref[...]
    sinp = sin_ref[...]
    a = jnp.dot(cqn, wq_ref[...], preferred_element_type=F32)
    b = jnp.dot(cqn, wqs_ref[...], preferred_element_type=F32)
    for hd in range(N_HEADS):
        sl = slice(hd * LANES, (hd + 1) * LANES)
        q_ref[0, :, sl] = ((a[:, sl] * cosp + b[:, sl] * sinp) * A_SCALE).astype(BF16)


def _mla_proj(x, g, sc, sh, w, cosp, sinp, tm):
    bsz, L, d = x.shape
    row = lambda b, i: (b, i, 0)
    per_b = lambda b, i: (b, 0, 0)
    const = lambda b, i: (0, 0)
    nin = w["a_in"].shape[1]
    return pl.pallas_call(
        _mla_proj_kernel,
        grid=(bsz, L // tm),
        in_specs=[pl.BlockSpec((1, tm, d), row),
                  pl.BlockSpec((1, d), const),
                  pl.BlockSpec((1, 1, d), per_b),
                  pl.BlockSpec((1, 1, d), per_b),
                  pl.BlockSpec((d, nin), const),
                  pl.BlockSpec((1, A_Q_LORA), const),
                  pl.BlockSpec((1, A_KV_LORA), const),
                  pl.BlockSpec((A_Q_LORA, N_HEADS * LANES), const),
                  pl.BlockSpec((A_Q_LORA, N_HEADS * LANES), const),
                  pl.BlockSpec((tm, LANES), lambda b, i: (i, 0)),
                  pl.BlockSpec((tm, LANES), lambda b, i: (i, 0))],
        out_specs=[pl.BlockSpec((1, tm, N_HEADS * LANES), row),
                   pl.BlockSpec((1, tm, A_KV_LORA), row),
                   pl.BlockSpec((1, tm, A_ROPE), row)],
        out_shape=[jax.ShapeDtypeStruct((bsz, L, N_HEADS * LANES), BF16),
                   jax.ShapeDtypeStruct((bsz, L, A_KV_LORA), F32),
                   jax.ShapeDtypeStruct((bsz, L, A_ROPE), F32)],
        compiler_params=_params("parallel", "parallel"),
        name="mla_proj",
    )(x, g, sc, sh, w["a_in"], w["a_gq"], w["a_gkv"], w["a_wq"], w["a_wqs"], cosp, sinp)


def _mla_expand_kernel(lat_ref, kr_ref, wk_ref, sel_ref, wv_ref, k_ref, v_ref):
    lat = lat_ref[0].astype(BF16)
    kr = kr_ref[0].astype(BF16)
    k = (jnp.dot(lat, wk_ref[...], preferred_element_type=F32)
         + jnp.dot(kr, sel_ref[...], preferred_element_type=F32))
    k_ref[0] = k.astype(BF16)
    v_ref[0] = jnp.dot(lat, wv_ref[...], preferred_element_type=F32).astype(BF16)


def _mla_expand(lat, kr, w, tm):
    bsz, L, _ = lat.shape
    row = lambda b, i: (b, i, 0)
    const = lambda b, i: (0, 0)
    return pl.pallas_call(
        _mla_expand_kernel,
        grid=(bsz, L // tm),
        in_specs=[pl.BlockSpec((1, tm, A_KV_LORA), row),
                  pl.BlockSpec((1, tm, A_ROPE), row),
                  pl.BlockSpec((A_KV_LORA, N_HEADS * LANES), const),
                  pl.BlockSpec((A_ROPE, N_HEADS * LANES), const),
                  pl.BlockSpec((A_KV_LORA, N_HEADS * A_V), const)],
        out_specs=[pl.BlockSpec((1, tm, N_HEADS * LANES), row),
                   pl.BlockSpec((1, tm, N_HEADS * A_V), row)],
        out_shape=[jax.ShapeDtypeStruct((bsz, L, N_HEADS * LANES), BF16),
                   jax.ShapeDtypeStruct((bsz, L, N_HEADS * A_V), BF16)],
        compiler_params=_params("parallel", "parallel"),
        name="mla_expand",
    )(lat, kr, w["a_wk"], w["a_sel"], w["a_wv"])


def _mla_proj_t_kernel(x_ref, g_ref, sc_ref, sh_ref, win_ref, gq_ref, gkv_ref, wqt_ref, wqst_ref,
                       cos_ref, sin_ref, cost_ref, sint_ref, qt_ref, lat_ref, kr_ref):
    cqn = _mla_in(x_ref, g_ref, sc_ref, sh_ref, win_ref, gq_ref, gkv_ref, cos_ref, sin_ref, lat_ref, kr_ref)
    a = lax.dot_general(wqt_ref[...], cqn, _NT, preferred_element_type=F32)
    b = lax.dot_general(wqst_ref[...], cqn, _NT, preferred_element_type=F32)
    cost = cost_ref[...]
    sint = sint_ref[...]
    for hd in range(N_HEADS):
        sl = slice(hd * LANES, (hd + 1) * LANES)
        qt_ref[0, sl, :] = ((a[sl] * cost + b[sl] * sint) * (A_SCALE * LOG2E)).astype(BF16)


def _mla_proj_t(x, g, sc, sh, w, cosp, sinp, tm):
    bsz, L, d = x.shape
    row = lambda b, i: (b, i, 0)
    per_b = lambda b, i: (b, 0, 0)
    const = lambda b, i: (0, 0)
    nin = w["a_in"].shape[1]
    hq = N_HEADS * LANES
    return pl.pallas_call(
        _mla_proj_t_kernel,
        grid=(bsz, L // tm),
        in_specs=[pl.BlockSpec((1, tm, d), row),
                  pl.BlockSpec((1, d), const),
                  pl.BlockSpec((1, 1, d), per_b),
                  pl.BlockSpec((1, 1, d), per_b),
                  pl.BlockSpec((d, nin), const),
                  pl.BlockSpec((1, A_Q_LORA), const),
                  pl.BlockSpec((1, A_KV_LORA), const),
                  pl.BlockSpec((hq, A_Q_LORA), const),
                  pl.BlockSpec((hq, A_Q_LORA), const),
                  pl.BlockSpec((tm, LANES), lambda b, i: (i, 0)),
                  pl.BlockSpec((tm, LANES), lambda b, i: (i, 0)),
                  pl.BlockSpec((LANES, tm), lambda b, i: (0, i)),
                  pl.BlockSpec((LANES, tm), lambda b, i: (0, i))],
        out_specs=[pl.BlockSpec((1, hq, tm), lambda b, i: (b, 0, i)),
                   pl.BlockSpec((1, tm, A_KV_LORA), row),
                   pl.BlockSpec((1, tm, A_ROPE), row)],
        out_shape=[jax.ShapeDtypeStruct((bsz, hq, L), BF16),
                   jax.ShapeDtypeStruct((bsz, L, A_KV_LORA), F32),
                   jax.ShapeDtypeStruct((bsz, L, A_ROPE), F32)],
        compiler_params=_params("parallel", "parallel"),
        name="mla_proj_t",
    )(x, g, sc, sh, w["a_in"], w["a_gq"], w["a_gkv"], w["a_wq"].T, w["a_wqs"].T, cosp, sinp, cosp.T, sinp.T)


def _mla_expand_t_kernel(lat_ref, kr_ref, wk_ref, sel_ref, wvt_ref, ones_ref, k_ref, vt_ref):
    lat = lat_ref[0].astype(BF16)
    kr = kr_ref[0].astype(BF16)
    k = (jnp.dot(lat, wk_ref[...], preferred_element_type=F32)
         + jnp.dot(kr, sel_ref[...], preferred_element_type=F32))
    k_ref[0] = k.astype(BF16)
    vt = lax.dot_general(wvt_ref[...], lat, _NT, preferred_element_type=F32) + ones_ref[...]
    vt_ref[0] = vt.astype(BF16)


def _mla_expand_t(lat, kr, w, tm):
    bsz, L, _ = lat.shape
    row = lambda b, i: (b, i, 0)
    const = lambda b, i: (0, 0)
    hv = N_HEADS * V_ROWS
    return pl.pallas_call(
        _mla_expand_t_kernel,
        grid=(bsz, L // tm),
        in_specs=[pl.BlockSpec((1, tm, A_KV_LORA), row),
                  pl.BlockSpec((1, tm, A_ROPE), row),
                  pl.BlockSpec((A_KV_LORA, N_HEADS * LANES), const),
                  pl.BlockSpec((A_ROPE, N_HEADS * LANES), const),
                  pl.BlockSpec((hv, A_KV_LORA), const),
                  pl.BlockSpec((hv, 1), const)],
        out_specs=[pl.BlockSpec((1, tm, N_HEADS * LANES), row),
                   pl.BlockSpec((1, hv, tm), lambda b, i: (b, 0, i))],
        out_shape=[jax.ShapeDtypeStruct((bsz, L, N_HEADS * LANES), BF16),
                   jax.ShapeDtypeStruct((bsz, hv, L), BF16)],
        compiler_params=_params("parallel", "parallel"),
        name="mla_expand_t",
    )(lat, kr, w["a_wk"], w["a_sel"], w["a_wvt"], w["ones_rows"])


def _flash_kernel(*refs, tq, tk, qpos0, per_frame, wide, decay, nk_tiles):
    if decay:
        q_ref, k_ref, v_ref, qd_ref, kd_ref, o_ref, m_sc, l_sc, acc_sc = refs
    else:
        q_ref, k_ref, v_ref, o_ref, m_sc, l_sc, acc_sc = refs
    i = pl.program_id(2)
    q0 = qpos0 + i * tq
    if per_frame:
        vis_all = q0 + 1
        vis_any = q0 + tq
    else:
        vis_all = ((q0 >> CHUNK_SHIFT) + 1) * CHUNK
        vis_any = (((q0 + tq - 1) >> CHUNK_SHIFT) + 1) * CHUNK
    n_full = jnp.minimum(vis_all // tk, nk_tiles)
    n_tot = jnp.minimum((vis_any + tk - 1) // tk, nk_tiles)

    q = q_ref[0]
    lane = lax.broadcasted_iota(I32, (1, LANES), 1)
    first = lane < HEAD_DIM
    if wide:
        qs = (q[:, :LANES], q[:, LANES:])
    else:
        zero = jnp.zeros_like(q)
        qs = (jnp.where(first, q, zero), jnp.where(first, zero, q))

    m_sc[...] = jnp.full(m_sc.shape, NEG_INF, F32)
    l_sc[...] = jnp.zeros(l_sc.shape, F32)
    acc_sc[...] = jnp.zeros(acc_sc.shape, F32)
    qp = q0 + lax.broadcasted_iota(I32, (tq, 1), 0)

    def tile(j, masked):
        ks = pl.multiple_of(j * tk, tk)
        kt = k_ref[0, pl.ds(ks, tk), :]
        vt = v_ref[0, pl.ds(ks, tk), :]
        if masked:
            kp = ks + lax.broadcasted_iota(I32, (1, tk), 1)
            if per_frame:
                mask = kp <= qp
            else:
                mask = (kp >> CHUNK_SHIFT) <= (qp >> CHUNK_SHIFT)
        for hh in range(2):
            kk = kt[:, hh * LANES:(hh + 1) * LANES] if wide else kt
            s = lax.dot_general(qs[hh], kk, _NT, preferred_element_type=F32)
            if decay:
                s = s + (qd_ref[0, hh] - kd_ref[0, hh, :, pl.ds(ks, tk)])
            if masked:
                s = jnp.where(mask, s, NEG_INF)
            m_old = m_sc[hh]
            m_new = jnp.maximum(m_old, jnp.max(s, axis=1, keepdims=True))
            p = jnp.exp(s - m_new)
            alpha = jnp.exp(m_old - m_new)
            l_sc[hh] = alpha * l_sc[hh] + jnp.sum(p, axis=1, keepdims=True)
            acc_sc[hh] = alpha * acc_sc[hh] + jnp.dot(p.astype(BF16), vt, preferred_element_type=F32)
            m_sc[hh] = m_new

    def full_body(j, c):
        tile(j, False)
        return c

    def diag_body(j, c):
        tile(j, True)
        return c

    lax.fori_loop(0, n_full, full_body, 0)
    lax.fori_loop(n_full, n_tot, diag_body, 0)
    o0 = acc_sc[0] / l_sc[0]
    o1 = acc_sc[1] / l_sc[1]
    o_ref[0] = jnp.where(first, o0, o1).astype(BF16)


def _flash(q, k, v, *, tq, tk, qpos0, per_frame, wide, qd=None, kd=None):
    bsz, lq, _ = q.shape
    lk = k.shape[1]
    qw = 2 * LANES if wide else LANES
    decay = qd is not None
    in_specs = [pl.BlockSpec((1, tq, qw), lambda b, hp, i: (b, i, hp)),
                pl.BlockSpec((1, lk, qw), lambda b, hp, i: (b, 0, hp)),
                pl.BlockSpec((1, lk, LANES), lambda b, hp, i: (b, 0, hp))]
    args = [q, k, v]
    if decay:
        in_specs += [pl.BlockSpec((1, 2, tq, 1), lambda b, hp, i: (b, hp, i, 0)),
                     pl.BlockSpec((1, 2, 1, lk), lambda b, hp, i: (b, hp, 0, 0))]
        args += [qd, kd]
    kern = functools.partial(_flash_kernel, tq=tq, tk=tk, qpos0=qpos0, per_frame=per_frame,
                             wide=wide, decay=decay, nk_tiles=lk // tk)
    return pl.pallas_call(
        kern,
        grid=(bsz, N_HEADS // 2, lq // tq),
        in_specs=in_specs,
        out_specs=pl.BlockSpec((1, tq, LANES), lambda b, hp, i: (b, i, hp)),
        out_shape=jax.ShapeDtypeStruct((bsz, lq, N_HEADS * HEAD_DIM), BF16),
        scratch_shapes=[pltpu.VMEM((2, tq, 1), F32), pltpu.VMEM((2, tq, 1), F32),
                        pltpu.VMEM((2, tq, LANES), F32)],
        compiler_params=_params("parallel", "parallel", "arbitrary"),
        name="flash_attention",
    )(*args)


DECAY_K_ROW = HEAD_DIM
DECAY_Q_ROW = HEAD_DIM + 3


def _flash_t_kernel(*refs, tile, per_frame, decay, nh):
    if decay:
        qt_ref, k_ref, vt_ref, qd_ref, o_ref, m_sc, acc_sc, p_sc, alpha_sc = refs
    else:
        qt_ref, k_ref, vt_ref, o_ref, m_sc, acc_sc, p_sc, alpha_sc = refs
    heads = range(nh)
    i = pl.program_id(2)
    q0 = i * tile
    qt = qt_ref[0]
    qs = []
    for hh in heads:
        blk = qt[hh * LANES:(hh + 1) * LANES]
        if decay:
            hi, mid, lo = _split3(qd_ref[0, 0, hh:hh + 1, :])
            row = lax.broadcasted_iota(I32, (LANES, tile), 0)
            aug = jnp.where(row == DECAY_Q_ROW, hi, jnp.where(row == DECAY_Q_ROW + 1, mid,
                            jnp.where(row == DECAY_Q_ROW + 2, lo, 0.0)))
            aug = jnp.where((row >= DECAY_K_ROW) & (row < DECAY_Q_ROW), 1.0, aug)
            blk = (blk.astype(F32) + aug).astype(BF16)
        qs.append(blk)
    m_sc[...] = jnp.full(m_sc.shape, NEG_INF, F32)
    acc_sc[...] = jnp.zeros(acc_sc.shape, F32)
    qp = q0 + lax.broadcasted_iota(I32, (1, tile), 1)

    def scores(ks, masked):
        kt = k_ref[0, pl.ds(ks, tile), :]
        ss = [jnp.dot(kt[:, hh * LANES:(hh + 1) * LANES], qs[hh], preferred_element_type=F32) for hh in heads]
        if masked:
            kp = ks + lax.broadcasted_iota(I32, (tile, 1), 0)
            mask = (kp <= qp) if per_frame else ((kp >> CHUNK_SHIFT) <= (qp >> CHUNK_SHIFT))
            ss = [jnp.where(mask, s, NEG_INF) for s in ss]
        return ss

    def softmax(ss):
        m_old = m_sc[...]
        m_new = [jnp.maximum(m_old[hh:hh + 1], jnp.max(ss[hh], axis=0, keepdims=True)) for hh in heads]
        m_sc[...] = jnp.concatenate(m_new, axis=0)
        for hh in heads:
            p_sc[hh] = jnp.exp2(ss[hh] - m_new[hh]).astype(BF16)
        alpha_sc[...] = jnp.concatenate([jnp.exp2(m_old[hh:hh + 1] - m_new[hh]) for hh in heads], axis=0)

    def accumulate(jt):
        vt = vt_ref[0, :, pl.ds(pl.multiple_of(jt * tile, tile), tile)]
        alpha = alpha_sc[...]
        for hh in heads:
            pv = jnp.dot(vt[hh * V_ROWS:(hh + 1) * V_ROWS], p_sc[hh], preferred_element_type=F32)
            acc_sc[hh] = alpha[hh:hh + 1] * acc_sc[hh] + pv

    softmax(scores(pl.multiple_of(i * tile, tile), True))

    def body(j, j_prev):
        ss = scores(pl.multiple_of(j * tile, tile), False)
        accumulate(j_prev)
        softmax(ss)
        return j

    accumulate(lax.fori_loop(0, i, body, i))
    for hh in heads:
        a = acc_sc[hh]
        o_ref[0, hh * HEAD_DIM:(hh + 1) * HEAD_DIM, :] = (a[:HEAD_DIM] / a[HEAD_DIM:HEAD_DIM + 1]).astype(BF16)


def _flash_t(qt, k, vt, *, per_frame, qd=None):
    bsz, _, L = qt.shape
    tile = FLASH_T_TILE
    decay = qd is not None
    nh = FLASH_T_HEADS_DECAY if decay else FLASH_T_HEADS
    in_specs = [pl.BlockSpec((1, nh * LANES, tile), lambda b, hg, i: (b, hg, i)),
                pl.BlockSpec((1, L, nh * LANES), lambda b, hg, i: (b, 0, hg)),
                pl.BlockSpec((1, nh * V_ROWS, L), lambda b, hg, i: (b, hg, 0))]
    args = [qt, k, vt]
    if decay:
        in_specs.append(pl.BlockSpec((1, 1, nh, tile), lambda b, hg, i: (b, hg, 0, i)))
        args.append(qd.reshape(bsz, N_HEADS // nh, nh, L))
    return pl.pallas_call(
        functools.partial(_flash_t_kernel, tile=tile, per_frame=per_frame, decay=decay, nh=nh),
        grid=(bsz, N_HEADS // nh, L // tile),
        in_specs=in_specs,
        out_specs=pl.BlockSpec((1, nh * HEAD_DIM, tile), lambda b, hg, i: (b, hg, i)),
        out_shape=jax.ShapeDtypeStruct((bsz, N_HEADS * HEAD_DIM, L), BF16),
        scratch_shapes=[pltpu.VMEM((nh, tile), F32), pltpu.VMEM((nh, V_ROWS, tile), F32),
                        pltpu.VMEM((nh, tile, tile), BF16), pltpu.VMEM((nh, tile), F32)],
        compiler_params=_params("parallel", "parallel", "arbitrary"),
        name="flash_attention_t",
    )(*args)


def _fox_proj_kernel(x_ref, g_ref, sc_ref, sh_ref, wq_ref, wk_ref, wv_ref, wf_ref, bf_ref,
                     q_ref, k_ref, v_ref, kb_ref, vb_ref, lf_ref):
    h = _modnorm(x_ref[0], g_ref[...], sc_ref[0], sh_ref[0]).astype(BF16)
    q_ref[0] = (jnp.dot(h, wq_ref[...], preferred_element_type=F32) * HEAD_SCALE).astype(BF16)
    k = jnp.dot(h, wk_ref[...], preferred_element_type=F32)
    k_ref[0] = k
    kb_ref[0] = k.astype(BF16)
    v = jnp.dot(h, wv_ref[...], preferred_element_type=F32)
    v_ref[0] = v
    vb_ref[0] = v.astype(BF16)
    f = jnp.dot(h, wf_ref[...], preferred_element_type=F32)[:, :N_HEADS] + bf_ref[...]
    lf_ref[0] = _log_sigmoid(f)


def _fox_proj(x, g, sc, sh, w, tm):
    bsz, L, d = x.shape
    hd = N_HEADS * HEAD_DIM
    row = lambda b, i: (b, i, 0)
    per_b = lambda b, i: (b, 0, 0)
    const = lambda b, i: (0, 0)
    return pl.pallas_call(
        _fox_proj_kernel,
        grid=(bsz, L // tm),
        in_specs=[pl.BlockSpec((1, tm, d), row),
                  pl.BlockSpec((1, d), const),
                  pl.BlockSpec((1, 1, d), per_b),
                  pl.BlockSpec((1, 1, d), per_b),
                  pl.BlockSpec((d, hd), const),
                  pl.BlockSpec((d, hd), const),
                  pl.BlockSpec((d, hd), const),
                  pl.BlockSpec((d, LANES), const),
                  pl.BlockSpec((1, N_HEADS), const)],
        out_specs=[pl.BlockSpec((1, tm, hd), row)] * 5 + [pl.BlockSpec((1, tm, N_HEADS), row)],
        out_shape=[jax.ShapeDtypeStruct((bsz, L, hd), BF16),
                   jax.ShapeDtypeStruct((bsz, L, hd), F32),
                   jax.ShapeDtypeStruct((bsz, L, hd), F32),
                   jax.ShapeDtypeStruct((bsz, L, hd), BF16),
                   jax.ShapeDtypeStruct((bsz, L, hd), BF16),
                   jax.ShapeDtypeStruct((bsz, L, N_HEADS), F32)],
        compiler_params=_params("parallel", "parallel"),
        name="fox_proj",
    )(x, g, sc, sh, w["b_wq"], w["b_wk"], w["b_wv"], w["b_wf"], w["b_bf"])


def _cumsum_kernel(x_ref, o_ref, carry_ref, *, tc):
    @pl.when(pl.program_id(1) == 0)
    def _():
        carry_ref[...] = jnp.zeros(carry_ref.shape, F32)

    r = lax.broadcasted_iota(I32, (tc, tc), 0)
    c = lax.broadcasted_iota(I32, (tc, tc), 1)
    upper = (r <= c).astype(BF16)
    h1, h2, h3 = [a.astype(BF16) for a in _split3(x_ref[0])]
    d = lambda a: jnp.dot(a, upper, preferred_element_type=F32)
    cum = ((d(h3) + d(h2)) + d(h1)) + carry_ref[...]
    o_ref[0] = cum
    carry_ref[...] = cum[:, tc - 1:tc]


def _cumsum_rows(x, tc):
    bsz, nh, L = x.shape
    return pl.pallas_call(
        functools.partial(_cumsum_kernel, tc=tc),
        grid=(bsz, L // tc),
        in_specs=[pl.BlockSpec((1, nh, tc), lambda b, j: (b, 0, j))],
        out_specs=pl.BlockSpec((1, nh, tc), lambda b, j: (b, 0, j)),
        out_shape=jax.ShapeDtypeStruct((bsz, nh, L), F32),
        scratch_shapes=[pltpu.VMEM((nh, 1), F32)],
        compiler_params=_params("parallel", "arbitrary"),
        name="cumsum",
    )(x)


def _fox_proj_t_kernel(x_ref, g_ref, sc_ref, sh_ref, wqt_ref, wk_ref, wv_ref, wvt_ref, ones_ref,
                       wf_ref, bf_ref, wft_ref, bft_ref, qt_ref, k_ref, v_ref, vt_ref, lf_ref, lft_ref):
    h = _modnorm(x_ref[0], g_ref[...], sc_ref[0], sh_ref[0]).astype(BF16)
    qt = lax.dot_general(wqt_ref[...], h, _NT, preferred_element_type=F32)
    qt_ref[0] = (qt * (HEAD_SCALE * LOG2E)).astype(BF16)
    k_ref[0] = jnp.dot(h, wk_ref[...], preferred_element_type=F32)
    v_ref[0] = jnp.dot(h, wv_ref[...], preferred_element_type=F32)
    vt = lax.dot_general(wvt_ref[...], h, _NT, preferred_element_type=F32) + ones_ref[...]
    vt_ref[0] = vt.astype(BF16)
    f = jnp.dot(h, wf_ref[...], preferred_element_type=F32)[:, :N_HEADS] + bf_ref[...]
    lf_ref[0] = _log_sigmoid(f)
    ft = lax.dot_general(wft_ref[...], h, _NT, preferred_element_type=F32)[:N_HEADS] + bft_ref[...]
    lft_ref[0] = _log_sigmoid(ft)


def _fox_proj_t(x, g, sc, sh, w, tm):
    bsz, L, d = x.shape
    hd = N_HEADS * HEAD_DIM
    hq = N_HEADS * LANES
    hv = N_HEADS * V_ROWS
    row = lambda b, i: (b, i, 0)
    col = lambda b, i: (b, 0, i)
    per_b = lambda b, i: (b, 0, 0)
    const = lambda b, i: (0, 0)
    return pl.pallas_call(
        _fox_proj_t_kernel,
        grid=(bsz, L // tm),
        in_specs=[pl.BlockSpec((1, tm, d), row),
                  pl.BlockSpec((1, d), const),
                  pl.BlockSpec((1, 1, d), per_b),
                  pl.BlockSpec((1, 1, d), per_b),
                  pl.BlockSpec((hq, d), const),
                  pl.BlockSpec((d, hd), const),
                  pl.BlockSpec((d, hd), const),
                  pl.BlockSpec((hv, d), const),
                  pl.BlockSpec((hv, 1), const),
                  pl.BlockSpec((d, LANES), const),
                  pl.BlockSpec((1, N_HEADS), const),
                  pl.BlockSpec((LANES, d), const),
                  pl.BlockSpec((N_HEADS, 1), const)],
        out_specs=[pl.BlockSpec((1, hq, tm), col), pl.BlockSpec((1, tm, hd), row), pl.BlockSpec((1, tm, hd), row),
                   pl.BlockSpec((1, hv, tm), col), pl.BlockSpec((1, tm, N_HEADS), row),
                   pl.BlockSpec((1, N_HEADS, tm), col)],
        out_shape=[jax.ShapeDtypeStruct((bsz, hq, L), BF16),
                   jax.ShapeDtypeStruct((bsz, L, hd), F32),
                   jax.ShapeDtypeStruct((bsz, L, hd), F32),
                   jax.ShapeDtypeStruct((bsz, hv, L), BF16),
                   jax.ShapeDtypeStruct((bsz, L, N_HEADS), F32),
                   jax.ShapeDtypeStruct((bsz, N_HEADS, L), F32)],
        compiler_params=_params("parallel", "parallel"),
        name="fox_proj_t",
    )(x, g, sc, sh, w["b_wqt"], w["b_wk"], w["b_wv"], w["b_wvt"], w["ones_rows"],
      w["b_wf"], w["b_bf"], w["b_wf"].T, w["b_bf"].T)


def _fox_cumaug_kernel(lft_ref, lf_ref, k_ref, cumt_ref, kaug_ref, crow_sc, ccol_sc, *, tc):
    @pl.when(pl.program_id(1) == 0)
    def _():
        crow_sc[...] = jnp.zeros(crow_sc.shape, F32)
        ccol_sc[...] = jnp.zeros(ccol_sc.shape, F32)

    r = lax.broadcasted_iota(I32, (tc, tc), 0)
    c = lax.broadcasted_iota(I32, (tc, tc), 1)
    upper = (r <= c).astype(BF16)
    lower = (c <= r).astype(BF16)
    xh, xm, xl = [a.astype(BF16) for a in _split3(lft_ref[0])]
    dr = lambda a: jnp.dot(a, upper, preferred_element_type=F32)
    cumt = ((dr(xl) + dr(xm)) + dr(xh)) + crow_sc[...]
    crow_sc[...] = cumt[:, tc - 1:tc]
    cumt_ref[0] = cumt * LOG2E
    yh, ym, yl = [a.astype(BF16) for a in _split3(lf_ref[0])]
    dc = lambda a: jnp.dot(lower, a, preferred_element_type=F32)
    cum = ((dc(yl) + dc(ym)) + dc(yh)) + ccol_sc[...]
    ccol_sc[...] = cum[tc - 1:tc, :]
    neg = cum * (-LOG2E)
    k = k_ref[0]
    lane = lax.broadcasted_iota(I32, (tc, LANES - HEAD_DIM), 1)
    for h in range(N_HEADS):
        hi, mid, lo = _split3(neg[:, h:h + 1])
        aug = jnp.where(lane == 0, hi, jnp.where(lane == 1, mid, jnp.where(lane == 2, lo, 0.0)))
        aug = jnp.where((lane >= DECAY_Q_ROW - HEAD_DIM) & (lane < DECAY_Q_ROW - HEAD_DIM + 3), 1.0, aug)
        kaug_ref[0, :, h * LANES:(h + 1) * LANES] = jnp.concatenate(
            [k[:, h * HEAD_DIM:(h + 1) * HEAD_DIM], aug], axis=1).astype(BF16)


def _fox_cumaug(lft, lf, k, tc):
    bsz, nh, L = lft.shape
    hd = k.shape[2]
    return pl.pallas_call(
        functools.partial(_fox_cumaug_kernel, tc=tc),
        grid=(bsz, L // tc),
        in_specs=[pl.BlockSpec((1, nh, tc), lambda b, j: (b, 0, j)),
                  pl.BlockSpec((1, tc, nh), lambda b, j: (b, j, 0)),
                  pl.BlockSpec((1, tc, hd), lambda b, j: (b, j, 0))],
        out_specs=[pl.BlockSpec((1, nh, tc), lambda b, j: (b, 0, j)),
                   pl.BlockSpec((1, tc, nh * LANES), lambda b, j: (b, j, 0))],
        out_shape=[jax.ShapeDtypeStruct((bsz, nh, L), F32),
                   jax.ShapeDtypeStruct((bsz, L, nh * LANES), BF16)],
        scratch_shapes=[pltpu.VMEM((nh, 1), F32), pltpu.VMEM((1, nh), F32)],
        compiler_params=_params("parallel", "arbitrary"),
        name="fox_cumsum_aug",
    )(lft, lf, k)


def _dsa_proj_kernel(x_ref, g_ref, sc_ref, sh_ref, wqt_ref, wqit_ref, wsm_ref, wsmt_ref,
                     qt_ref, qit_ref, k_ref, v_ref, ki_ref, wit_ref, kaug_ref, vt_ref, kib_ref):
    h = _modnorm(x_ref[0], g_ref[...], sc_ref[0], sh_ref[0]).astype(BF16)
    tm = h.shape[0]
    qt = lax.dot_general(wqt_ref[...], h, _NT, preferred_element_type=F32)
    qt_ref[0] = (qt * (HEAD_SCALE * LOG2E)).astype(BF16)
    qit = lax.dot_general(wqit_ref[...], h, _NT, preferred_element_type=F32)
    qit_ref[0] = (qit * (C_IDX_DIM ** -0.5)).astype(BF16)
    sm = jnp.dot(h, wsm_ref[...], preferred_element_type=F32)
    smt = lax.dot_general(wsmt_ref[...], h, _NT, preferred_element_type=F32)
    k = sm[:, :HEAD_DIM]
    ki = sm[:, 2 * HEAD_DIM:2 * HEAD_DIM + C_IDX_DIM]
    k_ref[0] = k
    v_ref[0] = sm[:, HEAD_DIM:2 * HEAD_DIM]
    ki_ref[0] = ki
    kib_ref[0] = ki.astype(BF16)
    lane = lax.broadcasted_iota(I32, (tm, LANES - HEAD_DIM), 1)
    ones_cols = jnp.where(lane < 2, 1.0, 0.0)
    kaug_ref[0] = jnp.concatenate([k, ones_cols], axis=1).astype(BF16)
    row = lax.broadcasted_iota(I32, (V_ROWS - HEAD_DIM, tm), 0)
    ones_row = jnp.where(row == 0, 1.0, 0.0)
    vt_ref[0] = jnp.concatenate([smt[HEAD_DIM:2 * HEAD_DIM], ones_row], axis=0).astype(BF16)
    o = 2 * HEAD_DIM + C_IDX_DIM
    wit_ref[0] = smt[o:o + C_IDX_HEADS] * (C_IDX_HEADS ** -0.5)


def _dsa_proj(x, g, sc, sh, w, tm):
    bsz, L, d = x.shape
    hi = C_IDX_HEADS * C_IDX_DIM
    hq = N_HEADS * LANES
    row = lambda b, i: (b, i, 0)
    col = lambda b, i: (b, 0, i)
    per_b = lambda b, i: (b, 0, 0)
    const = lambda b, i: (0, 0)
    small = lambda n, dt: jax.ShapeDtypeStruct((bsz, L, n), dt)
    tall = lambda n, dt: jax.ShapeDtypeStruct((bsz, n, L), dt)
    return pl.pallas_call(
        _dsa_proj_kernel,
        grid=(bsz, L // tm),
        in_specs=[pl.BlockSpec((1, tm, d), row),
                  pl.BlockSpec((1, d), const),
                  pl.BlockSpec((1, 1, d), per_b),
                  pl.BlockSpec((1, 1, d), per_b),
                  pl.BlockSpec((hq, d), const),
                  pl.BlockSpec((hi, d), const),
                  pl.BlockSpec((d, 2 * LANES), const),
                  pl.BlockSpec((2 * LANES, d), const)],
        out_specs=[pl.BlockSpec((1, hq, tm), col), pl.BlockSpec((1, hi, tm), col),
                   pl.BlockSpec((1, tm, HEAD_DIM), row), pl.BlockSpec((1, tm, HEAD_DIM), row),
                   pl.BlockSpec((1, tm, C_IDX_DIM), row), pl.BlockSpec((1, C_IDX_HEADS, tm), col),
                   pl.BlockSpec((1, tm, LANES), row), pl.BlockSpec((1, V_ROWS, tm), col),
                   pl.BlockSpec((1, tm, C_IDX_DIM), row)],
        out_shape=[tall(hq, BF16), tall(hi, BF16), small(HEAD_DIM, F32), small(HEAD_DIM, F32),
                   small(C_IDX_DIM, F32), tall(C_IDX_HEADS, F32),
                   small(LANES, BF16), tall(V_ROWS, BF16), small(C_IDX_DIM, BF16)],
        compiler_params=_params("parallel", "parallel"),
        name="dsa_proj",
    )(x, g, sc, sh, w["c_wqt"], w["c_wqit"], w["c_wsm"], w["c_wsmt"])


_INT_MIN = -2 ** 31
_COUNT_ROWS = 64


def _dsa_kernel(qt_ref, qit_ref, wit_ref, k_ref, vt_ref, ki_ref, farq_ref, nb_ref, o_ref,
                sk_sc, m_sc, acc_sc, p_sc, alpha_sc, *, qpos0, n_sel, nk_tiles):
    tq, tk, ts = DSA_TQ, DSA_TK, DSA_SEARCH_ROWS
    i = pl.program_id(1)
    q0 = qpos0 + i * tq
    n_steps = jnp.minimum((q0 + tq + ts - 1) // ts, nk_tiles * tk // ts)
    qch = (q0 + lax.broadcasted_iota(I32, (1, tq), 1)) >> CHUNK_SHIFT

    def admissible(ks, w):
        kp = ks + lax.broadcasted_iota(I32, (w, 1), 0)
        return (kp >> CHUNK_SHIFT) <= qch

    qit = qit_ref[0]
    qis = jnp.concatenate([qit[h * C_IDX_DIM:(h + 1) * C_IDX_DIM] for h in range(C_IDX_HEADS)], axis=1)
    wit = wit_ref[0]

    def score_body(j, c):
        ks = pl.multiple_of(j * ts, ts)
        d = jnp.dot(ki_ref[0, pl.ds(ks, ts), :], qis, preferred_element_type=F32)
        sc = jnp.zeros((ts, tq), F32)
        for h in range(C_IDX_HEADS):
            sc = sc + jnp.maximum(d[:, h * tq:(h + 1) * tq], 0.0) * wit[h:h + 1]
        sc = jnp.where(sc == 0.0, 0.0, sc)
        sc = jnp.where(admissible(ks, ts), sc, NEG_INF)
        bits = pltpu.bitcast(sc, I32)
        sk_sc[pl.ds(ks, ts), :] = bits ^ ((bits >> 31) & 0x7FFFFFFF)
        return c

    lax.fori_loop(0, n_steps, score_body, 0)

    def count(pred):
        def body(j, c):
            kt = sk_sc[pl.ds(pl.multiple_of(j * DSA_SEARCH_ROWS, DSA_SEARCH_ROWS), DSA_SEARCH_ROWS), :]
            g = jnp.where(pred(kt), 1.0, 0.0)
            parts = [g[r * _COUNT_ROWS:(r + 1) * _COUNT_ROWS] for r in range(DSA_SEARCH_ROWS // _COUNT_ROWS)]
            while len(parts) > 1:
                parts = [parts[a] + parts[a + 1] for a in range(0, len(parts), 2)]
            return c + parts[0]
        c = lax.fori_loop(0, n_steps, body, jnp.zeros((_COUNT_ROWS, tq), F32))
        return jnp.sum(c, axis=0, keepdims=True)

    nsel = float(n_sel)
    lo = jnp.where(count(lambda kt: kt >= 0) >= nsel, 0, _INT_MIN).astype(I32)

    def bit_body(t, lo):
        cand = lo | jnp.left_shift(jnp.int32(1), 30 - t)
        return jnp.where(count(lambda kt: kt >= cand) >= nsel, cand, lo)

    thr = lax.fori_loop(0, 31, bit_body, lo)
    need = nsel - count(lambda kt: kt > thr)

    qt = qt_ref[0]
    qs = jnp.concatenate([qt[h * LANES:(h + 1) * LANES] for h in range(N_HEADS)], axis=1) + farq_ref[...]
    m_sc[...] = jnp.full(m_sc.shape, NEG_INF, F32)
    acc_sc[...] = jnp.zeros(acc_sc.shape, F32)
    ra = lax.broadcasted_iota(I32, (tk, tk), 0)
    ca = lax.broadcasted_iota(I32, (tk, tk), 1)
    earlier = (ca < ra).astype(BF16)

    def select(ks, w, run):
        kt = sk_sc[pl.ds(ks, w), :]
        eq = kt == thr
        rank = run + jnp.dot(earlier[:w, :w], jnp.where(eq, 1.0, 0.0).astype(BF16), preferred_element_type=F32)
        sel = ((kt > thr) | (eq & (rank < need))) & admissible(ks, w)
        return sel, run + jnp.sum(jnp.where(eq, 1.0, 0.0), axis=0, keepdims=True)

    def scores(ks, w):
        return jnp.dot(k_ref[0, pl.ds(ks, w), :], qs, preferred_element_type=F32)

    def softmax(s, sel, w, kind):
        for h in range(N_HEADS):
            sl = slice(h * tq, (h + 1) * tq)
            sh = s[:, sl]
            if kind is not None:
                sh = sh + nb_ref[kind, :, sl]
            sh = jnp.where(sel, sh, NEG_INF)
            m_old = m_sc[:, sl]
            m_new = jnp.maximum(m_old, jnp.max(sh, axis=0, keepdims=True))
            m_sc[:, sl] = m_new
            alpha_sc[:, sl] = jnp.exp2(m_old - m_new)
            p_sc[0:w, sl] = jnp.exp2(sh - m_new).astype(BF16)

    def accumulate(ks, w):
        pv = jnp.dot(vt_ref[0, :, pl.ds(ks, w)], p_sc[0:w, :], preferred_element_type=F32)
        acc_sc[...] = alpha_sc[...] * acc_sc[...] + pv

    p_sc[...] = jnp.zeros(p_sc.shape, BF16)
    alpha_sc[...] = jnp.ones(alpha_sc.shape, F32)
    n_far = jnp.maximum(q0 - tq, 0) // tk

    def far_body(j, carry):
        j_prev, run = carry
        ks = pl.multiple_of(j * tk, tk)
        sel, run = select(ks, tk, run)
        s = scores(ks, tk)
        accumulate(pl.multiple_of(j_prev * tk, tk), tk)
        softmax(s, sel, tk, None)
        return j, run

    j_last, run = lax.fori_loop(0, n_far, far_body, (0, jnp.zeros((1, tq), F32)))
    accumulate(pl.multiple_of(j_last * tk, tk), tk)
    ks0 = n_far * tk
    n_tail = (q0 + tq - ks0) // tq

    def tail_body(t, run):
        ks = pl.multiple_of(ks0 + t * tq, tq)
        kind = jnp.clip((ks - q0) // tq + 2, 0, 2)
        sel, run = select(ks, tq, run)
        softmax(scores(ks, tq), sel, tq, kind)
        accumulate(ks, tq)
        return run

    lax.fori_loop(0, n_tail, tail_body, run)
    acc = acc_sc[...]
    ot = jnp.concatenate([acc[:HEAD_DIM, h * tq:(h + 1) * tq] / acc[HEAD_DIM:HEAD_DIM + 1, h * tq:(h + 1) * tq]
                          for h in range(N_HEADS)], axis=0)
    o_ref[0] = ot.T.astype(BF16)


def _dsa_attention(qt, qit, wit, k, vt, ki, farq, nb, *, qpos0, n_sel):
    bsz, hq, lq = qt.shape
    lk = k.shape[1]
    hi = qit.shape[1]
    hd = N_HEADS * HEAD_DIM
    col = lambda b, i: (b, 0, i)
    whole = lambda b, i: (b, 0, 0)
    kern = functools.partial(_dsa_kernel, qpos0=qpos0, n_sel=n_sel, nk_tiles=lk // DSA_TK)
    return pl.pallas_call(
        kern,
        grid=(bsz, lq // DSA_TQ),
        in_specs=[pl.BlockSpec((1, hq, DSA_TQ), col),
                  pl.BlockSpec((1, hi, DSA_TQ), col),
                  pl.BlockSpec((1, C_IDX_HEADS, DSA_TQ), col),
                  pl.BlockSpec((1, lk, LANES), whole),
                  pl.BlockSpec((1, V_ROWS, lk), whole),
                  pl.BlockSpec((1, lk, C_IDX_DIM), whole),
                  pl.BlockSpec((LANES, N_HEADS * DSA_TQ), lambda b, i: (0, 0)),
                  pl.BlockSpec((3, DSA_TQ, N_HEADS * DSA_TQ), lambda b, i: (0, 0, 0))],
        out_specs=pl.BlockSpec((1, DSA_TQ, hd), lambda b, i: (b, i, 0)),
        out_shape=jax.ShapeDtypeStruct((bsz, lq, hd), BF16),
        scratch_shapes=[pltpu.VMEM((lk, DSA_TQ), I32),
                        pltpu.VMEM((1, N_HEADS * DSA_TQ), F32),
                        pltpu.VMEM((V_ROWS, N_HEADS * DSA_TQ), F32),
                        pltpu.VMEM((DSA_TK, N_HEADS * DSA_TQ), BF16),
                        pltpu.VMEM((1, N_HEADS * DSA_TQ), F32)],
        compiler_params=_params("parallel", "arbitrary"),
        name="dsa_attention",
    )(qt, qit, wit, k, vt, ki, farq, nb)


SWA_TQ = DSA_TQ


def _swa_proj_kernel(x_ref, g_ref, sc_ref, sh_ref, wqt_ref, wkv_ref, wvt_ref, ones_ref,
                     qt_ref, kv_ref, kb_ref, vt_ref):
    h = _modnorm(x_ref[0], g_ref[...], sc_ref[0], sh_ref[0]).astype(BF16)
    qt = lax.dot_general(wqt_ref[...], h, _NT, preferred_element_type=F32)
    qt_ref[0] = (qt * (HEAD_SCALE * LOG2E)).astype(BF16)
    kv = jnp.dot(h, wkv_ref[...], preferred_element_type=F32)
    kv_ref[0] = kv
    kb_ref[0] = kv[:, :D_KV_HEADS * HEAD_DIM].astype(BF16)
    vt = lax.dot_general(wvt_ref[...], h, _NT, preferred_element_type=F32) + ones_ref[...]
    vt_ref[0] = vt.astype(BF16)


def _swa_proj(x, g, sc, sh, w, tm):
    bsz, L, d = x.shape
    hd = N_HEADS * HEAD_DIM
    kw = D_KV_HEADS * HEAD_DIM
    vr = D_KV_HEADS * V_ROWS
    row = lambda b, i: (b, i, 0)
    col = lambda b, i: (b, 0, i)
    per_b = lambda b, i: (b, 0, 0)
    const = lambda b, i: (0, 0)
    return pl.pallas_call(
        _swa_proj_kernel,
        grid=(bsz, L // tm),
        in_specs=[pl.BlockSpec((1, tm, d), row),
                  pl.BlockSpec((1, d), const),
                  pl.BlockSpec((1, 1, d), per_b),
                  pl.BlockSpec((1, 1, d), per_b),
                  pl.BlockSpec((hd, d), const),
                  pl.BlockSpec((d, 2 * kw), const),
                  pl.BlockSpec((vr, d), const),
                  pl.BlockSpec((vr, 1), const)],
        out_specs=[pl.BlockSpec((1, hd, tm), col), pl.BlockSpec((1, tm, 2 * kw), row),
                   pl.BlockSpec((1, tm, kw), row), pl.BlockSpec((1, vr, tm), col)],
        out_shape=[jax.ShapeDtypeStruct((bsz, hd, L), BF16),
                   jax.ShapeDtypeStruct((bsz, L, 2 * kw), F32),
                   jax.ShapeDtypeStruct((bsz, L, kw), BF16),
                   jax.ShapeDtypeStruct((bsz, vr, L), BF16)],
        compiler_params=_params("parallel", "parallel"),
        name="swa_proj",
    )(x, g, sc, sh, w["d_wqt"], w["d_wkv"], w["d_wvt"], w["ones_rows"][:vr])


def _swa_kernel(qt_ref, kp_ref, ko_ref, vp_ref, vo_ref, nb_ref, sink_ref, o_ref, *, tile_off):
    tq = SWA_TQ
    it = pl.program_id(1) + tile_off
    r = lax.broadcasted_iota(I32, (tq, tq), 0)
    c = lax.broadcasted_iota(I32, (tq, tq), 1)
    qch = c >> CHUNK_SHIFT
    kch = r >> CHUNK_SHIFT
    nch = tq // CHUNK
    ok_prev = ((kch - nch) >= (qch - N_WIN_CHUNKS)) & ((it - 1) * tq + r >= 0)
    ok_own = kch <= qch
    qt = qt_ref[0]
    for g in range(D_KV_HEADS):
        qs = jnp.concatenate([qt[(g * D_REP + u) * HEAD_DIM:(g * D_REP + u + 1) * HEAD_DIM] for u in range(D_REP)],
                             axis=1)
        ksl = slice(g * HEAD_DIM, (g + 1) * HEAD_DIM)
        sp = jnp.dot(kp_ref[0][:, ksl], qs, preferred_element_type=F32)
        so = jnp.dot(ko_ref[0][:, ksl], qs, preferred_element_type=F32)
        pps, pos, ms = [], [], []
        for u in range(D_REP):
            hh = g * D_REP + u
            sl = slice(u * tq, (u + 1) * tq)
            hsl = slice(hh * tq, (hh + 1) * tq)
            a = jnp.where(ok_prev, sp[:, sl] + nb_ref[0, :, hsl], NEG_INF)
            b = jnp.where(ok_own, so[:, sl] + nb_ref[1, :, hsl], NEG_INF)
            m = jnp.maximum(jnp.maximum(jnp.max(a, axis=0, keepdims=True), jnp.max(b, axis=0, keepdims=True)),
                            sink_ref[:, hsl])
            pps.append(jnp.exp2(a - m).astype(BF16))
            pos.append(jnp.exp2(b - m).astype(BF16))
            ms.append(m)
        vsl = slice(g * V_ROWS, (g + 1) * V_ROWS)
        acc = (jnp.dot(vp_ref[0][vsl], jnp.concatenate(pps, axis=1), preferred_element_type=F32)
               + jnp.dot(vo_ref[0][vsl], jnp.concatenate(pos, axis=1), preferred_element_type=F32))
        for u in range(D_REP):
            hh = g * D_REP + u
            sl = slice(u * tq, (u + 1) * tq)
            den = acc[HEAD_DIM:HEAD_DIM + 1, sl] + jnp.exp2(sink_ref[:, hh * tq:(hh + 1) * tq] - ms[u])
            o_ref[0, hh * HEAD_DIM:(hh + 1) * HEAD_DIM, :] = (acc[:HEAD_DIM, sl] / den).astype(BF16)


def _swa_attention(qt, kb, vt, nb, sinks, *, tile_off):
    bsz, hd, lq = qt.shape
    kw = kb.shape[2]
    vr = vt.shape[1]
    tq = SWA_TQ
    prev = lambda i: jnp.maximum(i + tile_off - 1, 0)
    return pl.pallas_call(
        functools.partial(_swa_kernel, tile_off=tile_off),
        grid=(bsz, lq // tq),
        in_specs=[pl.BlockSpec((1, hd, tq), lambda b, i: (b, 0, i)),
                  pl.BlockSpec((1, tq, kw), lambda b, i: (b, prev(i), 0)),
                  pl.BlockSpec((1, tq, kw), lambda b, i: (b, i + tile_off, 0)),
                  pl.BlockSpec((1, vr, tq), lambda b, i: (b, 0, prev(i))),
                  pl.BlockSpec((1, vr, tq), lambda b, i: (b, 0, i + tile_off)),
                  pl.BlockSpec((2, tq, N_HEADS * tq), lambda b, i: (0, 0, 0)),
                  pl.BlockSpec((1, N_HEADS * tq), lambda b, i: (0, 0))],
        out_specs=pl.BlockSpec((1, hd, tq), lambda b, i: (b, 0, i)),
        out_shape=jax.ShapeDtypeStruct((bsz, hd, lq), BF16),
        compiler_params=_params("parallel", "parallel"),
        name="swa_attention",
    )(qt, kb, kb, vt, vt, nb, sinks)


def _route_rows(s, sb):
    n, m = N_GROUPS, EXPERTS_PER_GROUP
    grp = []
    for g in range(n):
        x = sb[g * m:(g + 1) * m]
        best = None
        for a in range(m):
            for b in range(a + 1, m):
                pair = x[a] + x[b]
                best = pair if best is None else jnp.maximum(best, pair)
        grp.append(best)
    chosen_g = []
    taken = None
    for g in range(n):
        is_g = None
        for o in range(g + 1, n):
            c = grp[g] >= grp[o]
            is_g = c if is_g is None else (is_g & c)
        if is_g is None:
            is_g = ~taken
        elif taken is not None:
            is_g = is_g & (~taken)
        taken = is_g if taken is None else (taken | is_g)
        chosen_g.append(is_g)
    picked = []
    for e in range(N_EXPERTS):
        g, a = divmod(e, m)
        beaten = jnp.zeros(sb[e].shape, F32)
        for b in range(m):
            if b == a:
                continue
            o = g * m + b
            wins = (sb[o] > sb[e]) | ((sb[o] == sb[e]) & (b < a))
            beaten = beaten + wins.astype(F32)
        picked.append(chosen_g[g] & (beaten < 2.0))
    tops = [jnp.where(picked[e], s[e], 0.0) for e in range(N_EXPERTS)]
    denom = tops[0]
    for e in range(1, N_EXPERTS):
        denom = denom + tops[e]
    return [t / denom for t in tops]


def _post_mix_kernel(o_ref, wo_ref, x_ref, gt_ref, g_ref, sc_ref, sh_ref, wr_ref, br_ref,
                     x1_ref, h2_ref, gates_ref, *, o_transposed):
    mixed = lax.dot_general(o_ref[0], wo_ref[...], _TN if o_transposed else _NN, preferred_element_type=F32)
    x1 = x_ref[0] + gt_ref[0] * mixed
    x1_ref[0] = x1
    h2 = _modnorm(x1, g_ref[...], sc_ref[0], sh_ref[0])
    h2_ref[0] = h2.astype(BF16)
    logits = _dot3(wr_ref[...], h2, _NT)
    s = _sigmoid(logits)
    sb = s + br_ref[...]
    rows = _route_rows([s[e:e + 1] for e in range(N_EXPERTS)], [sb[e:e + 1] for e in range(N_EXPERTS)])
    gates_ref[0] = jnp.concatenate(rows, axis=0)


def _post_mix(o, wo, x, gt, g, sc, sh, wr_t, br, tm, o_transposed=False):
    bsz, L, d = x.shape
    hd = wo.shape[0]
    o_spec = (pl.BlockSpec((1, hd, tm), lambda b, i: (b, 0, i)) if o_transposed
              else pl.BlockSpec((1, tm, hd), lambda b, i: (b, i, 0)))
    row = lambda b, i: (b, i, 0)
    per_b = lambda b, i: (b, 0, 0)
    const = lambda b, i: (0, 0)
    return pl.pallas_call(
        functools.partial(_post_mix_kernel, o_transposed=o_transposed),
        grid=(bsz, L // tm),
        in_specs=[o_spec,
                  pl.BlockSpec((hd, d), const),
                  pl.BlockSpec((1, tm, d), row),
                  pl.BlockSpec((1, 1, d), per_b),
                  pl.BlockSpec((1, d), const),
                  pl.BlockSpec((1, 1, d), per_b),
                  pl.BlockSpec((1, 1, d), per_b),
                  pl.BlockSpec((N_EXPERTS, d), const),
                  pl.BlockSpec((N_EXPERTS, 1), const)],
        out_specs=[pl.BlockSpec((1, tm, d), row), pl.BlockSpec((1, tm, d), row),
                   pl.BlockSpec((1, N_EXPERTS, tm), lambda b, i: (b, 0, i))],
        out_shape=[jax.ShapeDtypeStruct((bsz, L, d), F32),
                   jax.ShapeDtypeStruct((bsz, L, d), BF16),
                   jax.ShapeDtypeStruct((bsz, N_EXPERTS, L), F32)],
        compiler_params=_params("parallel", "parallel"),
        name="post_mix_route",
    )(o, wo, x, gt, g, sc, sh, wr_t, br)


def _moe_kernel(x_ref, h_ref, gates_ref, gt_ref, wgu_ref, wd_ref, o_ref, acc_sc):
    e = pl.program_id(2)

    @pl.when(e == 0)
    def _():
        acc_sc[...] = jnp.zeros(acc_sc.shape, F32)

    h = h_ref[0]
    gates = gates_ref[0]
    lane = lax.broadcasted_iota(I32, gates.shape, 1)
    gus = [jnp.dot(h, wgu_ref[u], preferred_element_type=F32) for u in range(MOE_EXPERTS_PER_STEP)]
    acts = [((gu[:, :D_EXPERT] * _sigmoid(gu[:, :D_EXPERT])) * gu[:, D_EXPERT:]).astype(BF16) for gu in gus]
    ys = [jnp.dot(acts[u], wd_ref[u], preferred_element_type=F32) for u in range(MOE_EXPERTS_PER_STEP)]
    tot = None
    for u in range(MOE_EXPERTS_PER_STEP):
        ge = jnp.sum(jnp.where(lane == e * MOE_EXPERTS_PER_STEP + u, gates, 0.0), axis=1, keepdims=True)
        tot = ge * ys[u] if tot is None else tot + ge * ys[u]
    acc_sc[...] += tot

    @pl.when(e == N_EXPERTS // MOE_EXPERTS_PER_STEP - 1)
    def _():
        o_ref[0] = x_ref[0] + gt_ref[0] * acc_sc[...]


def _moe(x, h, gates, gt, wgu, wd, tm):
    bsz, L, d = x.shape
    row = lambda b, i, e: (b, i, 0)
    return pl.pallas_call(
        _moe_kernel,
        grid=(bsz, L // tm, N_EXPERTS // MOE_EXPERTS_PER_STEP),
        in_specs=[pl.BlockSpec((1, tm, d), row),
                  pl.BlockSpec((1, tm, d), row),
                  pl.BlockSpec((1, tm, N_EXPERTS), row),
                  pl.BlockSpec((1, 1, d), lambda b, i, e: (b, 0, 0)),
                  pl.BlockSpec((MOE_EXPERTS_PER_STEP, d, 2 * D_EXPERT), lambda b, i, e: (e, 0, 0)),
                  pl.BlockSpec((MOE_EXPERTS_PER_STEP, D_EXPERT, d), lambda b, i, e: (e, 0, 0))],
        out_specs=pl.BlockSpec((1, tm, d), row),
        out_shape=jax.ShapeDtypeStruct((bsz, L, d), F32),
        scratch_shapes=[pltpu.VMEM((tm, d), F32)],
        compiler_params=_params("parallel", "parallel", "arbitrary"),
        name="moe_experts",
    )(x, h, gates, gt, wgu, wd)


def _final_norm_kernel(x_ref, g_ref, o_ref):
    o_ref[0] = _rms(x_ref[0]) * g_ref[...]


def _final_norm(x, g, tm):
    bsz, L, d = x.shape
    row = lambda b, i: (b, i, 0)
    return pl.pallas_call(
        _final_norm_kernel,
        grid=(bsz, L // tm),
        in_specs=[pl.BlockSpec((1, tm, d), row), pl.BlockSpec((1, d), lambda b, i: (0, 0))],
        out_specs=pl.BlockSpec((1, tm, d), row),
        out_shape=jax.ShapeDtypeStruct((bsz, L, d), F32),
        compiler_params=_params("parallel", "parallel"),
        name="final_norm",
    )(x, g)


def _rot_cols(w):
    half = w.shape[-1] // 2
    return jnp.concatenate([-w[..., half:], w[..., :half]], axis=-1)


def _head_pad(w2d, width):
    w3 = w2d.reshape(w2d.shape[0], N_HEADS, HEAD_DIM)
    return jnp.pad(w3, ((0, 0), (0, 0), (0, width - HEAD_DIM))).reshape(w2d.shape[0], N_HEADS * width)


def _prep_weights(a_w_in, a_g_q, a_w_uq, a_g_kv, a_w_ukv, a_w_o, b_w_in, b_b_f, b_w_o,
                  c_w_in, c_w_o, d_w_in, d_sinks, d_w_o, moe_w_router, moe_b_router,
                  moe_w_gate, moe_w_up, moe_w_down):
    w = {}
    hd = N_HEADS * HEAD_DIM
    kr = a_w_in[:, A_Q_LORA + A_KV_LORA:]
    w["a_in"] = jnp.concatenate([a_w_in, _rot_cols(kr)], axis=1).astype(BF16)
    w["a_gq"] = a_g_q.reshape(1, -1)
    w["a_gkv"] = a_g_kv.reshape(1, -1)
    uq = a_w_uq.reshape(A_Q_LORA, N_HEADS, A_NOPE + A_ROPE)
    pad = LANES - A_NOPE - A_ROPE
    zq = lambda n: jnp.zeros((A_Q_LORA, N_HEADS, n), F32)
    w["a_wq"] = jnp.concatenate([uq, zq(pad)], axis=-1).reshape(A_Q_LORA, N_HEADS * LANES).astype(BF16)
    w["a_wqs"] = jnp.concatenate([zq(A_NOPE), _rot_cols(uq[..., A_NOPE:]), zq(pad)],
                                 axis=-1).reshape(A_Q_LORA, N_HEADS * LANES).astype(BF16)
    ukv = a_w_ukv.reshape(A_KV_LORA, N_HEADS, A_NOPE + A_V)
    w["a_wk"] = jnp.concatenate([ukv[..., :A_NOPE], jnp.zeros((A_KV_LORA, N_HEADS, LANES - A_NOPE), F32)],
                                axis=-1).reshape(A_KV_LORA, N_HEADS * LANES).astype(BF16)
    w["a_wv"] = ukv[..., A_NOPE:].reshape(A_KV_LORA, N_HEADS * A_V).astype(BF16)
    w["a_wvt"] = _head_pad(ukv[..., A_NOPE:].reshape(A_KV_LORA, N_HEADS * A_V), V_ROWS).T.astype(BF16)
    sel = np.zeros((A_ROPE, N_HEADS, LANES), np.float32)
    for r in range(A_ROPE):
        sel[r, :, A_NOPE + r] = 1.0
    w["a_sel"] = jnp.asarray(sel.reshape(A_ROPE, N_HEADS * LANES), BF16)
    w["a_wo"] = a_w_o.astype(BF16)
    ones_rows = np.zeros((N_HEADS, V_ROWS, 1), np.float32)
    ones_rows[:, HEAD_DIM, 0] = 1.0
    w["ones_rows"] = jnp.asarray(ones_rows.reshape(N_HEADS * V_ROWS, 1))
    w["b_wq"] = b_w_in[:, :hd].astype(BF16)
    w["b_wk"] = b_w_in[:, hd:2 * hd].astype(BF16)
    w["b_wv"] = b_w_in[:, 2 * hd:3 * hd].astype(BF16)
    w["b_wf"] = jnp.pad(b_w_in[:, 3 * hd:], ((0, 0), (0, LANES - N_HEADS))).astype(BF16)
    w["b_bf"] = b_b_f.reshape(1, N_HEADS)
    w["b_wo"] = b_w_o.astype(BF16)
    w["b_wqt"] = _head_pad(b_w_in[:, :hd], LANES).T.astype(BF16)
    w["b_wvt"] = _head_pad(b_w_in[:, 2 * hd:3 * hd], V_ROWS).T.astype(BF16)
    hi = C_IDX_HEADS * C_IDX_DIM
    w["c_wqt"] = _head_pad(c_w_in[:, :hd], LANES).T.astype(BF16)
    kv = c_w_in[:, hd:hd + 2 * HEAD_DIM]
    qi = c_w_in[:, hd + 2 * HEAD_DIM:hd + 2 * HEAD_DIM + hi]
    rest = c_w_in[:, hd + 2 * HEAD_DIM + hi:]
    w["c_wqit"] = qi.T.astype(BF16)
    sm = jnp.concatenate([kv, rest], axis=1)
    w["c_wsm"] = jnp.pad(sm, ((0, 0), (0, 2 * LANES - sm.shape[1]))).astype(BF16)
    w["c_wsmt"] = w["c_wsm"].T
    w["c_wo"] = c_w_o.astype(BF16)
    kw = D_KV_HEADS * HEAD_DIM
    w["d_wqt"] = d_w_in[:, :hd].T.astype(BF16)
    w["d_wkv"] = d_w_in[:, hd:].astype(BF16)
    dv = d_w_in[:, hd + kw:].reshape(-1, D_KV_HEADS, HEAD_DIM)
    dv = jnp.pad(dv, ((0, 0), (0, 0), (0, V_ROWS - HEAD_DIM))).reshape(-1, D_KV_HEADS * V_ROWS)
    w["d_wvt"] = dv.T.astype(BF16)
    w["d_sinks"] = jnp.repeat(d_sinks.astype(F32) * LOG2E, SWA_TQ).reshape(1, N_HEADS * SWA_TQ)
    w["d_wo"] = d_w_o.astype(BF16)
    w["wr_t"] = moe_w_router.T
    w["br"] = moe_b_router.reshape(N_EXPERTS, 1)
    w["wgu"] = jnp.concatenate([moe_w_gate, moe_w_up], axis=-1).astype(BF16)
    w["wd"] = moe_w_down.astype(BF16)
    return w


def _rope_tables(pos):
    half = A_ROPE // 2
    inv = ROPE_THETA ** (-jnp.arange(half, dtype=F32) / half)
    ang = pos.astype(F32)[:, None] * inv[None, :]
    cos, sin = jnp.cos(ang), jnp.sin(ang)
    n = pos.shape[0]
    pad = LANES - A_NOPE - A_ROPE
    cosp = jnp.concatenate([jnp.ones((n, A_NOPE), F32), cos, cos, jnp.zeros((n, pad), F32)], axis=1)
    sinp = jnp.concatenate([jnp.zeros((n, A_NOPE), F32), sin, sin, jnp.zeros((n, pad), F32)], axis=1)
    return cosp, sinp


def _pad_rows(a, n):
    return jnp.pad(a, ((0, 0), (0, n - a.shape[1])) + ((0, 0),) * (a.ndim - 2))


def _round_up(n, m):
    return (n + m - 1) // m * m


def kernel(x_prompt, x_sample, c_prompt, c_sample, cache_a_latent, cache_a_krope, cache_b_k, cache_b_v, cache_b_logf, cache_c_k, cache_c_v, cache_c_kidx, cache_d_k, cache_d_v, w_ada, b_ada, g_mix, g_ffn, g_final, rel_bias, a_w_in, a_g_q, a_w_uq, a_g_kv, a_w_ukv, a_w_o, b_w_in, b_b_f, b_w_o, c_w_in, c_w_o, d_w_in, d_sinks, d_w_o, moe_w_router, moe_b_router, moe_w_gate, moe_w_up, moe_w_down):
    bp, S, d = x_prompt.shape
    bs, Ls, _ = x_sample.shape
    P = cache_a_latent.shape[1]
    depth = w_ada.shape[0]
    hd = N_HEADS * HEAD_DIM
    assert S % PROJ_ROWS == 0 and Ls == CHUNK and P % DSA_TQ == 0
    assert DSA_SEARCH_ROWS % DSA_TK == 0 and S % DSA_SEARCH_ROWS == 0
    assert S % FLASH_T_TILE == 0 and FLASH_T_TILE % CHUNK == 0 and S % FOX_PROJ_ROWS == 0
    tm_p, tm_s = PROJ_ROWS, Ls
    lk_s = _round_up(P + Ls, DSA_SEARCH_ROWS)

    w = _prep_weights(a_w_in, a_g_q, a_w_uq, a_g_kv, a_w_ukv, a_w_o, b_w_in, b_b_f, b_w_o,
                      c_w_in, c_w_o, d_w_in, d_sinks, d_w_o, moe_w_router, moe_b_router,
                      moe_w_gate, moe_w_up, moe_w_down)
    mod = _adaln(jnp.concatenate([c_prompt, c_sample], axis=0), w_ada, b_ada)
    farq, nb_dsa, nb_swa = _bias_tiles(rel_bias)

    def mods(i, lo, hi):
        return [mod[i, lo:hi, k * d:(k + 1) * d][:, None, :] for k in range(6)]

    xp, xs = x_prompt, x_sample
    outs = {}
    for i in range(depth):
        sh1_p, sc1_p, gt1_p, sh2_p, sc2_p, gt2_p = mods(i, 0, bp)
        sh1_s, sc1_s, gt1_s, sh2_s, sc2_s, gt2_s = mods(i, bp, bp + bs)
        g1 = g_mix[i].reshape(1, d)
        mixer = i % 4
        if mixer == 0:
            cos_p, sin_p = _rope_tables(jnp.arange(S))
            cos_s, sin_s = _rope_tables(P + jnp.arange(Ls))
            qt_p, lat_p, kr_p = _mla_proj_t(xp, g1, sc1_p, sh1_p, w, cos_p, sin_p, tm_p)
            k_p, vt_p = _mla_expand_t(lat_p, kr_p, w, tm_p)
            op = _flash_t(qt_p, k_p, vt_p, per_frame=False)
            q_s, lat_s, kr_s = _mla_proj(xs, g1, sc1_s, sh1_s, w, cos_s, sin_s, tm_s)
            lat_all = _pad_rows(jnp.concatenate([cache_a_latent, lat_s], axis=1), lk_s)
            kr_all = _pad_rows(jnp.concatenate([cache_a_krope, kr_s], axis=1), lk_s)
            k_s, v_s = _mla_expand(lat_all, kr_all, w, DSA_TK)
            os_ = _flash(q_s, k_s, v_s, tq=Ls, tk=lk_s, qpos0=P, per_frame=False, wide=True)
            outs["a"] = (lat_p, kr_p, lat_s, kr_s)
            wo = w["a_wo"]
        elif mixer == 1:
            qt_p, k_p, v_p, vt_p, lf_p, lft_p = _fox_proj_t(xp, g1, sc1_p, sh1_p, w, FOX_PROJ_ROWS)
            cumt_p, kaug_p = _fox_cumaug(lft_p, lf_p, k_p, CUMSUM_TILE)
            op = _flash_t(qt_p, kaug_p, vt_p, per_frame=True, qd=cumt_p)
            q_s, k_s, v_s, kb_s, vb_s, lf_s = _fox_proj(xs, g1, sc1_s, sh1_s, w, tm_s)
            lf_all = jnp.concatenate([cache_b_logf.astype(F32), lf_s], axis=1)
            cum_s = _cumsum_rows(_pad_rows(lf_all, lk_s).swapaxes(1, 2), CUMSUM_TILE)
            kb_all = _pad_rows(jnp.concatenate([cache_b_k.reshape(bs, P, hd).astype(BF16), kb_s], axis=1), lk_s)
            vb_all = _pad_rows(jnp.concatenate([cache_b_v.reshape(bs, P, hd).astype(BF16), vb_s], axis=1), lk_s)
            os_ = _flash(q_s, kb_all, vb_all, tq=Ls, tk=lk_s, qpos0=P, per_frame=True, wide=False,
                         qd=cum_s[:, :, P:P + Ls, None], kd=cum_s[:, :, None, :])
            shp = lambda a: a.reshape(a.shape[0], a.shape[1], N_HEADS, HEAD_DIM)
            outs["b"] = (shp(k_p), shp(v_p), lf_p, shp(k_s), shp(v_s), lf_s)
            wo = w["b_wo"]
        elif mixer == 2:
            qt_p, qit_p, k_p, v_p, ki_p, wit_p, kaug_p, vt_p, kib_p = _dsa_proj(xp, g1, sc1_p, sh1_p, w, tm_p)
            op = _dsa_attention(qt_p, qit_p, wit_p, kaug_p, vt_p, kib_p, farq, nb_dsa, qpos0=0,
                                n_sel=min(C_TOPK_MAX, S // 4))
            qt_s, qit_s, k_s, v_s, ki_s, wit_s, kaug_s, vt_s, kib_s = _dsa_proj(xs, g1, sc1_s, sh1_s, w, tm_s)
            pad_q = lambda a: jnp.pad(a, ((0, 0), (0, 0), (0, DSA_TQ - Ls)))
            ones_cols = jnp.zeros((bs, P, LANES - HEAD_DIM), BF16).at[:, :, :2].set(1.0)
            kaug_c = jnp.concatenate([cache_c_k.astype(BF16), ones_cols], axis=2)
            kaug_all = _pad_rows(jnp.concatenate([kaug_c, kaug_s], axis=1), lk_s)
            ones_row = jnp.zeros((bs, V_ROWS - HEAD_DIM, P), BF16).at[:, 0, :].set(1.0)
            vt_c = jnp.concatenate([jnp.swapaxes(cache_c_v, 1, 2).astype(BF16), ones_row], axis=1)
            vt_all = jnp.pad(jnp.concatenate([vt_c, vt_s], axis=2), ((0, 0), (0, 0), (0, lk_s - P - Ls)))
            ki_all = _pad_rows(jnp.concatenate([cache_c_kidx.astype(BF16), kib_s], axis=1), lk_s)
            os_ = _dsa_attention(pad_q(qt_s), pad_q(qit_s), pad_q(wit_s), kaug_all, vt_all, ki_all,
                                 farq, nb_dsa, qpos0=P, n_sel=min(C_TOPK_MAX, (P + Ls) // 4))[:, :Ls]
            outs["c"] = (k_p, v_p, ki_p, k_s, v_s, ki_s)
            wo = w["c_wo"]
        else:
            kvw = D_KV_HEADS * HEAD_DIM
            qt_p, kv_p, kb_p, vt_p = _swa_proj(xp, g1, sc1_p, sh1_p, w, tm_p)
            op = _swa_attention(qt_p, kb_p, vt_p, nb_swa, w["d_sinks"], tile_off=0)
            qt_s, kv_s, kb_s, vt_s = _swa_proj(xs, g1, sc1_s, sh1_s, w, tm_s)
            wc = cache_d_k.shape[1]
            assert wc == WINDOW == SWA_TQ
            ck = cache_d_k.reshape(bs, wc, kvw)
            cv = cache_d_v.reshape(bs, wc, kvw)
            kb_all = _pad_rows(jnp.concatenate([ck.astype(BF16), kb_s], axis=1), 2 * SWA_TQ)
            cvt = jnp.swapaxes(cache_d_v, 1, 3).swapaxes(1, 2)
            cvt = jnp.pad(cvt, ((0, 0), (0, 0), (0, V_ROWS - HEAD_DIM), (0, 0))).at[:, :, HEAD_DIM, :].set(1.0)
            vt_all = jnp.concatenate([cvt.reshape(bs, D_KV_HEADS * V_ROWS, wc).astype(BF16), vt_s], axis=2)
            vt_all = jnp.pad(vt_all, ((0, 0), (0, 0), (0, 2 * SWA_TQ - wc - Ls)))
            qt_s = jnp.pad(qt_s, ((0, 0), (0, 0), (0, SWA_TQ - Ls)))
            os_ = _swa_attention(qt_s, kb_all, vt_all, nb_swa, w["d_sinks"], tile_off=1)[:, :, :Ls]
            keep = min(WINDOW, S)
            shp = lambda a: a.reshape(a.shape[0], a.shape[1], D_KV_HEADS, HEAD_DIM)
            k_roll = jnp.concatenate([ck, kv_s[..., :kvw]], axis=1)[:, Ls:]
            v_roll = jnp.concatenate([cv, kv_s[..., kvw:]], axis=1)[:, Ls:]
            outs["d"] = (shp(kv_p[:, S - keep:, :kvw]), shp(kv_p[:, S - keep:, kvw:]), shp(k_roll), shp(v_roll))
            wo = w["d_wo"]

        g2 = g_ffn[i].reshape(1, d)
        x1_p, h2_p, gates_p = _post_mix(op, wo, xp, gt1_p, g2, sc2_p, sh2_p, w["wr_t"], w["br"], tm_p,
                                        o_transposed=(mixer != 2))
        xp = _moe(x1_p, h2_p, jnp.swapaxes(gates_p, 1, 2), gt2_p, w["wgu"][i], w["wd"][i], tm_p)
        x1_s, h2_s, gates_s = _post_mix(os_, wo, xs, gt1_s, g2, sc2_s, sh2_s, w["wr_t"], w["br"], tm_s,
                                        o_transposed=(mixer == 3))
        xs = _moe(x1_s, h2_s, jnp.swapaxes(gates_s, 1, 2), gt2_s, w["wgu"][i], w["wd"][i], tm_s)

    gf = g_final.reshape(1, d)
    y_p = _final_norm(xp, gf, tm_p)
    y_s = _final_norm(xs, gf, tm_s)
    a_lat_p, a_kr_p, a_lat_s, a_kr_s = outs["a"]
    b_k_p, b_v_p, b_lf_p, b_k_s, b_v_s, b_lf_s = outs["b"]
    c_k_p, c_v_p, c_ki_p, c_k_s, c_v_s, c_ki_s = outs["c"]
    d_k_p, d_v_p, d_k_s, d_v_s = outs["d"]
    return (y_p, y_s,
            a_lat_p, a_kr_p, b_k_p, b_v_p, b_lf_p, c_k_p, c_v_p, c_ki_p, d_k_p, d_v_p,
            a_lat_s, a_kr_s, b_k_s, b_v_s, b_lf_s, c_k_s, c_v_s, c_ki_s, d_k_s, d_v_s)
```
